```python
import jax, jax.numpy as jnp
from jax import lax
import numpy as np

D_MODEL = 2048
BATCH = 1
SEQ = 8192
DEPTH = 1

N_META = 16
BLOCK = 128
PAD_LEN = BLOCK - N_META
MLA_HEADS = 16
Q_LORA_RANK = 1536
KV_LORA_RANK = 512
QK_NOPE_DIM = 128
QK_ROPE_DIM = 64
QK_HEAD_DIM = QK_NOPE_DIM + QK_ROPE_DIM
V_HEAD_DIM = 128
MLA_WIDTH = MLA_HEADS * V_HEAD_DIM
ROPE_THETA = 10000.0
HGRN_HEADS = 16
HGRN_EXPAND = 128
HGRN_KEY_WIDTH = HGRN_HEADS * HGRN_EXPAND
HGRN_V_DIM = D_MODEL // HGRN_HEADS
HGRN_V_WIDTH = HGRN_HEADS * HGRN_V_DIM
D_FF = 5632
CONV_WIDTH = 3
NORM_EPS = 1e-6
IN_WIDTHS = (Q_LORA_RANK, KV_LORA_RANK, QK_ROPE_DIM,
             HGRN_KEY_WIDTH, HGRN_KEY_WIDTH, HGRN_V_WIDTH,
             HGRN_V_WIDTH,
             D_MODEL, D_MODEL)
IN_COLS = sum(IN_WIDTHS)

kernel_name = "hybrid_mla_hgrn2_convffn_block"


def _rms_norm(x, g):
    xf = x.astype(jnp.float32)
    y = xf * lax.rsqrt(jnp.mean(xf * xf, axis=-1, keepdims=True) + NORM_EPS)
    return (y * g.astype(jnp.float32)).astype(x.dtype)


def _rope_tables(pos):
    inv = 1.0 / (ROPE_THETA ** (jnp.arange(0, QK_ROPE_DIM, 2, dtype=jnp.float32) / QK_ROPE_DIM))
    ang = pos.astype(jnp.float32)[..., None] * inv
    ang = jnp.concatenate([ang, ang], axis=-1)
    return jnp.cos(ang), jnp.sin(ang)


def _apply_rope(x, cos, sin):
    xf = x.astype(jnp.float32)
    x1, x2 = jnp.split(xf, 2, axis=-1)
    rot = jnp.concatenate([-x2, x1], axis=-1)
    return (xf * cos + rot * sin).astype(x.dtype)


def _mla_attention(q_nope, q_rope, k_nope, k_rope, v, valid):
    B, L, H, _ = q_nope.shape
    n_blocks = L // BLOCK
    scale = QK_HEAD_DIM ** -0.5
    key_idx = jnp.arange(L)

    def one_block(i):
        start = i * BLOCK
        qn = lax.dynamic_slice_in_dim(q_nope, start, BLOCK, axis=1)
        qr = lax.dynamic_slice_in_dim(q_rope, start, BLOCK, axis=1)
        s = (jnp.einsum('bqhd,bkhd->bhqk', qn, k_nope, preferred_element_type=jnp.float32)
             + jnp.einsum('bqhr,bkr->bhqk', qr, k_rope, preferred_element_type=jnp.float32)) * scale
        q_idx = start + jnp.arange(BLOCK)
        causal = key_idx[None, :] <= q_idx[:, None]
        visible = valid[None, :] | (key_idx[None, :] == q_idx[:, None])
        s = jnp.where(causal & visible, s, -jnp.inf)
        p = jax.nn.softmax(s, axis=-1)
        return jnp.einsum('bhqk,bkhd->bqhd', p.astype(v.dtype), v)

    out = lax.map(one_block, jnp.arange(n_blocks))
    return jnp.transpose(out, (1, 0, 2, 3, 4)).reshape(B, L, H * v.shape[-1])


def _hgrn2_chunked(q, k, v, log_f):
    B, L, H, DK = q.shape
    DV = v.shape[-1]
    n_chunks = L // BLOCK

    def to_chunks(t):
        return jnp.transpose(t.reshape(B, n_chunks, BLOCK, H, t.shape[-1]), (1, 0, 3, 2, 4))

    causal = jnp.tril(jnp.ones((BLOCK, BLOCK), dtype=bool))

    def step(S, inp):
        qc, kc, vc, gc = inp
        b = jnp.cumsum(gc, axis=2)
        diff = b[:, :, :, None, :] - b[:, :, None, :, :]
        decay = jnp.exp(jnp.where(causal[:, :, None], diff, -jnp.inf))
        scores = jnp.einsum('bhtd,bhtsd,bhsd->bhts', qc, decay, kc)
        o = (jnp.einsum('bhts,bhse->bhte', scores, vc)
             + jnp.einsum('bhtd,bhde->bhte', qc * jnp.exp(b), S))
        b_last = b[:, :, -1:, :]
        S = (jnp.exp(b_last[:, :, 0, :])[..., None] * S
             + jnp.einsum('bhsd,bhse->bhde', kc * jnp.exp(b_last - b), vc))
        return S, o

    S0 = jnp.zeros((B, H, DK, DV), jnp.float32)
    _, o = lax.scan(step, S0, (to_chunks(q), to_chunks(k), to_chunks(v), to_chunks(log_f)))
    return jnp.transpose(o, (1, 0, 3, 2, 4)).reshape(B, L, H, DV)


def setup_inputs(seed: int = 0) -> dict:
    key = jax.random.key(seed)
    ks = jax.random.split(key, 24)
    f32 = jnp.float32

    def w(k, shape, fan_in):
        return jax.random.normal(k, shape, f32) * (fan_in ** -0.5)

    def gain(k, shape):
        return 1.0 + 0.02 * jax.random.normal(k, shape, f32)

    return {
        "x": jax.random.normal(ks[0], (BATCH, SEQ, D_MODEL), f32),
        "positions": jnp.broadcast_to(jnp.arange(SEQ, dtype=jnp.int32)[None, :], (BATCH, SEQ)),
        "meta_tokens": jax.random.normal(ks[1], (N_META, D_MODEL), f32),
        "w_in": w(ks[2], (DEPTH, D_MODEL, IN_COLS), D_MODEL),
        "w_q_up": w(ks[3], (DEPTH, Q_LORA_RANK, MLA_HEADS * QK_HEAD_DIM), Q_LORA_RANK),
        "w_kv_up": w(ks[4], (DEPTH, KV_LORA_RANK, MLA_HEADS * (QK_NOPE_DIM + V_HEAD_DIM)), KV_LORA_RANK),
        "w_branch_mla": w(ks[5], (DEPTH, MLA_WIDTH, D_MODEL), MLA_WIDTH),
        "w_branch_hgrn": w(ks[6], (DEPTH, HGRN_V_WIDTH, D_MODEL), HGRN_V_WIDTH),
        "w_out": w(ks[7], (DEPTH, D_MODEL, D_MODEL), D_MODEL),
        "w_ffn_in": w(ks[8], (DEPTH, D_MODEL, 2 * D_FF), D_MODEL),
        "w_ffn_out": w(ks[9], (DEPTH, D_FF, D_MODEL), D_FF),
        "conv_w": w(ks[10], (DEPTH, CONV_WIDTH, D_FF), CONV_WIDTH),
        "conv_b": 0.01 * jax.random.normal(ks[11], (DEPTH, D_FF), f32),
        "g_mix_norm": gain(ks[12], (DEPTH, D_MODEL)),
        "g_q_norm": gain(ks[13], (DEPTH, Q_LORA_RANK)),
        "g_kv_norm": gain(ks[14], (DEPTH, KV_LORA_RANK)),
        "g_hgrn_norm": gain(ks[15], (DEPTH, HGRN_V_DIM)),
        "g_ffn_norm": gain(ks[16], (DEPTH, D_MODEL)),
        "g_final_norm": gain(ks[17], (D_MODEL,)),
        "lb_raw": 1.0 + 0.1 * jax.random.normal(ks[18], (DEPTH + 1, HGRN_KEY_WIDTH), f32),
    }


def reference(x, positions, meta_tokens, w_in, w_q_up, w_kv_up, w_branch_mla, w_branch_hgrn,
              w_out, w_ffn_in, w_ffn_out, conv_w, conv_b, g_mix_norm, g_q_norm, g_kv_norm,
              g_hgrn_norm, g_ffn_norm, g_final_norm, lb_raw):
    B, S, D = x.shape
    dt = x.dtype
    prefix = PAD_LEN + N_META
    L = prefix + S

    h = jnp.concatenate([jnp.zeros((B, PAD_LEN, D), dt),
                         jnp.broadcast_to(meta_tokens.astype(dt)[None], (B, N_META, D)),
                         x], axis=1)
    valid = jnp.arange(L) >= PAD_LEN
    pos = jnp.concatenate([jnp.zeros((B, PAD_LEN), jnp.int32),
                           jnp.broadcast_to(jnp.arange(N_META, dtype=jnp.int32)[None], (B, N_META)),
                           positions.astype(jnp.int32) + N_META], axis=1)
    cos, sin = _rope_tables(pos)
    split_pts = np.cumsum(IN_WIDTHS)[:-1].tolist()
    lb_all = jnp.cumsum(jax.nn.softmax(lb_raw.astype(jnp.float32), axis=0), axis=0)

    for layer in range(DEPTH):
        u = _rms_norm(h, g_mix_norm[layer])
        proj = u @ w_in[layer]
        q_lat, kv_lat, k_rope, hq, hf, hi, hg, gate_a, gate_b = jnp.split(proj, split_pts, axis=-1)

        q = (_rms_norm(q_lat, g_q_norm[layer]) @ w_q_up[layer]).reshape(B, L, MLA_HEADS, QK_HEAD_DIM)
        q_nope, q_rope = q[..., :QK_NOPE_DIM], q[..., QK_NOPE_DIM:]
        kv = (_rms_norm(kv_lat, g_kv_norm[layer]) @ w_kv_up[layer]).reshape(
            B, L, MLA_HEADS, QK_NOPE_DIM + V_HEAD_DIM)
        k_nope, v_mla = kv[..., :QK_NOPE_DIM], kv[..., QK_NOPE_DIM:]
        q_rope = _apply_rope(q_rope, cos[:, :, None, :], sin[:, :, None, :])
        k_rope = _apply_rope(k_rope, cos, sin)
        o_mla = _mla_attention(q_nope, q_rope, k_nope, k_rope, v_mla, valid)

        lb = lb_all[layer]
        f = lb + (1.0 - lb) * jax.nn.sigmoid(hf.astype(jnp.float32))
        vmask = valid[None, :, None]
        log_f = jnp.where(vmask, jnp.log(f), 0.0)
        k_in = jnp.where(vmask, 1.0 - f, 0.0)
        q_h = jax.nn.silu(hq.astype(jnp.float32))
        o_h = _hgrn2_chunked(q_h.reshape(B, L, HGRN_HEADS, HGRN_EXPAND),
                             k_in.reshape(B, L, HGRN_HEADS, HGRN_EXPAND),
                             hi.astype(jnp.float32).reshape(B, L, HGRN_HEADS, HGRN_V_DIM),
                             log_f.reshape(B, L, HGRN_HEADS, HGRN_EXPAND))
        o_h = _rms_norm(o_h, g_hgrn_norm[layer]) * jax.nn.silu(
            hg.astype(jnp.float32).reshape(B, L, HGRN_HEADS, HGRN_V_DIM))
        o_hgrn = o_h.reshape(B, L, HGRN_V_WIDTH).astype(dt)

        merged = (jax.nn.sigmoid(gate_a) * (o_mla @ w_branch_mla[layer])
                  + jax.nn.sigmoid(gate_b) * (o_hgrn @ w_branch_hgrn[layer]))
        h = h + merged @ w_out[layer]

        u = _rms_norm(h, g_ffn_norm[layer])
        gate, up = jnp.split(u @ w_ffn_in[layer], 2, axis=-1)
        gate = jnp.where(vmask, gate, 0.0)
        gp = jnp.pad(gate, ((0, 0), (CONV_WIDTH - 1, 0), (0, 0)))
        cw = conv_w[layer]
        conv = (cw[0] * gp[:, :-2] + cw[1] * gp[:, 1:-1] + cw[2] * gp[:, 2:]) + conv_b[layer]
        h = h + (jax.nn.silu(conv) * up) @ w_ffn_out[layer]

    out = _rms_norm(h, g_final_norm)
    return out[:, prefix:, :]
```

```python
import functools

import jax
import jax.numpy as jnp
from jax import lax
from jax.experimental import pallas as pl
from jax.experimental.pallas import tpu as pltpu

F32 = jnp.float32
BF16 = jnp.bfloat16

N_META = 16
BLOCK = 128
PAD_LEN = BLOCK - N_META
MLA_HEADS = 16
Q_LORA_RANK = 1536
KV_LORA_RANK = 512
QK_NOPE_DIM = 128
QK_ROPE_DIM = 64
QK_HEAD_DIM = QK_NOPE_DIM + QK_ROPE_DIM
V_HEAD_DIM = 128
ROPE_THETA = 10000.0
HGRN_HEADS = 16
HGRN_EXPAND = 128
HGRN_V_DIM = 128
D_FF = 5632
NORM_EPS = 1e-6

LANES = 128
Q_CAT = 2 * LANES
LAT_COLS = Q_LORA_RANK + KV_LORA_RANK + LANES
SUB = 16
VMEM_LIMIT = 52 * 1024 * 1024


def _cparams(n_axes):
    return pltpu.CompilerParams(dimension_semantics=("arbitrary",) * n_axes, vmem_limit_bytes=VMEM_LIMIT)


def _row_tile(rows, target):
    for t in range(target, 0, -LANES):
        if rows % t == 0:
            return t
    raise ValueError(f"no 128-multiple row tile for {rows}")


def _dot(a, b):
    return jnp.dot(a, b, preferred_element_type=F32)


def _dot_nt(a, b):
    return lax.dot_general(a, b, (((1,), (1,)), ((), ())), preferred_element_type=F32)


def _dot_tn(a, b):
    return lax.dot_general(a, b, (((0,), (0,)), ((), ())), preferred_element_type=F32)


def _rms(x, g):
    return x * lax.rsqrt(jnp.mean(x * x, axis=-1, keepdims=True) + NORM_EPS) * g


def _sigmoid(x):
    return 1.0 / (1.0 + jnp.exp(-x))


def _silu(x):
    return x * _sigmoid(x)


def _rope(x, cos_t, sin_t):
    rot = pltpu.roll(x, 32, 1) - pltpu.roll(x, 96, 1)
    return x * cos_t + rot * sin_t


def _rope_table_kernel(pos_ref, inv_ref, cos_ref, sin_ref):
    ang = pos_ref[...] * inv_ref[...]
    keep = lax.broadcasted_iota(jnp.int32, ang.shape, 1) < QK_ROPE_DIM
    cos_ref[...] = jnp.where(keep, jnp.cos(ang), 0.0)
    sin_ref[...] = jnp.where(keep, jnp.sin(ang), 0.0)


def _rope_tables(pos_f, inv):
    rows = pos_f.shape[0]
    tm = _row_tile(rows, 640)
    spec = pl.BlockSpec((tm, LANES), lambda i: (i, 0))
    return pl.pallas_call(
        _rope_table_kernel, name="rope_tables",
        grid=(rows // tm,),
        in_specs=[pl.BlockSpec((tm, 1), lambda i: (i, 0)), pl.BlockSpec((1, LANES), lambda i: (0, 0))],
        out_specs=[spec, spec],
        out_shape=[jax.ShapeDtypeStruct((rows, LANES), F32)] * 2,
        compiler_params=_cparams(1),
    )(pos_f, inv)


def _norm_cast_kernel(x_ref, g_ref, o_ref):
    o_ref[...] = _rms(x_ref[...], g_ref[...]).astype(o_ref.dtype)


def _norm_cast(x, g):
    rows, d = x.shape
    tm = _row_tile(rows, 640)
    return pl.pallas_call(
        _norm_cast_kernel, name="mix_norm",
        grid=(rows // tm,),
        in_specs=[pl.BlockSpec((tm, d), lambda i: (i, 0)), pl.BlockSpec((1, d), lambda i: (0, 0))],
        out_specs=pl.BlockSpec((tm, d), lambda i: (i, 0)),
        out_shape=jax.ShapeDtypeStruct((rows, d), BF16),
        compiler_params=_cparams(1),
    )(x, g)


def _proj_lat_kernel(u_ref, w_ref, gq_ref, gkv_ref, cos_ref, sin_ref, qn_ref, kvn_ref, kr_ref):
    y = _dot(u_ref[...], w_ref[...])
    kv_end = Q_LORA_RANK + KV_LORA_RANK
    qn_ref[...] = _rms(y[:, :Q_LORA_RANK], gq_ref[...]).astype(qn_ref.dtype)
    kvn_ref[...] = _rms(y[:, Q_LORA_RANK:kv_end], gkv_ref[...]).astype(kvn_ref.dtype)
    kr_ref[...] = _rope(y[:, kv_end:], cos_ref[...], sin_ref[...]).astype(kr_ref.dtype)


def _proj_lat(u, w_lat, g_q, g_kv, cos_t, sin_t):
    rows, d = u.shape
    tm = _row_tile(rows, 640)
    row_spec = lambda n: pl.BlockSpec((tm, n), lambda i: (i, 0))
    full_spec = lambda r, n: pl.BlockSpec((r, n), lambda i: (0, 0))
    return pl.pallas_call(
        _proj_lat_kernel, name="proj_latents",
        grid=(rows // tm,),
        in_specs=[row_spec(d), full_spec(d, LAT_COLS), full_spec(1, Q_LORA_RANK), full_spec(1, KV_LORA_RANK),
                  row_spec(LANES), row_spec(LANES)],
        out_specs=[row_spec(Q_LORA_RANK), row_spec(KV_LORA_RANK), row_spec(LANES)],
        out_shape=[jax.ShapeDtypeStruct((rows, Q_LORA_RANK), BF16),
                   jax.ShapeDtypeStruct((rows, KV_LORA_RANK), BF16),
                   jax.ShapeDtypeStruct((rows, LANES), BF16)],
        compiler_params=_cparams(1),
    )(u, w_lat, g_q, g_kv, cos_t, sin_t)


def _matmul_kernel(x_ref, w_ref, o_ref):
    o_ref[...] = _dot(x_ref[...], w_ref[...]).astype(o_ref.dtype)


def _matmul_cols(x, w, col_start, n_cols, out_dtype, tn, name):
    rows, k = x.shape
    tm = _row_tile(rows, 640)
    off = col_start // tn
    return pl.pallas_call(
        _matmul_kernel, name=name,
        grid=(n_cols // tn, rows // tm),
        in_specs=[pl.BlockSpec((tm, k), lambda n, m: (m, 0)), pl.BlockSpec((k, tn), lambda n, m: (0, n + off))],
        out_specs=pl.BlockSpec((tm, tn), lambda n, m: (m, n)),
        out_shape=jax.ShapeDtypeStruct((rows, n_cols), out_dtype),
        compiler_params=_cparams(2),
    )(x, w)


def _q_up_kernel(x_ref, w_ref, cos_ref, sin_ref, o_ref, *, heads_per_tile):
    y = _dot(x_ref[...], w_ref[...]) * (QK_HEAD_DIM ** -0.5)
    cos_t = cos_ref[...]
    sin_t = sin_ref[...]
    for hh in range(heads_per_tile):
        lo = hh * Q_CAT
        o_ref[:, lo:lo + LANES] = y[:, lo:lo + LANES].astype(o_ref.dtype)
        o_ref[:, lo + LANES:lo + Q_CAT] = _rope(y[:, lo + LANES:lo + Q_CAT], cos_t, sin_t).astype(o_ref.dtype)


def _q_up(qn, w_q, cos_t, sin_t):
    rows, k = qn.shape
    n_cols = w_q.shape[1]
    tm = _row_tile(rows, 640)
    tn = 4 * Q_CAT
    return pl.pallas_call(
        functools.partial(_q_up_kernel, heads_per_tile=tn // Q_CAT), name="q_up_rope",
        grid=(n_cols // tn, rows // tm),
        in_specs=[pl.BlockSpec((tm, k), lambda n, m: (m, 0)), pl.BlockSpec((k, tn), lambda n, m: (0, n)),
                  pl.BlockSpec((tm, LANES), lambda n, m: (m, 0)), pl.BlockSpec((tm, LANES), lambda n, m: (m, 0))],
        out_specs=pl.BlockSpec((tm, tn), lambda n, m: (m, n)),
        out_shape=jax.ShapeDtypeStruct((rows, n_cols), BF16),
        compiler_params=_cparams(2),
    )(qn, w_q, cos_t, sin_t)


def _attn_kernel(q_ref, kn_ref, v_ref, kr_ref, o_ref, kcat_ref, m_ref, l_ref, acc_ref, *, tq, n_real_tiles):
    qi = pl.program_id(1)
    s_real = n_real_tiles * tq

    @pl.when(qi == 0)
    def _():
        kcat_ref[:, :LANES] = kn_ref[...]
        kcat_ref[:, LANES:] = kr_ref[...]

    def init():
        m_ref[...] = jnp.full(m_ref.shape, -jnp.inf, F32)
        l_ref[...] = jnp.zeros(l_ref.shape, F32)
        acc_ref[...] = jnp.zeros(acc_ref.shape, F32)

    def update(s, v_blk):
        chunks = [s[:, c * LANES:(c + 1) * LANES] for c in range(s.shape[1] // LANES)]
        m_prev = m_ref[...]
        m_cur = functools.reduce(jnp.maximum, chunks)
        m_new = jnp.maximum(m_prev, jnp.max(m_cur, axis=1, keepdims=True))
        alpha = jnp.exp(m_prev - m_new)
        p_chunks = [jnp.exp(c - m_new) for c in chunks]
        l_cur = functools.reduce(jnp.add, p_chunks)
        l_ref[...] = alpha * l_ref[...] + jnp.sum(l_cur, axis=1, keepdims=True)
        p = jnp.concatenate(p_chunks, axis=1) if len(p_chunks) > 1 else p_chunks[0]
        acc_ref[...] = alpha * acc_ref[...] + _dot(p.astype(v_blk.dtype), v_blk)
        m_ref[...] = m_new

    def finalize():
        o_ref[...] = (acc_ref[...] / l_ref[...]).astype(o_ref.dtype)

    def prefix_scores():
        return _dot_nt(q_ref[...], kcat_ref[s_real:s_real + BLOCK, :])

    @pl.when(qi < n_real_tiles)
    def _():
        init()
        col = lax.broadcasted_iota(jnp.int32, (tq, BLOCK), 1)
        update(jnp.where(col >= PAD_LEN, prefix_scores(), -jnp.inf), v_ref[s_real:s_real + BLOCK, :])

        def body(j, carry):
            start = pl.multiple_of(j * tq, tq)
            update(_dot_nt(q_ref[...], kcat_ref[pl.ds(start, tq), :]), v_ref[pl.ds(start, tq), :])
            return carry

        lax.fori_loop(0, qi, body, 0)

        start = pl.multiple_of(qi * tq, tq)
        s = _dot_nt(q_ref[...], kcat_ref[pl.ds(start, tq), :])
        row = lax.broadcasted_iota(jnp.int32, (tq, tq), 0)
        col = lax.broadcasted_iota(jnp.int32, (tq, tq), 1)
        update(jnp.where(col <= row, s, -jnp.inf), v_ref[pl.ds(start, tq), :])
        finalize()

    @pl.when(qi == n_real_tiles)
    def _():
        init()
        row = lax.broadcasted_iota(jnp.int32, (tq, BLOCK), 0)
        col = lax.broadcasted_iota(jnp.int32, (tq, BLOCK), 1)
        visible = (col <= row) & ((col >= PAD_LEN) | (col == row))
        update(jnp.where(visible, prefix_scores(), -jnp.inf), v_ref[s_real:s_real + BLOCK, :])
        finalize()


def _attention(q_cat, kv, kr, s_real):
    rows = q_cat.shape[0]
    tq = 512
    n_real_tiles = s_real // tq
    return pl.pallas_call(
        functools.partial(_attn_kernel, tq=tq, n_real_tiles=n_real_tiles), name="mla_attention",
        grid=(MLA_HEADS, n_real_tiles + 1),
        in_specs=[pl.BlockSpec((tq, Q_CAT), lambda h, i: (i, h)),
                  pl.BlockSpec((rows, LANES), lambda h, i: (0, h)),
                  pl.BlockSpec((rows, LANES), lambda h, i: (0, MLA_HEADS + h)),
                  pl.BlockSpec((rows, LANES), lambda h, i: (0, 0))],
        out_specs=pl.BlockSpec((tq, V_HEAD_DIM), lambda h, i: (i, h)),
        out_shape=jax.ShapeDtypeStruct((rows, MLA_HEADS * V_HEAD_DIM), BF16),
        scratch_shapes=[pltpu.VMEM((rows, Q_CAT), BF16), pltpu.VMEM((tq, LANES), F32),
                        pltpu.VMEM((tq, LANES), F32), pltpu.VMEM((tq, V_HEAD_DIM), F32)],
        compiler_params=_cparams(2),
    )(q_cat, kv, kv, kr)


def _split3(x):
    x1 = x.astype(BF16)
    r1 = x - x1.astype(F32)
    x2 = r1.astype(BF16)
    x3 = (r1 - x2.astype(F32)).astype(BF16)
    return x1, x2, x3


def _group_row(x, j):
    c, n = x.shape
    g = x.reshape(c // SUB, SUB, n)[:, j:j + 1, :]
    return jnp.broadcast_to(g, (c // SUB, SUB, n)).reshape(c, n)


def _hgrn_kernel(hq_ref, hf_ref, hi_ref, hg_ref, lb_ref, g_ref, o_ref, st_ref):
    c = pl.program_id(1)
    n = BLOCK

    @pl.when(c == 0)
    def _():
        st_ref[...] = jnp.zeros(st_ref.shape, F32)

    lb_raw = lb_ref[...]
    top = jnp.max(lb_raw, axis=0, keepdims=True)
    e = jnp.exp(lb_raw - top)
    lb = e[0:1, :] / jnp.sum(e, axis=0, keepdims=True)

    row = lax.broadcasted_iota(jnp.int32, (n, n), 0)
    col = lax.broadcasted_iota(jnp.int32, (n, n), 1)
    valid = (c > 0) | (row >= PAD_LEN)

    f = lb + (1.0 - lb) * _sigmoid(hf_ref[...])
    g = jnp.where(valid, jnp.log(f), 0.0)
    k = jnp.where(valid, 1.0 - f, 0.0)
    q = _silu(hq_ref[...].astype(F32))
    v = hi_ref[...]

    tri = (col <= row).astype(BF16)
    g1, g2, g3 = _split3(g)
    b = _dot(tri, g1) + _dot(tri, g2) + _dot(tri, g3)

    b_end = _group_row(b, SUB - 1)
    k_hat = k * jnp.exp(b_end - b)
    a = jnp.zeros((n, n), F32)
    for jj in range(n // SUB - 1):
        hi_row = (jj + 1) * SUB
        anchor = b[hi_row - 1:hi_row, :]
        q_t = jnp.where(row >= hi_row, q * jnp.exp(jnp.where(row >= hi_row, b - anchor, 0.0)), 0.0)
        k_t = jnp.where((row >= jj * SUB) & (row < hi_row), k_hat, 0.0)
        a = a + _dot_nt(q_t.astype(BF16), k_t.astype(BF16))

    row_in = row % SUB
    row_base = row - row_in
    for j in range(SUB):
        ok = row_in >= j
        decay = jnp.exp(jnp.where(ok, b - _group_row(b, j), -jnp.inf))
        r = jnp.sum(q * decay * _group_row(k, j), axis=1, keepdims=True)
        a = a + jnp.where(col == row_base + j, r, 0.0)

    st = st_ref[...]
    o = _dot(a.astype(BF16), v) + _dot_nt((q * jnp.exp(b)).astype(BF16), st.astype(BF16))
    b_last = b[n - 1:n, :]
    st_ref[...] = st * jnp.exp(b_last) + _dot_tn(v, (k * jnp.exp(b_last - b)).astype(BF16))

    o_ref[...] = (_rms(o, g_ref[...]) * _silu(hg_ref[...].astype(F32))).astype(o_ref.dtype)


def _hgrn(hq, hf, hrest, lb_raw, g_hgrn):
    rows = hq.shape[0]
    n_chunks = rows // BLOCK
    blk = lambda off: pl.BlockSpec((BLOCK, BLOCK), lambda h, c: ((c + n_chunks - 1) % n_chunks, off + h))
    return pl.pallas_call(
        _hgrn_kernel, name="hgrn2",
        grid=(HGRN_HEADS, n_chunks),
        in_specs=[blk(0), blk(0), blk(0), blk(HGRN_HEADS),
                  pl.BlockSpec((lb_raw.shape[0], BLOCK), lambda h, c: (0, h)),
                  pl.BlockSpec((1, HGRN_V_DIM), lambda h, c: (0, 0))],
        out_specs=blk(0),
        out_shape=jax.ShapeDtypeStruct((rows, HGRN_HEADS * HGRN_V_DIM), BF16),
        scratch_shapes=[pltpu.VMEM((HGRN_V_DIM, HGRN_EXPAND), F32)],
        compiler_params=_cparams(2),
    )(hq, hf, hrest, hrest, lb_raw, g_hgrn)


def _merge_kernel(om_ref, oh_ref, wa_ref, wb_ref, ga_ref, gb_ref, o_ref):
    a = _dot(om_ref[...], wa_ref[...])
    b = _dot(oh_ref[...], wb_ref[...])
    o = _sigmoid(ga_ref[...].astype(F32)) * a + _sigmoid(gb_ref[...].astype(F32)) * b
    o_ref[...] = o.astype(o_ref.dtype)


def _merge(o_mla, o_hgrn, w_a, w_b, hrest, gate_off):
    rows, k = o_mla.shape
    d = w_a.shape[1]
    tm = _row_tile(rows, 640)
    tn = 512
    ga_off = gate_off // tn
    gb_off = (gate_off + d) // tn
    return pl.pallas_call(
        _merge_kernel, name="branch_merge",
        grid=(d // tn, rows // tm),
        in_specs=[pl.BlockSpec((tm, k), lambda n, m: (m, 0)), pl.BlockSpec((tm, k), lambda n, m: (m, 0)),
                  pl.BlockSpec((k, tn), lambda n, m: (0, n)), pl.BlockSpec((k, tn), lambda n, m: (0, n)),
                  pl.BlockSpec((tm, tn), lambda n, m: (m, n + ga_off)),
                  pl.BlockSpec((tm, tn), lambda n, m: (m, n + gb_off))],
        out_specs=pl.BlockSpec((tm, tn), lambda n, m: (m, n)),
        out_shape=jax.ShapeDtypeStruct((rows, d), BF16),
        compiler_params=_cparams(2),
    )(o_mla, o_hgrn, w_a, w_b, hrest, hrest)


def _out_proj_kernel(mg_ref, w_ref, h_ref, g_ref, h2_ref, u2_ref):
    h2 = h_ref[...] + _dot(mg_ref[...], w_ref[...])
    h2_ref[...] = h2
    u2_ref[...] = _rms(h2, g_ref[...]).astype(u2_ref.dtype)


def _out_proj(merged, w_out, h, g_ffn):
    rows, d = h.shape
    tm = _row_tile(rows, 320)
    row_spec = pl.BlockSpec((tm, d), lambda i: (i, 0))
    return pl.pallas_call(
        _out_proj_kernel, name="mix_out_proj",
        grid=(rows // tm,),
        in_specs=[row_spec, pl.BlockSpec((d, d), lambda i: (0, 0)), row_spec, pl.BlockSpec((1, d), lambda i: (0, 0))],
        out_specs=[row_spec, row_spec],
        out_shape=[jax.ShapeDtypeStruct((rows, d), F32), jax.ShapeDtypeStruct((rows, d), BF16)],
        compiler_params=_cparams(1),
    )(merged, w_out, h, g_ffn)


HALO = 16


def _ffn_in_kernel(u_ref, halo_ref, wg_ref, wu_ref, cw_ref, cb_ref, o_ref):
    u = u_ref[...]
    wg = wg_ref[...]
    gate = _dot(u, wg)
    up = _dot(u, wu_ref[...])
    prev = _dot(halo_ref[...], wg)
    row = lax.broadcasted_iota(jnp.int32, gate.shape, 0)
    back1 = jnp.where(row == 0, prev[HALO - 1:HALO, :], pltpu.roll(gate, 1, 0))
    back2 = jnp.where(row == 0, prev[HALO - 2:HALO - 1, :],
                      jnp.where(row == 1, prev[HALO - 1:HALO, :], pltpu.roll(gate, 2, 0)))
    cw = cw_ref[...]
    conv = cw[0:1, :] * back2 + cw[1:2, :] * back1 + cw[2:3, :] * gate + cb_ref[...]
    o_ref[...] = (_silu(conv) * up).astype(o_ref.dtype)


def _ffn_in(u2, w_fi, conv_w, conv_b, s_real):
    rows, d = u2.shape
    tm = 512
    tn = 512
    up_off = D_FF // tn
    n_halo = rows // HALO
    halo_map = lambda n, m: ((m * (tm // HALO) + n_halo - 1) % n_halo, 0)
    return pl.pallas_call(
        _ffn_in_kernel, name="ffn_in_conv_gate",
        grid=(D_FF // tn, s_real // tm),
        in_specs=[pl.BlockSpec((tm, d), lambda n, m: (m, 0)), pl.BlockSpec((HALO, d), halo_map),
                  pl.BlockSpec((d, tn), lambda n, m: (0, n)), pl.BlockSpec((d, tn), lambda n, m: (0, n + up_off)),
                  pl.BlockSpec((conv_w.shape[0], tn), lambda n, m: (0, n)), pl.BlockSpec((1, tn), lambda n, m: (0, n))],
        out_specs=pl.BlockSpec((tm, tn), lambda n, m: (m, n)),
        out_shape=jax.ShapeDtypeStruct((s_real, D_FF), BF16),
        compiler_params=_cparams(2),
    )(u2, u2, w_fi, w_fi, conv_w, conv_b)


def _ffn_out_kernel(a_ref, w_ref, h2_ref, g_ref, o_ref):
    kk = pl.program_id(1)

    @pl.when(kk == 0)
    def _():
        o_ref[...] = h2_ref[...]

    o_ref[...] += _dot(a_ref[...], w_ref[...])

    @pl.when(kk == pl.num_programs(1) - 1)
    def _():
        o_ref[...] = _rms(o_ref[...], g_ref[...])


def _ffn_out(act, w_fo, h2, g_final):
    s_real, k = act.shape
    d = w_fo.shape[1]
    tm = 512
    tk = 512
    return pl.pallas_call(
        _ffn_out_kernel, name="ffn_out_final_norm",
        grid=(s_real // tm, k // tk),
        in_specs=[pl.BlockSpec((tm, tk), lambda m, kk: (m, kk)), pl.BlockSpec((tk, d), lambda m, kk: (kk, 0)),
                  pl.BlockSpec((tm, d), lambda m, kk: (m, 0)), pl.BlockSpec((1, d), lambda m, kk: (0, 0))],
        out_specs=pl.BlockSpec((tm, d), lambda m, kk: (m, 0)),
        out_shape=jax.ShapeDtypeStruct((s_real, d), F32),
        compiler_params=_cparams(2),
    )(act, w_fo, h2, g_final)


def kernel(x, positions, meta_tokens, w_in, w_q_up, w_kv_up, w_branch_mla, w_branch_hgrn, w_out, w_ffn_in,
           w_ffn_out, conv_w, conv_b, g_mix_norm, g_q_norm, g_kv_norm, g_hgrn_norm, g_ffn_norm, g_final_norm,
           lb_raw):
    b, s_real, d = x.shape
    assert b == 1 and w_in.shape[0] == 1 and s_real % 512 == 0
    dt = x.dtype

    h = jnp.concatenate([x[0], jnp.zeros((PAD_LEN, d), dt), meta_tokens.astype(dt)], axis=0)
    pos = jnp.concatenate([positions[0].astype(jnp.int32) + N_META, jnp.zeros((PAD_LEN,), jnp.int32),
                           jnp.arange(N_META, dtype=jnp.int32)])
    inv = 1.0 / (ROPE_THETA ** (jnp.arange(0, QK_ROPE_DIM, 2, dtype=F32) / QK_ROPE_DIM))
    inv = jnp.concatenate([inv, inv, jnp.zeros((LANES - QK_ROPE_DIM,), F32)])[None, :]

    lat_end = Q_LORA_RANK + KV_LORA_RANK + QK_ROPE_DIM
    w_lat = jnp.pad(w_in[0, :, :lat_end], ((0, 0), (0, LANES - QK_ROPE_DIM))).astype(BF16)
    w_rest = w_in[0, :, lat_end:].astype(BF16)
    w_q = jnp.pad(w_q_up[0].reshape(Q_LORA_RANK, MLA_HEADS, QK_HEAD_DIM),
                  ((0, 0), (0, 0), (0, Q_CAT - QK_HEAD_DIM))).reshape(Q_LORA_RANK, MLA_HEADS * Q_CAT).astype(BF16)
    w_kv3 = w_kv_up[0].reshape(KV_LORA_RANK, MLA_HEADS, QK_NOPE_DIM + V_HEAD_DIM)
    w_kv = jnp.concatenate([w_kv3[:, :, :QK_NOPE_DIM].reshape(KV_LORA_RANK, -1),
                            w_kv3[:, :, QK_NOPE_DIM:].reshape(KV_LORA_RANK, -1)], axis=1).astype(BF16)
    w_a = w_branch_mla[0].astype(BF16)
    w_b = w_branch_hgrn[0].astype(BF16)
    w_o = w_out[0].astype(BF16)
    w_fi = w_ffn_in[0].astype(BF16)
    w_fo = w_ffn_out[0].astype(BF16)

    cos_t, sin_t = _rope_tables(pos.astype(F32)[:, None], inv)

    u = _norm_cast(h, g_mix_norm)
    qn, kvn, kr = _proj_lat(u, w_lat, g_q_norm, g_kv_norm, cos_t, sin_t)
    hq = _matmul_cols(u, w_rest, 0, d, BF16, 1024, "proj_hgrn_q")
    hf = _matmul_cols(u, w_rest, d, d, F32, 1024, "proj_hgrn_forget")
    hrest = _matmul_cols(u, w_rest, 2 * d, 4 * d, BF16, 1024, "proj_hgrn_rest")
    q_cat = _q_up(qn, w_q, cos_t, sin_t)
    kv = _matmul_cols(kvn, w_kv, 0, w_kv.shape[1], BF16, 1024, "kv_up")
    o_mla = _attention(q_cat, kv, kr, s_real)
    o_hgrn = _hgrn(hq, hf, hrest, lb_raw, g_hgrn_norm)
    merged = _merge(o_mla, o_hgrn, w_a, w_b, hrest, 2 * d)
    h2, u2 = _out_proj(merged, w_o, h, g_ffn_norm)

    act = _ffn_in(u2, w_fi, conv_w[0], conv_b, s_real)
    out = _ffn_out(act, w_fo, h2, g_final_norm[None, :])
    return out[None]
```

```python
import functools

import jax
import jax.numpy as jnp
from jax import lax
from jax.experimental import pallas as pl
from jax.experimental.pallas import tpu as pltpu

F32 = jnp.float32
BF16 = jnp.bfloat16

N_META = 16
BLOCK = 128
PAD_LEN = BLOCK - N_META
MLA_HEADS = 16
Q_LORA_RANK = 1536
KV_LORA_RANK = 512
QK_NOPE_DIM = 128
QK_ROPE_DIM = 64
QK_HEAD_DIM = QK_NOPE_DIM + QK_ROPE_DIM
V_HEAD_DIM = 128
ROPE_THETA = 10000.0
HGRN_HEADS = 16
HGRN_EXPAND = 128
HGRN_V_DIM = 128
D_FF = 5632
NORM_EPS = 1e-6

LANES = 128
Q_CAT = 2 * LANES
LAT_COLS = Q_LORA_RANK + KV_LORA_RANK + LANES
LOG2E = 1.4426950408889634
VMEM_LIMIT = 52 * 1024 * 1024


def _cparams(n_axes):
    return pltpu.CompilerParams(dimension_semantics=("arbitrary",) * n_axes, vmem_limit_bytes=VMEM_LIMIT)


def _row_tile(rows, target):
    for t in range(target, 0, -LANES):
        if rows % t == 0:
            return t
    raise ValueError(f"no 128-multiple row tile for {rows}")


def _dot(a, b):
    return jnp.dot(a, b, preferred_element_type=F32)


def _dot_nt(a, b):
    return lax.dot_general(a, b, (((1,), (1,)), ((), ())), preferred_element_type=F32)


def _dot_tn(a, b):
    return lax.dot_general(a, b, (((0,), (0,)), ((), ())), preferred_element_type=F32)


def _rms(x, g):
    return x * lax.rsqrt(jnp.mean(x * x, axis=-1, keepdims=True) + NORM_EPS) * g


def _sigmoid(x):
    return 1.0 / (1.0 + jnp.exp(-x))


def _silu(x):
    return x * _sigmoid(x)


def _rope(x, cos_t, sin_t):
    rot = pltpu.roll(x, 32, 1) - pltpu.roll(x, 96, 1)
    return x * cos_t + rot * sin_t


def _rope_table_kernel(pos_ref, inv_ref, cos_ref, sin_ref):
    ang = pos_ref[...] * inv_ref[...]
    keep = lax.broadcasted_iota(jnp.int32, ang.shape, 1) < QK_ROPE_DIM
    cos_ref[...] = jnp.where(keep, jnp.cos(ang), 0.0)
    sin_ref[...] = jnp.where(keep, jnp.sin(ang), 0.0)


def _rope_tables(pos_f, inv):
    rows = pos_f.shape[0]
    tm = _row_tile(rows, 640)
    spec = pl.BlockSpec((tm, LANES), lambda i: (i, 0))
    return pl.pallas_call(
        _rope_table_kernel, name="rope_tables",
        grid=(rows // tm,),
        in_specs=[pl.BlockSpec((tm, 1), lambda i: (i, 0)), pl.BlockSpec((1, LANES), lambda i: (0, 0))],
        out_specs=[spec, spec],
        out_shape=[jax.ShapeDtypeStruct((rows, LANES), F32)] * 2,
        compiler_params=_cparams(1),
    )(pos_f, inv)


def _norm_cast_kernel(x_ref, g_ref, o_ref):
    o_ref[...] = _rms(x_ref[...], g_ref[...]).astype(o_ref.dtype)


def _norm_cast(x, g):
    rows, d = x.shape
    tm = _row_tile(rows, 640)
    return pl.pallas_call(
        _norm_cast_kernel, name="mix_norm",
        grid=(rows // tm,),
        in_specs=[pl.BlockSpec((tm, d), lambda i: (i, 0)), pl.BlockSpec((1, d), lambda i: (0, 0))],
        out_specs=pl.BlockSpec((tm, d), lambda i: (i, 0)),
        out_shape=jax.ShapeDtypeStruct((rows, d), BF16),
        compiler_params=_cparams(1),
    )(x, g)


def _proj_lat_kernel(u_ref, w_ref, gq_ref, gkv_ref, cos_ref, sin_ref, qn_ref, kvn_ref, kr_ref):
    y = _dot(u_ref[...], w_ref[...])
    kv_end = Q_LORA_RANK + KV_LORA_RANK
    qn_ref[...] = _rms(y[:, :Q_LORA_RANK], gq_ref[...]).astype(qn_ref.dtype)
    kvn_ref[...] = _rms(y[:, Q_LORA_RANK:kv_end], gkv_ref[...]).astype(kvn_ref.dtype)
    kr_ref[...] = _rope(y[:, kv_end:], cos_ref[...], sin_ref[...]).astype(kr_ref.dtype)


def _proj_lat(u, w_lat, g_q, g_kv, cos_t, sin_t):
    rows, d = u.shape
    tm = _row_tile(rows, 640)
    row_spec = lambda n: pl.BlockSpec((tm, n), lambda i: (i, 0))
    full_spec = lambda r, n: pl.BlockSpec((r, n), lambda i: (0, 0))
    return pl.pallas_call(
        _proj_lat_kernel, name="proj_latents",
        grid=(rows // tm,),
        in_specs=[row_spec(d), full_spec(d, LAT_COLS), full_spec(1, Q_LORA_RANK), full_spec(1, KV_LORA_RANK),
                  row_spec(LANES), row_spec(LANES)],
        out_specs=[row_spec(Q_LORA_RANK), row_spec(KV_LORA_RANK), row_spec(LANES)],
        out_shape=[jax.ShapeDtypeStruct((rows, Q_LORA_RANK), BF16),
                   jax.ShapeDtypeStruct((rows, KV_LORA_RANK), BF16),
                   jax.ShapeDtypeStruct((rows, LANES), BF16)],
        compiler_params=_cparams(1),
    )(u, w_lat, g_q, g_kv, cos_t, sin_t)


def _matmul_kernel(x_ref, w_ref, o_ref):
    o_ref[...] = _dot(x_ref[...], w_ref[...]).astype(o_ref.dtype)


def _matmul_cols(x, w, col_start, n_cols, out_dtype, tn, name):
    rows, k = x.shape
    tm = _row_tile(rows, 640)
    off = col_start // tn
    return pl.pallas_call(
        _matmul_kernel, name=name,
        grid=(n_cols // tn, rows // tm),
        in_specs=[pl.BlockSpec((tm, k), lambda n, m: (m, 0)), pl.BlockSpec((k, tn), lambda n, m: (0, n + off))],
        out_specs=pl.BlockSpec((tm, tn), lambda n, m: (m, n)),
        out_shape=jax.ShapeDtypeStruct((rows, n_cols), out_dtype),
        compiler_params=_cparams(2),
    )(x, w)


def _q_up_kernel(x_ref, w_ref, cos_ref, sin_ref, o_ref, *, heads_per_tile):
    y = _dot(x_ref[...], w_ref[...]) * (QK_HEAD_DIM ** -0.5 * LOG2E)
    cos_t = cos_ref[...]
    sin_t = sin_ref[...]
    for hh in range(heads_per_tile):
        lo = hh * Q_CAT
        o_ref[:, lo:lo + LANES] = y[:, lo:lo + LANES].astype(o_ref.dtype)
        o_ref[:, lo + LANES:lo + Q_CAT] = _rope(y[:, lo + LANES:lo + Q_CAT], cos_t, sin_t).astype(o_ref.dtype)


def _q_up(qn, w_q, cos_t, sin_t):
    rows, k = qn.shape
    n_cols = w_q.shape[1]
    tm = _row_tile(rows, 640)
    tn = 4 * Q_CAT
    return pl.pallas_call(
        functools.partial(_q_up_kernel, heads_per_tile=tn // Q_CAT), name="q_up_rope",
        grid=(n_cols // tn, rows // tm),
        in_specs=[pl.BlockSpec((tm, k), lambda n, m: (m, 0)), pl.BlockSpec((k, tn), lambda n, m: (0, n)),
                  pl.BlockSpec((tm, LANES), lambda n, m: (m, 0)), pl.BlockSpec((tm, LANES), lambda n, m: (m, 0))],
        out_specs=pl.BlockSpec((tm, tn), lambda n, m: (m, n)),
        out_shape=jax.ShapeDtypeStruct((rows, n_cols), BF16),
        compiler_params=_cparams(2),
    )(qn, w_q, cos_t, sin_t)


def _attn_kernel(q_ref, kn_ref, v_ref, kr_ref, o_ref, kcat_ref, m_ref, l_ref, acc_ref, sa_ref, sb_ref, *, tq,
                 n_real_tiles):
    qi = pl.program_id(1)
    s_real = n_real_tiles * tq

    @pl.when(qi == 0)
    def _():
        kcat_ref[:, :LANES] = kn_ref[...]
        kcat_ref[:, LANES:] = kr_ref[...]

    th = tq // 2
    halves = (0, th)

    def init():
        m_ref[...] = jnp.full(m_ref.shape, -jnp.inf, F32)
        l_ref[...] = jnp.zeros(l_ref.shape, F32)
        acc_ref[...] = jnp.zeros(acc_ref.shape, F32)

    def update(r0, s, v_blk):
        rows = slice(r0, r0 + th)
        chunks = [s[:, c * LANES:(c + 1) * LANES] for c in range(s.shape[1] // LANES)]
        m_prev = m_ref[rows, :]
        m_cur = functools.reduce(jnp.maximum, chunks)
        m_new = jnp.maximum(m_prev, jnp.max(m_cur, axis=1, keepdims=True))
        alpha = jnp.exp2(m_prev - m_new)
        p_chunks = [jnp.exp2(c - m_new) for c in chunks]
        l_cur = functools.reduce(jnp.add, p_chunks)
        l_ref[rows, :] = alpha * l_ref[rows, :] + jnp.sum(l_cur, axis=1, keepdims=True)
        p = jnp.concatenate(p_chunks, axis=1) if len(p_chunks) > 1 else p_chunks[0]
        acc_ref[rows, :] = alpha * acc_ref[rows, :] + _dot(p.astype(v_blk.dtype), v_blk)
        m_ref[rows, :] = m_new

    def finalize():
        o_ref[...] = (acc_ref[...] / l_ref[...]).astype(o_ref.dtype)

    def prefix_scores(r0):
        return _dot_nt(q_ref[r0:r0 + th, :], kcat_ref[s_real:s_real + BLOCK, :])

    @pl.when(qi < n_real_tiles)
    def _():
        init()
        col = lax.broadcasted_iota(jnp.int32, (th, BLOCK), 1)
        for r0 in halves:
            update(r0, jnp.where(col >= PAD_LEN, prefix_scores(r0), -jnp.inf), v_ref[s_real:s_real + BLOCK, :])

        def raw_scores(dst_ref, j):
            k_blk = kcat_ref[pl.ds(pl.multiple_of(j * tq, tq), tq), :]
            for r0 in halves:
                dst_ref[r0:r0 + th, :] = _dot_nt(q_ref[r0:r0 + th, :], k_blk)

        def step(j, src_ref, dst_ref):
            raw_scores(dst_ref, j + 1)
            v_blk = v_ref[pl.ds(pl.multiple_of(j * tq, tq), tq), :]
            for r0 in halves:
                update(r0, src_ref[r0:r0 + th, :], v_blk)

        def diagonal(src_ref):
            v_blk = v_ref[pl.ds(pl.multiple_of(qi * tq, tq), tq), :]
            col = lax.broadcasted_iota(jnp.int32, (th, tq), 1)
            for r0 in halves:
                row = lax.broadcasted_iota(jnp.int32, (th, tq), 0) + r0
                update(r0, jnp.where(col <= row, src_ref[r0:r0 + th, :], -jnp.inf), v_blk)
            finalize()

        raw_scores(sa_ref, 0)

        def body(i, carry):
            step(2 * i, sa_ref, sb_ref)
            step(2 * i + 1, sb_ref, sa_ref)
            return carry

        lax.fori_loop(0, qi // 2, body, 0)

        @pl.when(qi % 2 == 1)
        def _():
            step(qi - 1, sa_ref, sb_ref)
            diagonal(sb_ref)

        @pl.when(qi % 2 == 0)
        def _():
            diagonal(sa_ref)

    @pl.when(qi == n_real_tiles)
    def _():
        init()
        for r0 in halves:
            row = lax.broadcasted_iota(jnp.int32, (th, BLOCK), 0) + r0
            col = lax.broadcasted_iota(jnp.int32, (th, BLOCK), 1)
            visible = (col <= row) & ((col >= PAD_LEN) | (col == row))
            update(r0, jnp.where(visible, prefix_scores(r0), -jnp.inf), v_ref[s_real:s_real + BLOCK, :])
        finalize()


def _attention(q_cat, kv, kr, s_real):
    rows = q_cat.shape[0]
    tq = 512
    n_real_tiles = s_real // tq
    return pl.pallas_call(
        functools.partial(_attn_kernel, tq=tq, n_real_tiles=n_real_tiles), name="mla_attention",
        grid=(MLA_HEADS, n_real_tiles + 1),
        in_specs=[pl.BlockSpec((tq, Q_CAT), lambda h, i: (i, h)),
                  pl.BlockSpec((rows, LANES), lambda h, i: (0, h)),
                  pl.BlockSpec((rows, LANES), lambda h, i: (0, MLA_HEADS + h)),
                  pl.BlockSpec((rows, LANES), lambda h, i: (0, 0))],
        out_specs=pl.BlockSpec((tq, V_HEAD_DIM), lambda h, i: (i, h)),
        out_shape=jax.ShapeDtypeStruct((rows, MLA_HEADS * V_HEAD_DIM), BF16),
        scratch_shapes=[pltpu.VMEM((rows, Q_CAT), BF16), pltpu.VMEM((tq, LANES), F32),
                        pltpu.VMEM((tq, LANES), F32), pltpu.VMEM((tq, V_HEAD_DIM), F32),
                        pltpu.VMEM((tq, tq), F32), pltpu.VMEM((tq, tq), F32)],
        compiler_params=_cparams(2),
    )(q_cat, kv, kv, kr)


def _split3(x):
    x1 = x.astype(BF16)
    r1 = x - x1.astype(F32)
    x2 = r1.astype(BF16)
    x3 = (r1 - x2.astype(F32)).astype(BF16)
    return x1, x2, x3


def _block_row(x, size, j):
    c, n = x.shape
    g = x.reshape(c // size, size, n)[:, j:j + 1, :]
    return jnp.broadcast_to(g, (c // size, size, n)).reshape(c, n)


HGRN_LEVELS = 7


def _hgrn_head(hq, hf, hi, hg, lb_raw, gain, st, valid, row, lev):
    n = BLOCK
    top = jnp.max(lb_raw, axis=0, keepdims=True)
    e = jnp.exp(lb_raw - top)
    lb = e[0:1, :] / jnp.sum(e, axis=0, keepdims=True)

    f = lb + (1.0 - lb) * _sigmoid(hf)
    f_eff = jnp.where(valid, f, 1.0)
    g = jnp.where(valid, jnp.log(f), 0.0)
    k = jnp.where(valid, 1.0 - f, 0.0)
    q = _silu(hq.astype(F32))

    col = lax.broadcasted_iota(jnp.int32, (n, n), 1)
    tri = (col <= row).astype(BF16)
    g1, g2, g3 = _split3(g)
    b = _dot(tri, g1) + _dot(tri, g2) + _dot(tri, g3)

    a = jnp.where(lev == 0, jnp.sum(q * k, axis=1, keepdims=True), 0.0)
    for level in range(1, HGRN_LEVELS + 1):
        size = 1 << level
        if level == 1:
            w = jnp.where((row & 1) == 1, f_eff, 1.0)
        elif level == 2:
            r4 = row & 3
            up1 = pltpu.roll(g, n - 1, 0)
            dn1 = pltpu.roll(g, 1, 0)
            w = jnp.exp(jnp.where(r4 == 0, up1, jnp.where(r4 == 1, 0.0, jnp.where(r4 == 2, g, g + dn1))))
        else:
            w = jnp.exp(-jnp.abs(b - _block_row(b, size, size // 2 - 1)))
        a_l = _dot_nt((q * w).astype(BF16), (k * w).astype(BF16))
        a = jnp.where(lev == level, a_l, a)

    o = _dot(a.astype(BF16), hi) + _dot_nt((q * jnp.exp(b)).astype(BF16), st.astype(BF16))
    b_last = b[n - 1:n, :]
    st_new = st * jnp.exp(b_last) + _dot_tn(hi, (k * jnp.exp(b_last - b)).astype(BF16))
    out = _rms(o, gain) * _silu(hg.astype(F32))
    return out, st_new


def _hgrn_kernel(hq_ref, hf_ref, hi_ref, hg_ref, lb_ref, g_ref, o_ref, st_ref):
    c = pl.program_id(0)
    n = BLOCK

    @pl.when(c == 0)
    def _():
        st_ref[...] = jnp.zeros(st_ref.shape, F32)

    row = lax.broadcasted_iota(jnp.int32, (n, n), 0)
    col = lax.broadcasted_iota(jnp.int32, (n, n), 1)
    valid = (c > 0) | (row >= PAD_LEN)
    x = row ^ col
    lev = functools.reduce(jnp.add, [(x >= (1 << i)).astype(jnp.int32) for i in range(HGRN_LEVELS)])
    lev = jnp.where(col > row, -1, lev)
    gain = g_ref[...]

    def head_pair(i, carry):
        for hh in range(2):
            head = 2 * i + hh
            cols = pl.ds(pl.multiple_of(head * BLOCK, BLOCK), BLOCK)
            out, st_new = _hgrn_head(hq_ref[:, cols], hf_ref[:, cols], hi_ref[:, cols], hg_ref[:, cols],
                                     lb_ref[:, cols], gain, st_ref[head], valid, row, lev)
            st_ref[head] = st_new
            o_ref[:, cols] = out.astype(o_ref.dtype)
        return carry

    lax.fori_loop(0, HGRN_HEADS // 2, head_pair, 0)


def _hgrn(hq, hf, hrest, lb_raw, g_hgrn):
    rows, width = hq.shape
    n_chunks = rows // BLOCK
    blk = lambda off: pl.BlockSpec((BLOCK, width), lambda c: ((c + n_chunks - 1) % n_chunks, off))
    return pl.pallas_call(
        _hgrn_kernel, name="hgrn2",
        grid=(n_chunks,),
        in_specs=[blk(0), blk(0), blk(0), blk(1),
                  pl.BlockSpec((lb_raw.shape[0], width), lambda c: (0, 0)),
                  pl.BlockSpec((1, HGRN_V_DIM), lambda c: (0, 0))],
        out_specs=blk(0),
        out_shape=jax.ShapeDtypeStruct((rows, width), BF16),
        scratch_shapes=[pltpu.VMEM((HGRN_HEADS, HGRN_V_DIM, HGRN_EXPAND), F32)],
        compiler_params=_cparams(1),
    )(hq, hf, hrest, hrest, lb_raw, g_hgrn)


def _merge_kernel(om_ref, oh_ref, wa_ref, wb_ref, ga_ref, gb_ref, o_ref):
    a = _dot(om_ref[...], wa_ref[...])
    b = _dot(oh_ref[...], wb_ref[...])
    o = _sigmoid(ga_ref[...].astype(F32)) * a + _sigmoid(gb_ref[...].astype(F32)) * b
    o_ref[...] = o.astype(o_ref.dtype)


def _merge(o_mla, o_hgrn, w_a, w_b, hrest, gate_off):
    rows, k = o_mla.shape
    d = w_a.shape[1]
    tm = _row_tile(rows, 640)
    tn = 512
    ga_off = gate_off // tn
    gb_off = (gate_off + d) // tn
    return pl.pallas_call(
        _merge_kernel, name="branch_merge",
        grid=(d // tn, rows // tm),
        in_specs=[pl.BlockSpec((tm, k), lambda n, m: (m, 0)), pl.BlockSpec((tm, k), lambda n, m: (m, 0)),
                  pl.BlockSpec((k, tn), lambda n, m: (0, n)), pl.BlockSpec((k, tn), lambda n, m: (0, n)),
                  pl.BlockSpec((tm, tn), lambda n, m: (m, n + ga_off)),
                  pl.BlockSpec((tm, tn), lambda n, m: (m, n + gb_off))],
        out_specs=pl.BlockSpec((tm, tn), lambda n, m: (m, n)),
        out_shape=jax.ShapeDtypeStruct((rows, d), BF16),
        compiler_params=_cparams(2),
    )(o_mla, o_hgrn, w_a, w_b, hrest, hrest)


def _out_proj_kernel(mg_ref, w_ref, h_ref, g_ref, h2_ref, u2_ref):
    h2 = h_ref[...] + _dot(mg_ref[...], w_ref[...])
    h2_ref[...] = h2
    u2_ref[...] = _rms(h2, g_ref[...]).astype(u2_ref.dtype)


def _out_proj(merged, w_out, h, g_ffn):
    rows, d = h.shape
    tm = _row_tile(rows, 320)
    row_spec = pl.BlockSpec((tm, d), lambda i: (i, 0))
    return pl.pallas_call(
        _out_proj_kernel, name="mix_out_proj",
        grid=(rows // tm,),
        in_specs=[row_spec, pl.BlockSpec((d, d), lambda i: (0, 0)), row_spec, pl.BlockSpec((1, d), lambda i: (0, 0))],
        out_specs=[row_spec, row_spec],
        out_shape=[jax.ShapeDtypeStruct((rows, d), F32), jax.ShapeDtypeStruct((rows, d), BF16)],
        compiler_params=_cparams(1),
    )(merged, w_out, h, g_ffn)


HALO = 16


def _ffn_in_kernel(u_ref, halo_ref, wg_ref, wu_ref, cw_ref, cb_ref, o_ref):
    u = u_ref[...]
    wg = wg_ref[...]
    gate = _dot(u, wg)
    up = _dot(u, wu_ref[...])
    prev = _dot(halo_ref[...], wg)
    row = lax.broadcasted_iota(jnp.int32, gate.shape, 0)
    back1 = jnp.where(row == 0, prev[HALO - 1:HALO, :], pltpu.roll(gate, 1, 0))
    back2 = jnp.where(row == 0, prev[HALO - 2:HALO - 1, :],
                      jnp.where(row == 1, prev[HALO - 1:HALO, :], pltpu.roll(gate, 2, 0)))
    cw = cw_ref[...]
    conv = cw[0:1, :] * back2 + cw[1:2, :] * back1 + cw[2:3, :] * gate + cb_ref[...]
    o_ref[...] = (_silu(conv) * up).astype(o_ref.dtype)


def _ffn_in(u2, w_fi, conv_w, conv_b, s_real):
    rows, d = u2.shape
    tm = 512
    tn = 512
    up_off = D_FF // tn
    n_halo = rows // HALO
    halo_map = lambda n, m: ((m * (tm // HALO) + n_halo - 1) % n_halo, 0)
    return pl.pallas_call(
        _ffn_in_kernel, name="ffn_in_conv_gate",
        grid=(D_FF // tn, s_real // tm),
        in_specs=[pl.BlockSpec((tm, d), lambda n, m: (m, 0)), pl.BlockSpec((HALO, d), halo_map),
                  pl.BlockSpec((d, tn), lambda n, m: (0, n)), pl.BlockSpec((d, tn), lambda n, m: (0, n + up_off)),
                  pl.BlockSpec((conv_w.shape[0], tn), lambda n, m: (0, n)), pl.BlockSpec((1, tn), lambda n, m: (0, n))],
        out_specs=pl.BlockSpec((tm, tn), lambda n, m: (m, n)),
        out_shape=jax.ShapeDtypeStruct((s_real, D_FF), BF16),
        compiler_params=_cparams(2),
    )(u2, u2, w_fi, w_fi, conv_w, conv_b)


def _ffn_out_kernel(a_ref, w_ref, h2_ref, g_ref, o_ref):
    kk = pl.program_id(1)

    @pl.when(kk == 0)
    def _():
        o_ref[...] = h2_ref[...]

    o_ref[...] += _dot(a_ref[...], w_ref[...])

    @pl.when(kk == pl.num_programs(1) - 1)
    def _():
        o_ref[...] = _rms(o_ref[...], g_ref[...])


def _ffn_out(act, w_fo, h2, g_final):
    s_real, k = act.shape
    d = w_fo.shape[1]
    tm = 512
    tk = 512
    return pl.pallas_call(
        _ffn_out_kernel, name="ffn_out_final_norm",
        grid=(s_real // tm, k // tk),
        in_specs=[pl.BlockSpec((tm, tk), lambda m, kk: (m, kk)), pl.BlockSpec((tk, d), lambda m, kk: (kk, 0)),
                  pl.BlockSpec((tm, d), lambda m, kk: (m, 0)), pl.BlockSpec((1, d), lambda m, kk: (0, 0))],
        out_specs=pl.BlockSpec((tm, d), lambda m, kk: (m, 0)),
        out_shape=jax.ShapeDtypeStruct((s_real, d), F32),
        compiler_params=_cparams(2),
    )(act, w_fo, h2, g_final)


def kernel(x, positions, meta_tokens, w_in, w_q_up, w_kv_up, w_branch_mla, w_branch_hgrn, w_out, w_ffn_in,
           w_ffn_out, conv_w, conv_b, g_mix_norm, g_q_norm, g_kv_norm, g_hgrn_norm, g_ffn_norm, g_final_norm,
           lb_raw):
    b, s_real, d = x.shape
    assert b == 1 and w_in.shape[0] == 1 and s_real % 512 == 0
    dt = x.dtype

    h = jnp.concatenate([x[0], jnp.zeros((PAD_LEN, d), dt), meta_tokens.astype(dt)], axis=0)
    pos = jnp.concatenate([positions[0].astype(jnp.int32) + N_META, jnp.zeros((PAD_LEN,), jnp.int32),
                           jnp.arange(N_META, dtype=jnp.int32)])
    inv = 1.0 / (ROPE_THETA ** (jnp.arange(0, QK_ROPE_DIM, 2, dtype=F32) / QK_ROPE_DIM))
    inv = jnp.concatenate([inv, inv, jnp.zeros((LANES - QK_ROPE_DIM,), F32)])[None, :]

    lat_end = Q_LORA_RANK + KV_LORA_RANK + QK_ROPE_DIM
    w_lat = jnp.pad(w_in[0, :, :lat_end], ((0, 0), (0, LANES - QK_ROPE_DIM))).astype(BF16)
    w_rest = w_in[0, :, lat_end:].astype(BF16)
    w_q = jnp.pad(w_q_up[0].reshape(Q_LORA_RANK, MLA_HEADS, QK_HEAD_DIM),
                  ((0, 0), (0, 0), (0, Q_CAT - QK_HEAD_DIM))).reshape(Q_LORA_RANK, MLA_HEADS * Q_CAT).astype(BF16)
    w_kv3 = w_kv_up[0].reshape(KV_LORA_RANK, MLA_HEADS, QK_NOPE_DIM + V_HEAD_DIM)
    w_kv = jnp.concatenate([w_kv3[:, :, :QK_NOPE_DIM].reshape(KV_LORA_RANK, -1),
                            w_kv3[:, :, QK_NOPE_DIM:].reshape(KV_LORA_RANK, -1)], axis=1).astype(BF16)
    w_a = w_branch_mla[0].astype(BF16)
    w_b = w_branch_hgrn[0].astype(BF16)
    w_o = w_out[0].astype(BF16)
    w_fi = w_ffn_in[0].astype(BF16)
    w_fo = w_ffn_out[0].astype(BF16)

    cos_t, sin_t = _rope_tables(pos.astype(F32)[:, None], inv)

    u = _norm_cast(h, g_mix_norm)
    qn, kvn, kr = _proj_lat(u, w_lat, g_q_norm, g_kv_norm, cos_t, sin_t)
    hq = _matmul_cols(u, w_rest, 0, d, BF16, 1024, "proj_hgrn_q")
    hf = _matmul_cols(u, w_rest, d, d, F32, 1024, "proj_hgrn_forget")
    hrest = _matmul_cols(u, w_rest, 2 * d, 4 * d, BF16, 1024, "proj_hgrn_rest")
    q_cat = _q_up(qn, w_q, cos_t, sin_t)
    kv = _matmul_cols(kvn, w_kv, 0, w_kv.shape[1], BF16, 1024, "kv_up")
    o_mla = _attention(q_cat, kv, kr, s_real)
    o_hgrn = _hgrn(hq, hf, hrest, lb_raw, g_hgrn_norm)
    merged = _merge(o_mla, o_hgrn, w_a, w_b, hrest, 2 * d)
    h2, u2 = _out_proj(merged, w_o, h, g_ffn_norm)

    act = _ffn_in(u2, w_fi, conv_w[0], conv_b, s_real)
    out = _ffn_out(act, w_fo, h2, g_final_norm[None, :])
    return out[None]
```

```python
import functools

import jax
import jax.numpy as jnp
from jax import lax
from jax.experimental import pallas as pl
from jax.experimental.pallas import tpu as pltpu

F32 = jnp.float32
BF16 = jnp.bfloat16

N_META = 16
BLOCK = 128
PAD_LEN = BLOCK - N_META
MLA_HEADS = 16
Q_LORA_RANK = 1536
KV_LORA_RANK = 512
QK_NOPE_DIM = 128
QK_ROPE_DIM = 64
QK_HEAD_DIM = QK_NOPE_DIM + QK_ROPE_DIM
V_HEAD_DIM = 128
ROPE_THETA = 10000.0
HGRN_HEADS = 16
HGRN_EXPAND = 128
HGRN_V_DIM = 128
D_FF = 5632
NORM_EPS = 1e-6

LANES = 128
Q_CAT = 2 * LANES
LAT_COLS = Q_LORA_RANK + KV_LORA_RANK + LANES
LOG2E = 1.4426950408889634
VMEM_LIMIT = 52 * 1024 * 1024


def _cparams(n_axes):
    return pltpu.CompilerParams(dimension_semantics=("arbitrary",) * n_axes, vmem_limit_bytes=VMEM_LIMIT)


def _row_tile(rows, target):
    for t in range(target, 0, -LANES):
        if rows % t == 0:
            return t
    raise ValueError(f"no 128-multiple row tile for {rows}")


def _dot(a, b):
    return jnp.dot(a, b, preferred_element_type=F32)


def _dot_nt(a, b):
    return lax.dot_general(a, b, (((1,), (1,)), ((), ())), preferred_element_type=F32)


def _dot_tn(a, b):
    return lax.dot_general(a, b, (((0,), (0,)), ((), ())), preferred_element_type=F32)


def _rms(x, g):
    return x * lax.rsqrt(jnp.mean(x * x, axis=-1, keepdims=True) + NORM_EPS) * g


def _sigmoid(x):
    return 1.0 / (1.0 + jnp.exp(-x))


def _silu(x):
    return x * _sigmoid(x)


def _rope(x, cos_t, sin_t):
    rot = pltpu.roll(x, 32, 1) - pltpu.roll(x, 96, 1)
    return x * cos_t + rot * sin_t


def _rope_table_kernel(pos_ref, inv_ref, cos_ref, sin_ref):
    ang = pos_ref[...] * inv_ref[...]
    keep = lax.broadcasted_iota(jnp.int32, ang.shape, 1) < QK_ROPE_DIM
    cos_ref[...] = jnp.where(keep, jnp.cos(ang), 0.0)
    sin_ref[...] = jnp.where(keep, jnp.sin(ang), 0.0)


def _rope_tables(pos_f, inv):
    rows = pos_f.shape[0]
    tm = _row_tile(rows, 640)
    spec = pl.BlockSpec((tm, LANES), lambda i: (i, 0))
    return pl.pallas_call(
        _rope_table_kernel, name="rope_tables",
        grid=(rows // tm,),
        in_specs=[pl.BlockSpec((tm, 1), lambda i: (i, 0)), pl.BlockSpec((1, LANES), lambda i: (0, 0))],
        out_specs=[spec, spec],
        out_shape=[jax.ShapeDtypeStruct((rows, LANES), F32)] * 2,
        compiler_params=_cparams(1),
    )(pos_f, inv)


def _norm_cast_kernel(x_ref, g_ref, o_ref):
    o_ref[...] = _rms(x_ref[...], g_ref[...]).astype(o_ref.dtype)


def _norm_cast(x, g):
    rows, d = x.shape
    tm = _row_tile(rows, 640)
    return pl.pallas_call(
        _norm_cast_kernel, name="mix_norm",
        grid=(rows // tm,),
        in_specs=[pl.BlockSpec((tm, d), lambda i: (i, 0)), pl.BlockSpec((1, d), lambda i: (0, 0))],
        out_specs=pl.BlockSpec((tm, d), lambda i: (i, 0)),
        out_shape=jax.ShapeDtypeStruct((rows, d), BF16),
        compiler_params=_cparams(1),
    )(x, g)


def _proj_lat_kernel(u_ref, w_ref, gq_ref, gkv_ref, cos_ref, sin_ref, qn_ref, kvn_ref, kr_ref):
    y = _dot(u_ref[...], w_ref[...])
    kv_end = Q_LORA_RANK + KV_LORA_RANK
    qn_ref[...] = _rms(y[:, :Q_LORA_RANK], gq_ref[...]).astype(qn_ref.dtype)
    kvn_ref[...] = _rms(y[:, Q_LORA_RANK:kv_end], gkv_ref[...]).astype(kvn_ref.dtype)
    kr_ref[...] = _rope(y[:, kv_end:], cos_ref[...], sin_ref[...]).astype(kr_ref.dtype)


def _proj_lat(u, w_lat, g_q, g_kv, cos_t, sin_t):
    rows, d = u.shape
    tm = _row_tile(rows, 640)
    row_spec = lambda n: pl.BlockSpec((tm, n), lambda i: (i, 0))
    full_spec = lambda r, n: pl.BlockSpec((r, n), lambda i: (0, 0))
    return pl.pallas_call(
        _proj_lat_kernel, name="proj_latents",
        grid=(rows // tm,),
        in_specs=[row_spec(d), full_spec(d, LAT_COLS), full_spec(1, Q_LORA_RANK), full_spec(1, KV_LORA_RANK),
                  row_spec(LANES), row_spec(LANES)],
        out_specs=[row_spec(Q_LORA_RANK), row_spec(KV_LORA_RANK), row_spec(LANES)],
        out_shape=[jax.ShapeDtypeStruct((rows, Q_LORA_RANK), BF16),
                   jax.ShapeDtypeStruct((rows, KV_LORA_RANK), BF16),
                   jax.ShapeDtypeStruct((rows, LANES), BF16)],
        compiler_params=_cparams(1),
    )(u, w_lat, g_q, g_kv, cos_t, sin_t)


def _matmul_kernel(x_ref, w_ref, o_ref):
    o_ref[...] = _dot(x_ref[...], w_ref[...]).astype(o_ref.dtype)


def _matmul_cols(x, w, col_start, n_cols, out_dtype, tn, name):
    rows, k = x.shape
    tm = _row_tile(rows, 640)
    off = col_start // tn
    return pl.pallas_call(
        _matmul_kernel, name=name,
        grid=(n_cols // tn, rows // tm),
        in_specs=[pl.BlockSpec((tm, k), lambda n, m: (m, 0)), pl.BlockSpec((k, tn), lambda n, m: (0, n + off))],
        out_specs=pl.BlockSpec((tm, tn), lambda n, m: (m, n)),
        out_shape=jax.ShapeDtypeStruct((rows, n_cols), out_dtype),
        compiler_params=_cparams(2),
    )(x, w)


def _kv_up_kernel(x_ref, w_ref, o_ref):
    y = _dot(x_ref[...], w_ref[...])
    for hh in range(o_ref.shape[0]):
        o_ref[hh] = y[:, hh * LANES:(hh + 1) * LANES].astype(o_ref.dtype)


def _kv_up(kvn, w_kv):
    rows, k = kvn.shape
    n_slabs = w_kv.shape[1] // LANES
    tm = _row_tile(rows, 640)
    per_tile = 8
    return pl.pallas_call(
        _kv_up_kernel, name="kv_up",
        grid=(n_slabs // per_tile, rows // tm),
        in_specs=[pl.BlockSpec((tm, k), lambda n, m: (m, 0)),
                  pl.BlockSpec((k, per_tile * LANES), lambda n, m: (0, n))],
        out_specs=pl.BlockSpec((per_tile, tm, LANES), lambda n, m: (n, m, 0)),
        out_shape=jax.ShapeDtypeStruct((n_slabs, rows, LANES), BF16),
        compiler_params=_cparams(2),
    )(kvn, w_kv)


def _q_up_kernel(x_ref, w_ref, cos_ref, sin_ref, o_ref, *, heads_per_tile):
    y = _dot(x_ref[...], w_ref[...]) * (QK_HEAD_DIM ** -0.5 * LOG2E)
    cos_t = cos_ref[...]
    sin_t = sin_ref[...]
    for hh in range(heads_per_tile):
        lo = hh * Q_CAT
        o_ref[:, lo:lo + LANES] = y[:, lo:lo + LANES].astype(o_ref.dtype)
        o_ref[:, lo + LANES:lo + Q_CAT] = _rope(y[:, lo + LANES:lo + Q_CAT], cos_t, sin_t).astype(o_ref.dtype)


def _q_up(qn, w_q, cos_t, sin_t):
    rows, k = qn.shape
    n_cols = w_q.shape[1]
    tm = _row_tile(rows, 640)
    tn = 4 * Q_CAT
    return pl.pallas_call(
        functools.partial(_q_up_kernel, heads_per_tile=tn // Q_CAT), name="q_up_rope",
        grid=(n_cols // tn, rows // tm),
        in_specs=[pl.BlockSpec((tm, k), lambda n, m: (m, 0)), pl.BlockSpec((k, tn), lambda n, m: (0, n)),
                  pl.BlockSpec((tm, LANES), lambda n, m: (m, 0)), pl.BlockSpec((tm, LANES), lambda n, m: (m, 0))],
        out_specs=pl.BlockSpec((tm, tn), lambda n, m: (m, n)),
        out_shape=jax.ShapeDtypeStruct((rows, n_cols), BF16),
        compiler_params=_cparams(2),
    )(qn, w_q, cos_t, sin_t)


def _attn_kernel(q_ref, kn_ref, v_ref, kr_ref, o_ref, kcat_ref, m_ref, l_ref, acc_ref, sa_ref, sb_ref, *, tq,
                 n_real_tiles):
    step_id = pl.program_id(1)
    qi = step_id - 1
    s_real = n_real_tiles * tq

    @pl.when(step_id == 0)
    def _():
        kcat_ref[:, :LANES] = kn_ref[...]
        kcat_ref[:, LANES:] = kr_ref[...]

    th = tq // 2
    halves = (0, th)

    def init():
        m_ref[...] = jnp.full(m_ref.shape, -jnp.inf, F32)
        l_ref[...] = jnp.zeros(l_ref.shape, F32)
        acc_ref[...] = jnp.zeros(acc_ref.shape, F32)

    def update(r0, src_ref, n_keys, v_blk, visible=None):
        rows = slice(r0, r0 + th)

        def chunk(c):
            s = src_ref[rows, c * LANES:(c + 1) * LANES]
            return s if visible is None else jnp.where(visible(c), s, -jnp.inf)

        n_chunks = n_keys // LANES
        m_prev = m_ref[rows, :]
        m_cur = functools.reduce(jnp.maximum, [chunk(c) for c in range(n_chunks)])
        m_new = jnp.maximum(m_prev, jnp.max(m_cur, axis=1, keepdims=True))
        alpha = jnp.exp2(m_prev - m_new)
        p_chunks = [jnp.exp2(chunk(c) - m_new) for c in range(n_chunks)]
        l_cur = functools.reduce(jnp.add, p_chunks)
        l_ref[rows, :] = alpha * l_ref[rows, :] + jnp.sum(l_cur, axis=1, keepdims=True)
        p = jnp.concatenate(p_chunks, axis=1) if n_chunks > 1 else p_chunks[0]
        acc_ref[rows, :] = alpha * acc_ref[rows, :] + _dot(p.astype(v_blk.dtype), v_blk)
        m_ref[rows, :] = m_new

    def finalize():
        o_ref[...] = (acc_ref[...] / l_ref[...]).astype(o_ref.dtype)

    def prefix_block(visible_fn):
        for r0 in halves:
            sa_ref[r0:r0 + th, :BLOCK] = _dot_nt(q_ref[r0:r0 + th, :], kcat_ref[s_real:s_real + BLOCK, :])
        for r0 in halves:
            update(r0, sa_ref, BLOCK, v_ref[s_real:s_real + BLOCK, :], visible_fn(r0))

    @pl.when(step_id > 0)
    def _():
        init()
        col = lax.broadcasted_iota(jnp.int32, (th, BLOCK), 1)
        prefix_block(lambda r0: lambda c: col >= PAD_LEN)

        def raw_scores(dst_ref, j):
            k_blk = kcat_ref[pl.ds(pl.multiple_of(j * tq, tq), tq), :]
            for r0 in halves:
                dst_ref[r0:r0 + th, :] = _dot_nt(q_ref[r0:r0 + th, :], k_blk)

        def step(j, src_ref, dst_ref):
            raw_scores(dst_ref, j + 1)
            v_blk = v_ref[pl.ds(pl.multiple_of(j * tq, tq), tq), :]
            for r0 in halves:
                update(r0, src_ref, tq, v_blk)

        def diagonal(src_ref):
            v_blk = v_ref[pl.ds(pl.multiple_of(qi * tq, tq), tq), :]
            row = lax.broadcasted_iota(jnp.int32, (th, LANES), 0)
            lane = lax.broadcasted_iota(jnp.int32, (th, LANES), 1)
            for r0 in halves:
                update(r0, src_ref, tq, v_blk, lambda c, r0=r0: lane + c * LANES <= row + r0)
            finalize()

        raw_scores(sb_ref, 0)

        def body(i, carry):
            step(2 * i, sb_ref, sa_ref)
            step(2 * i + 1, sa_ref, sb_ref)
            return carry

        lax.fori_loop(0, qi // 2, body, 0)

        @pl.when(qi % 2 == 1)
        def _():
            step(qi - 1, sb_ref, sa_ref)
            diagonal(sa_ref)

        @pl.when(qi % 2 == 0)
        def _():
            diagonal(sb_ref)

    @pl.when(step_id == 0)
    def _():
        init()
        row = lax.broadcasted_iota(jnp.int32, (th, BLOCK), 0)
        col = lax.broadcasted_iota(jnp.int32, (th, BLOCK), 1)
        prefix_block(lambda r0: lambda c: (col <= row + r0) & ((col >= PAD_LEN) | (col == row + r0)))
        finalize()


def _attention(q_cat, kv, kr, s_real):
    rows = q_cat.shape[0]
    tq = 512
    n_real_tiles = s_real // tq
    q_tile = lambda h, i: ((i + n_real_tiles) % (n_real_tiles + 1), h)
    return pl.pallas_call(
        functools.partial(_attn_kernel, tq=tq, n_real_tiles=n_real_tiles), name="mla_attention",
        grid=(MLA_HEADS, n_real_tiles + 1),
        in_specs=[pl.BlockSpec((tq, Q_CAT), q_tile),
                  pl.BlockSpec((None, rows, LANES), lambda h, i: (h, 0, 0)),
                  pl.BlockSpec((None, rows, LANES), lambda h, i: (MLA_HEADS + h, 0, 0)),
                  pl.BlockSpec((rows, LANES), lambda h, i: (0, 0))],
        out_specs=pl.BlockSpec((tq, V_HEAD_DIM), q_tile),
        out_shape=jax.ShapeDtypeStruct((rows, MLA_HEADS * V_HEAD_DIM), BF16),
        scratch_shapes=[pltpu.VMEM((rows, Q_CAT), BF16), pltpu.VMEM((tq, LANES), F32),
                        pltpu.VMEM((tq, LANES), F32), pltpu.VMEM((tq, V_HEAD_DIM), F32),
                        pltpu.VMEM((tq, tq), F32), pltpu.VMEM((tq, tq), F32)],
        compiler_params=_cparams(2),
    )(q_cat, kv, kv, kr)


def _split3(x):
    x1 = x.astype(BF16)
    r1 = x - x1.astype(F32)
    x2 = r1.astype(BF16)
    x3 = (r1 - x2.astype(F32)).astype(BF16)
    return x1, x2, x3


def _block_row(x, size, j):
    c, n = x.shape
    g = x.reshape(c // size, size, n)[:, j:j + 1, :]
    return jnp.broadcast_to(g, (c // size, size, n)).reshape(c, n)


HGRN_LEVELS = 7


def _hgrn_head(hq, hf, hi, hg, lb_raw, gain, st, valid, row, lev):
    n = BLOCK
    top = jnp.max(lb_raw, axis=0, keepdims=True)
    e = jnp.exp(lb_raw - top)
    lb = e[0:1, :] / jnp.sum(e, axis=0, keepdims=True)

    f = lb + (1.0 - lb) * _sigmoid(hf)
    f_eff = jnp.where(valid, f, 1.0)
    g = jnp.where(valid, jnp.log(f), 0.0)
    k = jnp.where(valid, 1.0 - f, 0.0)
    q = _silu(hq.astype(F32))

    col = lax.broadcasted_iota(jnp.int32, (n, n), 1)
    tri = (col <= row).astype(BF16)
    g1, g2, g3 = _split3(g)
    b = _dot(tri, g1) + _dot(tri, g2) + _dot(tri, g3)

    a = jnp.where(lev == 0, jnp.sum(q * k, axis=1, keepdims=True), 0.0)
    for level in range(1, HGRN_LEVELS + 1):
        size = 1 << level
        if level == 1:
            w = jnp.where((row & 1) == 1, f_eff, 1.0)
        elif level == 2:
            r4 = row & 3
            up1 = pltpu.roll(g, n - 1, 0)
            dn1 = pltpu.roll(g, 1, 0)
            w = jnp.exp(jnp.where(r4 == 0, up1, jnp.where(r4 == 1, 0.0, jnp.where(r4 == 2, g, g + dn1))))
        else:
            w = jnp.exp(-jnp.abs(b - _block_row(b, size, size // 2 - 1)))
        a_l = _dot_nt((q * w).astype(BF16), (k * w).astype(BF16))
        a = jnp.where(lev == level, a_l, a)

    o = _dot(a.astype(BF16), hi) + _dot_nt((q * jnp.exp(b)).astype(BF16), st.astype(BF16))
    b_last = b[n - 1:n, :]
    st_new = st * jnp.exp(b_last) + _dot_tn(hi, (k * jnp.exp(b_last - b)).astype(BF16))
    out = _rms(o, gain) * _silu(hg.astype(F32))
    return out, st_new


def _hgrn_kernel(hq_ref, hf_ref, hi_ref, hg_ref, lb_ref, g_ref, o_ref, st_ref):
    c = pl.program_id(0)
    n = BLOCK

    @pl.when(c == 0)
    def _():
        st_ref[...] = jnp.zeros(st_ref.shape, F32)

    row = lax.broadcasted_iota(jnp.int32, (n, n), 0)
    col = lax.broadcasted_iota(jnp.int32, (n, n), 1)
    valid = (c > 0) | (row >= PAD_LEN)
    x = row ^ col
    lev = functools.reduce(jnp.add, [(x >= (1 << i)).astype(jnp.int32) for i in range(HGRN_LEVELS)])
    lev = jnp.where(col > row, -1, lev)
    gain = g_ref[...]

    def head_pair(i, carry):
        for hh in range(2):
            head = 2 * i + hh
            cols = pl.ds(pl.multiple_of(head * BLOCK, BLOCK), BLOCK)
            out, st_new = _hgrn_head(hq_ref[:, cols], hf_ref[:, cols], hi_ref[:, cols], hg_ref[:, cols],
                                     lb_ref[:, cols], gain, st_ref[head], valid, row, lev)
            st_ref[head] = st_new
            o_ref[:, cols] = out.astype(o_ref.dtype)
        return carry

    lax.fori_loop(0, HGRN_HEADS // 2, head_pair, 0)


def _hgrn(hq, hf, hrest, lb_raw, g_hgrn):
    rows, width = hq.shape
    n_chunks = rows // BLOCK
    blk = lambda off: pl.BlockSpec((BLOCK, width), lambda c: ((c + n_chunks - 1) % n_chunks, off))
    return pl.pallas_call(
        _hgrn_kernel, name="hgrn2",
        grid=(n_chunks,),
        in_specs=[blk(0), blk(0), blk(0), blk(1),
                  pl.BlockSpec((lb_raw.shape[0], width), lambda c: (0, 0)),
                  pl.BlockSpec((1, HGRN_V_DIM), lambda c: (0, 0))],
        out_specs=blk(0),
        out_shape=jax.ShapeDtypeStruct((rows, width), BF16),
        scratch_shapes=[pltpu.VMEM((HGRN_HEADS, HGRN_V_DIM, HGRN_EXPAND), F32)],
        compiler_params=_cparams(1),
    )(hq, hf, hrest, hrest, lb_raw, g_hgrn)


def _merge_kernel(om_ref, oh_ref, wa_ref, wb_ref, ga_ref, gb_ref, o_ref):
    a = _dot(om_ref[...], wa_ref[...])
    b = _dot(oh_ref[...], wb_ref[...])
    o = _sigmoid(ga_ref[...].astype(F32)) * a + _sigmoid(gb_ref[...].astype(F32)) * b
    o_ref[...] = o.astype(o_ref.dtype)


def _merge(o_mla, o_hgrn, w_a, w_b, hrest, gate_off):
    rows, k = o_mla.shape
    d = w_a.shape[1]
    tm = _row_tile(rows, 640)
    tn = 512
    ga_off = gate_off // tn
    gb_off = (gate_off + d) // tn
    return pl.pallas_call(
        _merge_kernel, name="branch_merge",
        grid=(d // tn, rows // tm),
        in_specs=[pl.BlockSpec((tm, k), lambda n, m: (m, 0)), pl.BlockSpec((tm, k), lambda n, m: (m, 0)),
                  pl.BlockSpec((k, tn), lambda n, m: (0, n)), pl.BlockSpec((k, tn), lambda n, m: (0, n)),
                  pl.BlockSpec((tm, tn), lambda n, m: (m, n + ga_off)),
                  pl.BlockSpec((tm, tn), lambda n, m: (m, n + gb_off))],
        out_specs=pl.BlockSpec((tm, tn), lambda n, m: (m, n)),
        out_shape=jax.ShapeDtypeStruct((rows, d), BF16),
        compiler_params=_cparams(2),
    )(o_mla, o_hgrn, w_a, w_b, hrest, hrest)


def _out_proj_kernel(mg_ref, w_ref, h_ref, g_ref, h2_ref, u2_ref):
    h2 = h_ref[...] + _dot(mg_ref[...], w_ref[...])
    h2_ref[...] = h2
    u2_ref[...] = _rms(h2, g_ref[...]).astype(u2_ref.dtype)


def _out_proj(merged, w_out, h, g_ffn):
    rows, d = h.shape
    tm = _row_tile(rows, 320)
    row_spec = pl.BlockSpec((tm, d), lambda i: (i, 0))
    return pl.pallas_call(
        _out_proj_kernel, name="mix_out_proj",
        grid=(rows // tm,),
        in_specs=[row_spec, pl.BlockSpec((d, d), lambda i: (0, 0)), row_spec, pl.BlockSpec((1, d), lambda i: (0, 0))],
        out_specs=[row_spec, row_spec],
        out_shape=[jax.ShapeDtypeStruct((rows, d), F32), jax.ShapeDtypeStruct((rows, d), BF16)],
        compiler_params=_cparams(1),
    )(merged, w_out, h, g_ffn)


HALO = 16


def _ffn_in_kernel(u_ref, halo_ref, wg_ref, wu_ref, cw_ref, cb_ref, o_ref):
    u = u_ref[...]
    wg = wg_ref[...]
    gate = _dot(u, wg)
    up = _dot(u, wu_ref[...])
    prev = _dot(halo_ref[...], wg)
    row = lax.broadcasted_iota(jnp.int32, gate.shape, 0)
    back1 = jnp.where(row == 0, prev[HALO - 1:HALO, :], pltpu.roll(gate, 1, 0))
    back2 = jnp.where(row == 0, prev[HALO - 2:HALO - 1, :],
                      jnp.where(row == 1, prev[HALO - 1:HALO, :], pltpu.roll(gate, 2, 0)))
    cw = cw_ref[...]
    conv = cw[0:1, :] * back2 + cw[1:2, :] * back1 + cw[2:3, :] * gate + cb_ref[...]
    o_ref[...] = (_silu(conv) * up).astype(o_ref.dtype)


def _ffn_in(u2, w_fi, conv_w, conv_b, s_real):
    rows, d = u2.shape
    tm = 512
    tn = 512
    up_off = D_FF // tn
    n_halo = rows // HALO
    halo_map = lambda n, m: ((m * (tm // HALO) + n_halo - 1) % n_halo, 0)
    return pl.pallas_call(
        _ffn_in_kernel, name="ffn_in_conv_gate",
        grid=(D_FF // tn, s_real // tm),
        in_specs=[pl.BlockSpec((tm, d), lambda n, m: (m, 0)), pl.BlockSpec((HALO, d), halo_map),
                  pl.BlockSpec((d, tn), lambda n, m: (0, n)), pl.BlockSpec((d, tn), lambda n, m: (0, n + up_off)),
                  pl.BlockSpec((conv_w.shape[0], tn), lambda n, m: (0, n)), pl.BlockSpec((1, tn), lambda n, m: (0, n))],
        out_specs=pl.BlockSpec((tm, tn), lambda n, m: (m, n)),
        out_shape=jax.ShapeDtypeStruct((s_real, D_FF), BF16),
        compiler_params=_cparams(2),
    )(u2, u2, w_fi, w_fi, conv_w, conv_b)


def _ffn_out_kernel(a_ref, w_ref, h2_ref, g_ref, o_ref):
    kk = pl.program_id(1)

    @pl.when(kk == 0)
    def _():
        o_ref[...] = h2_ref[...]

    o_ref[...] += _dot(a_ref[...], w_ref[...])

    @pl.when(kk == pl.num_programs(1) - 1)
    def _():
        o_ref[...] = _rms(o_ref[...], g_ref[...])


def _ffn_out(act, w_fo, h2, g_final):
    s_real, k = act.shape
    d = w_fo.shape[1]
    tm = 512
    tk = 512
    return pl.pallas_call(
        _ffn_out_kernel, name="ffn_out_final_norm",
        grid=(s_real // tm, k // tk),
        in_specs=[pl.BlockSpec((tm, tk), lambda m, kk: (m, kk)), pl.BlockSpec((tk, d), lambda m, kk: (kk, 0)),
                  pl.BlockSpec((tm, d), lambda m, kk: (m, 0)), pl.BlockSpec((1, d), lambda m, kk: (0, 0))],
        out_specs=pl.BlockSpec((tm, d), lambda m, kk: (m, 0)),
        out_shape=jax.ShapeDtypeStruct((s_real, d), F32),
        compiler_params=_cparams(2),
    )(act, w_fo, h2, g_final)


def kernel(x, positions, meta_tokens, w_in, w_q_up, w_kv_up, w_branch_mla, w_branch_hgrn, w_out, w_ffn_in,
           w_ffn_out, conv_w, conv_b, g_mix_norm, g_q_norm, g_kv_norm, g_hgrn_norm, g_ffn_norm, g_final_norm,
           lb_raw):
    b, s_real, d = x.shape
    assert b == 1 and w_in.shape[0] == 1 and s_real % 512 == 0
    dt = x.dtype

    h = jnp.concatenate([x[0], jnp.zeros((PAD_LEN, d), dt), meta_tokens.astype(dt)], axis=0)
    pos = jnp.concatenate([positions[0].astype(jnp.int32) + N_META, jnp.zeros((PAD_LEN,), jnp.int32),
                           jnp.arange(N_META, dtype=jnp.int32)])
    inv = 1.0 / (ROPE_THETA ** (jnp.arange(0, QK_ROPE_DIM, 2, dtype=F32) / QK_ROPE_DIM))
    inv = jnp.concatenate([inv, inv, jnp.zeros((LANES - QK_ROPE_DIM,), F32)])[None, :]

    lat_end = Q_LORA_RANK + KV_LORA_RANK + QK_ROPE_DIM
    w_lat = jnp.pad(w_in[0, :, :lat_end], ((0, 0), (0, LANES - QK_ROPE_DIM))).astype(BF16)
    w_rest = w_in[0, :, lat_end:].astype(BF16)
    w_q = jnp.pad(w_q_up[0].reshape(Q_LORA_RANK, MLA_HEADS, QK_HEAD_DIM),
                  ((0, 0), (0, 0), (0, Q_CAT - QK_HEAD_DIM))).reshape(Q_LORA_RANK, MLA_HEADS * Q_CAT).astype(BF16)
    w_kv3 = w_kv_up[0].reshape(KV_LORA_RANK, MLA_HEADS, QK_NOPE_DIM + V_HEAD_DIM)
    w_kv = jnp.concatenate([w_kv3[:, :, :QK_NOPE_DIM].reshape(KV_LORA_RANK, -1),
                            w_kv3[:, :, QK_NOPE_DIM:].reshape(KV_LORA_RANK, -1)], axis=1).astype(BF16)
    w_a = w_branch_mla[0].astype(BF16)
    w_b = w_branch_hgrn[0].astype(BF16)
    w_o = w_out[0].astype(BF16)
    w_fi = w_ffn_in[0].astype(BF16)
    w_fo = w_ffn_out[0].astype(BF16)

    cos_t, sin_t = _rope_tables(pos.astype(F32)[:, None], inv)

    u = _norm_cast(h, g_mix_norm)
    qn, kvn, kr = _proj_lat(u, w_lat, g_q_norm, g_kv_norm, cos_t, sin_t)
    hq = _matmul_cols(u, w_rest, 0, d, BF16, 1024, "proj_hgrn_q")
    hf = _matmul_cols(u, w_rest, d, d, F32, 1024, "proj_hgrn_forget")
    hrest = _matmul_cols(u, w_rest, 2 * d, 4 * d, BF16, 1024, "proj_hgrn_rest")
    q_cat = _q_up(qn, w_q, cos_t, sin_t)
    kv = _kv_up(kvn, w_kv)
    o_mla = _attention(q_cat, kv, kr, s_real)
    o_hgrn = _hgrn(hq, hf, hrest, lb_raw, g_hgrn_norm)
    merged = _merge(o_mla, o_hgrn, w_a, w_b, hrest, 2 * d)
    h2, u2 = _out_proj(merged, w_o, h, g_ffn_norm)

    act = _ffn_in(u2, w_fi, conv_w[0], conv_b, s_real)
    out = _ffn_out(act, w_fo, h2, g_final_norm[None, :])
    return out[None]
```

```python
import functools

import jax
import jax.numpy as jnp
from jax import lax
from jax.experimental import pallas as pl
from jax.experimental.pallas import tpu as pltpu

F32 = jnp.float32
BF16 = jnp.bfloat16

N_META = 16
BLOCK = 128
PAD_LEN = BLOCK - N_META
MLA_HEADS = 16
Q_LORA_RANK = 1536
KV_LORA_RANK = 512
QK_NOPE_DIM = 128
QK_ROPE_DIM = 64
QK_HEAD_DIM = QK_NOPE_DIM + QK_ROPE_DIM
V_HEAD_DIM = 128
ROPE_THETA = 10000.0
HGRN_HEADS = 16
HGRN_EXPAND = 128
HGRN_V_DIM = 128
D_FF = 5632
NORM_EPS = 1e-6

LANES = 128
Q_CAT = 2 * LANES
LAT_COLS = Q_LORA_RANK + KV_LORA_RANK + LANES
LOG2E = 1.4426950408889634
VMEM_LIMIT = 52 * 1024 * 1024


def _cparams(n_axes):
    return pltpu.CompilerParams(dimension_semantics=("arbitrary",) * n_axes, vmem_limit_bytes=VMEM_LIMIT)


def _row_tile(rows, target):
    for t in range(target, 0, -LANES):
        if rows % t == 0:
            return t
    raise ValueError(f"no 128-multiple row tile for {rows}")


def _dot(a, b):
    return jnp.dot(a, b, preferred_element_type=F32)


def _dot_nt(a, b):
    return lax.dot_general(a, b, (((1,), (1,)), ((), ())), preferred_element_type=F32)


def _dot_tn(a, b):
    return lax.dot_general(a, b, (((0,), (0,)), ((), ())), preferred_element_type=F32)


def _rms(x, g):
    return x * lax.rsqrt(jnp.mean(x * x, axis=-1, keepdims=True) + NORM_EPS) * g


def _sigmoid(x):
    return 1.0 / (1.0 + jnp.exp(-x))


def _silu(x):
    return x * _sigmoid(x)


def _rope(x, cos_t, sin_t):
    rot = pltpu.roll(x, 32, 1) - pltpu.roll(x, 96, 1)
    return x * cos_t + rot * sin_t


def _rope_table_kernel(pos_ref, inv_ref, cos_ref, sin_ref):
    ang = pos_ref[...] * inv_ref[...]
    keep = lax.broadcasted_iota(jnp.int32, ang.shape, 1) < QK_ROPE_DIM
    cos_ref[...] = jnp.where(keep, jnp.cos(ang), 0.0)
    sin_ref[...] = jnp.where(keep, jnp.sin(ang), 0.0)


def _rope_tables(pos_f, inv):
    rows = pos_f.shape[0]
    tm = _row_tile(rows, 640)
    spec = pl.BlockSpec((tm, LANES), lambda i: (i, 0))
    return pl.pallas_call(
        _rope_table_kernel, name="rope_tables",
        grid=(rows // tm,),
        in_specs=[pl.BlockSpec((tm, 1), lambda i: (i, 0)), pl.BlockSpec((1, LANES), lambda i: (0, 0))],
        out_specs=[spec, spec],
        out_shape=[jax.ShapeDtypeStruct((rows, LANES), F32)] * 2,
        compiler_params=_cparams(1),
    )(pos_f, inv)


LAT_END = Q_LORA_RANK + KV_LORA_RANK + QK_ROPE_DIM


def _w_lat_kernel(w_ref, o_ref):
    lane = lax.broadcasted_iota(jnp.int32, w_ref.shape, 1)
    o_ref[...] = jnp.where(lane < LAT_END, w_ref[...], 0.0).astype(o_ref.dtype)


def _w_lat(w_in):
    d = w_in.shape[1]
    tr = 512
    return pl.pallas_call(
        _w_lat_kernel, name="w_latent_cast",
        grid=(d // tr,),
        in_specs=[pl.BlockSpec((None, tr, LAT_COLS), lambda r: (0, r, 0))],
        out_specs=pl.BlockSpec((tr, LAT_COLS), lambda r: (r, 0)),
        out_shape=jax.ShapeDtypeStruct((d, LAT_COLS), BF16),
        compiler_params=_cparams(1),
    )(w_in)


def _w_rest_kernel(a_ref, b_ref, o_ref):
    shift = LAT_END % LANES
    cat = jnp.concatenate([a_ref[...], b_ref[...]], axis=1)
    o_ref[...] = cat[:, shift:shift + o_ref.shape[1]].astype(o_ref.dtype)


def _w_rest(w_in):
    d, n_in = w_in.shape[1], w_in.shape[2]
    n_out = n_in - LAT_END
    tr, tn = 512, 1024
    base = LAT_END - LAT_END % LANES
    assert base % tn == 0 and n_out % tn == 0
    return pl.pallas_call(
        _w_rest_kernel, name="w_rest_realign",
        grid=(n_out // tn, d // tr),
        in_specs=[pl.BlockSpec((None, tr, tn), lambda n, r: (0, r, base // tn + n)),
                  pl.BlockSpec((None, tr, LANES), lambda n, r: (0, r, (base + (n + 1) * tn) // LANES))],
        out_specs=pl.BlockSpec((tr, tn), lambda n, r: (r, n)),
        out_shape=jax.ShapeDtypeStruct((d, n_out), BF16),
        compiler_params=_cparams(2),
    )(w_in, w_in)


def _norm_cast_kernel(x_ref, g_ref, o_ref):
    o_ref[...] = _rms(x_ref[...], g_ref[...]).astype(o_ref.dtype)


def _norm_cast(x, g):
    rows, d = x.shape
    tm = _row_tile(rows, 640)
    return pl.pallas_call(
        _norm_cast_kernel, name="mix_norm",
        grid=(rows // tm,),
        in_specs=[pl.BlockSpec((tm, d), lambda i: (i, 0)), pl.BlockSpec((1, d), lambda i: (0, 0))],
        out_specs=pl.BlockSpec((tm, d), lambda i: (i, 0)),
        out_shape=jax.ShapeDtypeStruct((rows, d), BF16),
        compiler_params=_cparams(1),
    )(x, g)


def _proj_lat_kernel(u_ref, w_ref, gq_ref, gkv_ref, cos_ref, sin_ref, qn_ref, kvn_ref, kr_ref):
    y = _dot(u_ref[...], w_ref[...])
    kv_end = Q_LORA_RANK + KV_LORA_RANK
    qn_ref[...] = _rms(y[:, :Q_LORA_RANK], gq_ref[...]).astype(qn_ref.dtype)
    kvn_ref[...] = _rms(y[:, Q_LORA_RANK:kv_end], gkv_ref[...]).astype(kvn_ref.dtype)
    kr_ref[...] = _rope(y[:, kv_end:], cos_ref[...], sin_ref[...]).astype(kr_ref.dtype)


def _proj_lat(u, w_lat, g_q, g_kv, cos_t, sin_t):
    rows, d = u.shape
    tm = _row_tile(rows, 640)
    row_spec = lambda n: pl.BlockSpec((tm, n), lambda i: (i, 0))
    full_spec = lambda r, n: pl.BlockSpec((r, n), lambda i: (0, 0))
    return pl.pallas_call(
        _proj_lat_kernel, name="proj_latents",
        grid=(rows // tm,),
        in_specs=[row_spec(d), full_spec(d, LAT_COLS), full_spec(1, Q_LORA_RANK), full_spec(1, KV_LORA_RANK),
                  row_spec(LANES), row_spec(LANES)],
        out_specs=[row_spec(Q_LORA_RANK), row_spec(KV_LORA_RANK), row_spec(LANES)],
        out_shape=[jax.ShapeDtypeStruct((rows, Q_LORA_RANK), BF16),
                   jax.ShapeDtypeStruct((rows, KV_LORA_RANK), BF16),
                   jax.ShapeDtypeStruct((rows, LANES), BF16)],
        compiler_params=_cparams(1),
    )(u, w_lat, g_q, g_kv, cos_t, sin_t)


def _matmul_kernel(x_ref, w_ref, o_ref):
    o_ref[...] = _dot(x_ref[...], w_ref[...]).astype(o_ref.dtype)


def _matmul_cols(x, w, col_start, n_cols, out_dtype, tn, name):
    rows, k = x.shape
    tm = _row_tile(rows, 640)
    off = col_start // tn
    return pl.pallas_call(
        _matmul_kernel, name=name,
        grid=(n_cols // tn, rows // tm),
        in_specs=[pl.BlockSpec((tm, k), lambda n, m: (m, 0)), pl.BlockSpec((k, tn), lambda n, m: (0, n + off))],
        out_specs=pl.BlockSpec((tm, tn), lambda n, m: (m, n)),
        out_shape=jax.ShapeDtypeStruct((rows, n_cols), out_dtype),
        compiler_params=_cparams(2),
    )(x, w)


def _kv_up_kernel(x_ref, w_ref, o_ref):
    y = _dot(x_ref[...], w_ref[...])
    for hh in range(o_ref.shape[0]):
        o_ref[hh] = y[:, hh * LANES:(hh + 1) * LANES].astype(o_ref.dtype)


def _kv_up(kvn, w_kv):
    rows, k = kvn.shape
    n_slabs = w_kv.shape[1] // LANES
    tm = _row_tile(rows, 640)
    per_tile = 8
    return pl.pallas_call(
        _kv_up_kernel, name="kv_up",
        grid=(n_slabs // per_tile, rows // tm),
        in_specs=[pl.BlockSpec((tm, k), lambda n, m: (m, 0)),
                  pl.BlockSpec((k, per_tile * LANES), lambda n, m: (0, n))],
        out_specs=pl.BlockSpec((per_tile, tm, LANES), lambda n, m: (n, m, 0)),
        out_shape=jax.ShapeDtypeStruct((n_slabs, rows, LANES), BF16),
        compiler_params=_cparams(2),
    )(kvn, w_kv)


def _q_up_kernel(x_ref, w_ref, cos_ref, sin_ref, o_ref, *, heads_per_tile):
    y = _dot(x_ref[...], w_ref[...]) * (QK_HEAD_DIM ** -0.5 * LOG2E)
    cos_t = cos_ref[...]
    sin_t = sin_ref[...]
    for hh in range(heads_per_tile):
        lo = hh * Q_CAT
        o_ref[:, lo:lo + LANES] = y[:, lo:lo + LANES].astype(o_ref.dtype)
        o_ref[:, lo + LANES:lo + Q_CAT] = _rope(y[:, lo + LANES:lo + Q_CAT], cos_t, sin_t).astype(o_ref.dtype)


def _q_up(qn, w_q, cos_t, sin_t):
    rows, k = qn.shape
    n_cols = w_q.shape[1]
    tm = _row_tile(rows, 640)
    tn = 4 * Q_CAT
    return pl.pallas_call(
        functools.partial(_q_up_kernel, heads_per_tile=tn // Q_CAT), name="q_up_rope",
        grid=(n_cols // tn, rows // tm),
        in_specs=[pl.BlockSpec((tm, k), lambda n, m: (m, 0)), pl.BlockSpec((k, tn), lambda n, m: (0, n)),
                  pl.BlockSpec((tm, LANES), lambda n, m: (m, 0)), pl.BlockSpec((tm, LANES), lambda n, m: (m, 0))],
        out_specs=pl.BlockSpec((tm, tn), lambda n, m: (m, n)),
        out_shape=jax.ShapeDtypeStruct((rows, n_cols), BF16),
        compiler_params=_cparams(2),
    )(qn, w_q, cos_t, sin_t)


def _attn_kernel(q_ref, kn_ref, v_ref, kr_ref, o_ref, kcat_ref, m_ref, l_ref, acc_ref, sa_ref, sb_ref, *, tq,
                 n_real_tiles):
    step_id = pl.program_id(1)
    qi = step_id - 1
    s_real = n_real_tiles * tq

    @pl.when(step_id == 0)
    def _():
        kcat_ref[:, :LANES] = kn_ref[...]
        kcat_ref[:, LANES:] = kr_ref[...]

    th = tq // 2
    halves = (0, th)

    def init():
        m_ref[...] = jnp.full(m_ref.shape, -jnp.inf, F32)
        l_ref[...] = jnp.zeros(l_ref.shape, F32)
        acc_ref[...] = jnp.zeros(acc_ref.shape, F32)

    def update(r0, src_ref, n_keys, v_blk, visible=None):
        rows = slice(r0, r0 + th)

        def chunk(c):
            s = src_ref[rows, c * LANES:(c + 1) * LANES]
            return s if visible is None else jnp.where(visible(c), s, -jnp.inf)

        n_chunks = n_keys // LANES
        m_prev = m_ref[rows, :]
        m_cur = functools.reduce(jnp.maximum, [chunk(c) for c in range(n_chunks)])
        m_new = jnp.maximum(m_prev, jnp.max(m_cur, axis=1, keepdims=True))
        alpha = jnp.exp2(m_prev - m_new)
        p_chunks = [jnp.exp2(chunk(c) - m_new) for c in range(n_chunks)]
        l_cur = functools.reduce(jnp.add, p_chunks)
        l_ref[rows, :] = alpha * l_ref[rows, :] + jnp.sum(l_cur, axis=1, keepdims=True)
        p = jnp.concatenate(p_chunks, axis=1) if n_chunks > 1 else p_chunks[0]
        acc_ref[rows, :] = alpha * acc_ref[rows, :] + _dot(p.astype(v_blk.dtype), v_blk)
        m_ref[rows, :] = m_new

    def finalize():
        o_ref[...] = (acc_ref[...] / l_ref[...]).astype(o_ref.dtype)

    def prefix_block(visible_fn):
        for r0 in halves:
            sa_ref[r0:r0 + th, :BLOCK] = _dot_nt(q_ref[r0:r0 + th, :], kcat_ref[s_real:s_real + BLOCK, :])
        for r0 in halves:
            update(r0, sa_ref, BLOCK, v_ref[s_real:s_real + BLOCK, :], visible_fn(r0))

    @pl.when(step_id > 0)
    def _():
        init()
        col = lax.broadcasted_iota(jnp.int32, (th, BLOCK), 1)
        prefix_block(lambda r0: lambda c: col >= PAD_LEN)

        def raw_scores(dst_ref, j):
            k_blk = kcat_ref[pl.ds(pl.multiple_of(j * tq, tq), tq), :]
            for r0 in halves:
                dst_ref[r0:r0 + th, :] = _dot_nt(q_ref[r0:r0 + th, :], k_blk)

        def step(j, src_ref, dst_ref):
            raw_scores(dst_ref, j + 1)
            v_blk = v_ref[pl.ds(pl.multiple_of(j * tq, tq), tq), :]
            for r0 in halves:
                update(r0, src_ref, tq, v_blk)

        def diagonal(src_ref):
            v_blk = v_ref[pl.ds(pl.multiple_of(qi * tq, tq), tq), :]
            row = lax.broadcasted_iota(jnp.int32, (th, LANES), 0)
            lane = lax.broadcasted_iota(jnp.int32, (th, LANES), 1)
            for r0 in halves:
                update(r0, src_ref, tq, v_blk, lambda c, r0=r0: lane + c * LANES <= row + r0)
            finalize()

        raw_scores(sb_ref, 0)

        def body(i, carry):
            step(2 * i, sb_ref, sa_ref)
            step(2 * i + 1, sa_ref, sb_ref)
            return carry

        lax.fori_loop(0, qi // 2, body, 0)

        @pl.when(qi % 2 == 1)
        def _():
            step(qi - 1, sb_ref, sa_ref)
            diagonal(sa_ref)

        @pl.when(qi % 2 == 0)
        def _():
            diagonal(sb_ref)

    @pl.when(step_id == 0)
    def _():
        init()
        row = lax.broadcasted_iota(jnp.int32, (th, BLOCK), 0)
        col = lax.broadcasted_iota(jnp.int32, (th, BLOCK), 1)
        prefix_block(lambda r0: lambda c: (col <= row + r0) & ((col >= PAD_LEN) | (col == row + r0)))
        finalize()


def _attention(q_cat, kv, kr, s_real):
    rows = q_cat.shape[0]
    tq = 512
    n_real_tiles = s_real // tq
    q_tile = lambda h, i: ((i + n_real_tiles) % (n_real_tiles + 1), h)
    return pl.pallas_call(
        functools.partial(_attn_kernel, tq=tq, n_real_tiles=n_real_tiles), name="mla_attention",
        grid=(MLA_HEADS, n_real_tiles + 1),
        in_specs=[pl.BlockSpec((tq, Q_CAT), q_tile),
                  pl.BlockSpec((None, rows, LANES), lambda h, i: (h, 0, 0)),
                  pl.BlockSpec((None, rows, LANES), lambda h, i: (MLA_HEADS + h, 0, 0)),
                  pl.BlockSpec((rows, LANES), lambda h, i: (0, 0))],
        out_specs=pl.BlockSpec((tq, V_HEAD_DIM), q_tile),
        out_shape=jax.ShapeDtypeStruct((rows, MLA_HEADS * V_HEAD_DIM), BF16),
        scratch_shapes=[pltpu.VMEM((rows, Q_CAT), BF16), pltpu.VMEM((tq, LANES), F32),
                        pltpu.VMEM((tq, LANES), F32), pltpu.VMEM((tq, V_HEAD_DIM), F32),
                        pltpu.VMEM((tq, tq), F32), pltpu.VMEM((tq, tq), F32)],
        compiler_params=_cparams(2),
    )(q_cat, kv, kv, kr)


def _split3(x):
    x1 = x.astype(BF16)
    r1 = x - x1.astype(F32)
    x2 = r1.astype(BF16)
    x3 = (r1 - x2.astype(F32)).astype(BF16)
    return x1, x2, x3


def _block_row(x, size, j):
    c, n = x.shape
    g = x.reshape(c // size, size, n)[:, j:j + 1, :]
    return jnp.broadcast_to(g, (c // size, size, n)).reshape(c, n)


HGRN_LEVELS = 7


def _hgrn_head(hq, hf, hi, hg, lb_raw, gain, st, valid, row, lev):
    n = BLOCK
    top = jnp.max(lb_raw, axis=0, keepdims=True)
    e = jnp.exp(lb_raw - top)
    lb = e[0:1, :] / jnp.sum(e, axis=0, keepdims=True)

    f = lb + (1.0 - lb) * _sigmoid(hf)
    f_eff = jnp.where(valid, f, 1.0)
    g = jnp.where(valid, jnp.log(f), 0.0)
    k = jnp.where(valid, 1.0 - f, 0.0)
    q = _silu(hq.astype(F32))

    col = lax.broadcasted_iota(jnp.int32, (n, n), 1)
    tri = (col <= row).astype(BF16)
    g1, g2, g3 = _split3(g)
    b = _dot(tri, g1) + _dot(tri, g2) + _dot(tri, g3)

    a = jnp.where(lev == 0, jnp.sum(q * k, axis=1, keepdims=True), 0.0)
    for level in range(1, HGRN_LEVELS + 1):
        size = 1 << level
        if level == 1:
            w = jnp.where((row & 1) == 1, f_eff, 1.0)
        elif level == 2:
            r4 = row & 3
            up1 = pltpu.roll(g, n - 1, 0)
            dn1 = pltpu.roll(g, 1, 0)
            w = jnp.exp(jnp.where(r4 == 0, up1, jnp.where(r4 == 1, 0.0, jnp.where(r4 == 2, g, g + dn1))))
        else:
            w = jnp.exp(-jnp.abs(b - _block_row(b, size, size // 2 - 1)))
        a_l = _dot_nt((q * w).astype(BF16), (k * w).astype(BF16))
        a = jnp.where(lev == level, a_l, a)

    o = _dot(a.astype(BF16), hi) + _dot_nt((q * jnp.exp(b)).astype(BF16), st.astype(BF16))
    b_last = b[n - 1:n, :]
    st_new = st * jnp.exp(b_last) + _dot_tn(hi, (k * jnp.exp(b_last - b)).astype(BF16))
    out = _rms(o, gain) * _silu(hg.astype(F32))
    return out, st_new


def _hgrn_kernel(hq_ref, hf_ref, hi_ref, hg_ref, lb_ref, g_ref, o_ref, st_ref):
    c = pl.program_id(0)
    n = BLOCK

    @pl.when(c == 0)
    def _():
        st_ref[...] = jnp.zeros(st_ref.shape, F32)

    row = lax.broadcasted_iota(jnp.int32, (n, n), 0)
    col = lax.broadcasted_iota(jnp.int32, (n, n), 1)
    valid = (c > 0) | (row >= PAD_LEN)
    x = row ^ col
    lev = functools.reduce(jnp.add, [(x >= (1 << i)).astype(jnp.int32) for i in range(HGRN_LEVELS)])
    lev = jnp.where(col > row, -1, lev)
    gain = g_ref[...]

    def head_pair(i, carry):
        for hh in range(2):
            head = 2 * i + hh
            cols = pl.ds(pl.multiple_of(head * BLOCK, BLOCK), BLOCK)
            out, st_new = _hgrn_head(hq_ref[:, cols], hf_ref[:, cols], hi_ref[:, cols], hg_ref[:, cols],
                                     lb_ref[:, cols], gain, st_ref[head], valid, row, lev)
            st_ref[head] = st_new
            o_ref[:, cols] = out.astype(o_ref.dtype)
        return carry

    lax.fori_loop(0, HGRN_HEADS // 2, head_pair, 0)


def _hgrn(hq, hf, hrest, lb_raw, g_hgrn):
    rows, width = hq.shape
    n_chunks = rows // BLOCK
    blk = lambda off: pl.BlockSpec((BLOCK, width), lambda c: ((c + n_chunks - 1) % n_chunks, off))
    return pl.pallas_call(
        _hgrn_kernel, name="hgrn2",
        grid=(n_chunks,),
        in_specs=[blk(0), blk(0), blk(0), blk(1),
                  pl.BlockSpec((lb_raw.shape[0], width), lambda c: (0, 0)),
                  pl.BlockSpec((1, HGRN_V_DIM), lambda c: (0, 0))],
        out_specs=blk(0),
        out_shape=jax.ShapeDtypeStruct((rows, width), BF16),
        scratch_shapes=[pltpu.VMEM((HGRN_HEADS, HGRN_V_DIM, HGRN_EXPAND), F32)],
        compiler_params=_cparams(1),
    )(hq, hf, hrest, hrest, lb_raw, g_hgrn)


def _merge_kernel(om_ref, oh_ref, wa_ref, wb_ref, ga_ref, gb_ref, o_ref, wa_bf, wb_bf):
    @pl.when(pl.program_id(1) == 0)
    def _():
        wa_bf[...] = wa_ref[...].astype(BF16)
        wb_bf[...] = wb_ref[...].astype(BF16)

    a = _dot(om_ref[...], wa_bf[...])
    b = _dot(oh_ref[...], wb_bf[...])
    o = _sigmoid(ga_ref[...].astype(F32)) * a + _sigmoid(gb_ref[...].astype(F32)) * b
    o_ref[...] = o.astype(o_ref.dtype)


def _merge(o_mla, o_hgrn, w_a, w_b, hrest, gate_off):
    rows, k = o_mla.shape
    d = w_a.shape[2]
    tm = _row_tile(rows, 640)
    tn = 512
    ga_off = gate_off // tn
    gb_off = (gate_off + d) // tn
    w_spec = pl.BlockSpec((None, k, tn), lambda n, m: (0, 0, n))
    return pl.pallas_call(
        _merge_kernel, name="branch_merge",
        grid=(d // tn, rows // tm),
        in_specs=[pl.BlockSpec((tm, k), lambda n, m: (m, 0)), pl.BlockSpec((tm, k), lambda n, m: (m, 0)),
                  w_spec, w_spec,
                  pl.BlockSpec((tm, tn), lambda n, m: (m, n + ga_off)),
                  pl.BlockSpec((tm, tn), lambda n, m: (m, n + gb_off))],
        out_specs=pl.BlockSpec((tm, tn), lambda n, m: (m, n)),
        out_shape=jax.ShapeDtypeStruct((rows, d), BF16),
        scratch_shapes=[pltpu.VMEM((k, tn), BF16), pltpu.VMEM((k, tn), BF16)],
        compiler_params=_cparams(2),
    )(o_mla, o_hgrn, w_a, w_b, hrest, hrest)


def _out_proj_kernel(mg_ref, w_ref, h_ref, g_ref, h2_ref, u2_ref):
    h2 = h_ref[...] + _dot(mg_ref[...], w_ref[...])
    h2_ref[...] = h2
    u2_ref[...] = _rms(h2, g_ref[...]).astype(u2_ref.dtype)


def _out_proj(merged, w_out, h, g_ffn):
    rows, d = h.shape
    tm = _row_tile(rows, 320)
    row_spec = pl.BlockSpec((tm, d), lambda i: (i, 0))
    return pl.pallas_call(
        _out_proj_kernel, name="mix_out_proj",
        grid=(rows // tm,),
        in_specs=[row_spec, pl.BlockSpec((d, d), lambda i: (0, 0)), row_spec, pl.BlockSpec((1, d), lambda i: (0, 0))],
        out_specs=[row_spec, row_spec],
        out_shape=[jax.ShapeDtypeStruct((rows, d), F32), jax.ShapeDtypeStruct((rows, d), BF16)],
        compiler_params=_cparams(1),
    )(merged, w_out, h, g_ffn)


HALO = 16


def _ffn_in_kernel(u_ref, halo_ref, wg_ref, wu_ref, cw_ref, cb_ref, o_ref, wg_bf, wu_bf, *, n_parts):
    @pl.when(pl.program_id(1) == 0)
    def _():
        wg_bf[...] = wg_ref[...].astype(BF16)
        wu_bf[...] = wu_ref[...].astype(BF16)

    wg = wg_bf[...]
    wu = wu_bf[...]
    cw = cw_ref[...]
    cb = cb_ref[...]
    tp = o_ref.shape[0] // n_parts
    row = lax.broadcasted_iota(jnp.int32, (tp, o_ref.shape[1]), 0)
    tail = _dot(halo_ref[...], wg)[HALO - 2:HALO, :]
    for part in range(n_parts):
        u = u_ref[part * tp:(part + 1) * tp, :]
        gate = _dot(u, wg)
        up = _dot(u, wu)
        back1 = jnp.where(row == 0, tail[1:2, :], pltpu.roll(gate, 1, 0))
        back2 = jnp.where(row == 0, tail[0:1, :], jnp.where(row == 1, tail[1:2, :], pltpu.roll(gate, 2, 0)))
        conv = cw[0:1, :] * back2 + cw[1:2, :] * back1 + cw[2:3, :] * gate + cb
        o_ref[part * tp:(part + 1) * tp, :] = (_silu(conv) * up).astype(o_ref.dtype)
        tail = gate[tp - 2:tp, :]


def _ffn_in(u2, w_fi, conv_w, conv_b, s_real):
    rows, d = u2.shape
    tm = 1024
    tn = 512
    up_off = D_FF // tn
    n_halo = rows // HALO
    halo_map = lambda n, m: ((m * (tm // HALO) + n_halo - 1) % n_halo, 0)
    return pl.pallas_call(
        functools.partial(_ffn_in_kernel, n_parts=2), name="ffn_in_conv_gate",
        grid=(D_FF // tn, s_real // tm),
        in_specs=[pl.BlockSpec((tm, d), lambda n, m: (m, 0)), pl.BlockSpec((HALO, d), halo_map),
                  pl.BlockSpec((None, d, tn), lambda n, m: (0, 0, n)),
                  pl.BlockSpec((None, d, tn), lambda n, m: (0, 0, n + up_off)),
                  pl.BlockSpec((None, conv_w.shape[1], tn), lambda n, m: (0, 0, n)),
                  pl.BlockSpec((1, tn), lambda n, m: (0, n))],
        out_specs=pl.BlockSpec((tm, tn), lambda n, m: (m, n)),
        out_shape=jax.ShapeDtypeStruct((s_real, D_FF), BF16),
        scratch_shapes=[pltpu.VMEM((d, tn), BF16), pltpu.VMEM((d, tn), BF16)],
        compiler_params=_cparams(2),
    )(u2, u2, w_fi, w_fi, conv_w, conv_b)


def _ffn_out_kernel(a_ref, w_ref, h2_ref, g_ref, o_ref):
    kk = pl.program_id(1)

    @pl.when(kk == 0)
    def _():
        o_ref[...] = h2_ref[...]

    o_ref[...] += _dot(a_ref[...], w_ref[...])

    @pl.when(kk == pl.num_programs(1) - 1)
    def _():
        o_ref[...] = _rms(o_ref[...], g_ref[...])


def _ffn_out(act, w_fo, h2, g_final):
    s_real, k = act.shape
    d = w_fo.shape[1]
    tm = 512
    tk = k // 4
    assert tk % LANES == 0
    return pl.pallas_call(
        _ffn_out_kernel, name="ffn_out_final_norm",
        grid=(s_real // tm, k // tk),
        in_specs=[pl.BlockSpec((tm, tk), lambda m, kk: (m, kk)), pl.BlockSpec((tk, d), lambda m, kk: (kk, 0)),
                  pl.BlockSpec((tm, d), lambda m, kk: (m, 0)), pl.BlockSpec((1, d), lambda m, kk: (0, 0))],
        out_specs=pl.BlockSpec((tm, d), lambda m, kk: (m, 0)),
        out_shape=jax.ShapeDtypeStruct((s_real, d), F32),
        compiler_params=_cparams(2),
    )(act, w_fo, h2, g_final)


def kernel(x, positions, meta_tokens, w_in, w_q_up, w_kv_up, w_branch_mla, w_branch_hgrn, w_out, w_ffn_in,
           w_ffn_out, conv_w, conv_b, g_mix_norm, g_q_norm, g_kv_norm, g_hgrn_norm, g_ffn_norm, g_final_norm,
           lb_raw):
    b, s_real, d = x.shape
    assert b == 1 and w_in.shape[0] == 1 and s_real % 512 == 0
    dt = x.dtype

    h = jnp.concatenate([x[0], jnp.zeros((PAD_LEN, d), dt), meta_tokens.astype(dt)], axis=0)
    pos = jnp.concatenate([positions[0].astype(jnp.int32) + N_META, jnp.zeros((PAD_LEN,), jnp.int32),
                           jnp.arange(N_META, dtype=jnp.int32)])
    inv = 1.0 / (ROPE_THETA ** (jnp.arange(0, QK_ROPE_DIM, 2, dtype=F32) / QK_ROPE_DIM))
    inv = jnp.concatenate([inv, inv, jnp.zeros((LANES - QK_ROPE_DIM,), F32)])[None, :]

    w_lat = _w_lat(w_in)
    w_rest = _w_rest(w_in)
    w_q = jnp.pad(w_q_up[0].reshape(Q_LORA_RANK, MLA_HEADS, QK_HEAD_DIM),
                  ((0, 0), (0, 0), (0, Q_CAT - QK_HEAD_DIM))).reshape(Q_LORA_RANK, MLA_HEADS * Q_CAT).astype(BF16)
    w_kv3 = w_kv_up[0].reshape(KV_LORA_RANK, MLA_HEADS, QK_NOPE_DIM + V_HEAD_DIM)
    w_kv = jnp.concatenate([w_kv3[:, :, :QK_NOPE_DIM].reshape(KV_LORA_RANK, -1),
                            w_kv3[:, :, QK_NOPE_DIM:].reshape(KV_LORA_RANK, -1)], axis=1).astype(BF16)
    w_o = w_out[0].astype(BF16)
    w_fo = w_ffn_out[0].astype(BF16)

    cos_t, sin_t = _rope_tables(pos.astype(F32)[:, None], inv)

    u = _norm_cast(h, g_mix_norm)
    qn, kvn, kr = _proj_lat(u, w_lat, g_q_norm, g_kv_norm, cos_t, sin_t)
    hq = _matmul_cols(u, w_rest, 0, d, BF16, 1024, "proj_hgrn_q")
    hf = _matmul_cols(u, w_rest, d, d, F32, 1024, "proj_hgrn_forget")
    hrest = _matmul_cols(u, w_rest, 2 * d, 4 * d, BF16, 1024, "proj_hgrn_rest")
    q_cat = _q_up(qn, w_q, cos_t, sin_t)
    kv = _kv_up(kvn, w_kv)
    o_mla = _attention(q_cat, kv, kr, s_real)
    o_hgrn = _hgrn(hq, hf, hrest, lb_raw, g_hgrn_norm)
    merged = _merge(o_mla, o_hgrn, w_branch_mla, w_branch_hgrn, hrest, 2 * d)
    h2, u2 = _out_proj(merged, w_o, h, g_ffn_norm)

    act = _ffn_in(u2, w_ffn_in, conv_w, conv_b, s_real)
    out = _ffn_out(act, w_fo, h2, g_final_norm[None, :])
    return out[None]
```

```python
import functools

import jax
import jax.numpy as jnp
from jax import lax
from jax.experimental import pallas as pl
from jax.experimental.pallas import tpu as pltpu

F32 = jnp.float32
BF16 = jnp.bfloat16

N_META = 16
BLOCK = 128
PAD_LEN = BLOCK - N_META
MLA_HEADS = 16
Q_LORA_RANK = 1536
KV_LORA_RANK = 512
QK_NOPE_DIM = 128
QK_ROPE_DIM = 64
QK_HEAD_DIM = QK_NOPE_DIM + QK_ROPE_DIM
V_HEAD_DIM = 128
ROPE_THETA = 10000.0
HGRN_HEADS = 16
HGRN_EXPAND = 128
HGRN_V_DIM = 128
D_FF = 5632
NORM_EPS = 1e-6

LANES = 128
Q_CAT = 2 * LANES
LAT_COLS = Q_LORA_RANK + KV_LORA_RANK + LANES
LOG2E = 1.4426950408889634
VMEM_LIMIT = 52 * 1024 * 1024


def _cparams(n_axes):
    return pltpu.CompilerParams(dimension_semantics=("arbitrary",) * n_axes, vmem_limit_bytes=VMEM_LIMIT)


def _row_tile(rows, target):
    for t in range(target, 0, -LANES):
        if rows % t == 0:
            return t
    raise ValueError(f"no 128-multiple row tile for {rows}")


def _dot(a, b):
    return jnp.dot(a, b, preferred_element_type=F32)


def _dot_nt(a, b):
    return lax.dot_general(a, b, (((1,), (1,)), ((), ())), preferred_element_type=F32)


def _dot_tn(a, b):
    return lax.dot_general(a, b, (((0,), (0,)), ((), ())), preferred_element_type=F32)


def _rms(x, g):
    return x * lax.rsqrt(jnp.mean(x * x, axis=-1, keepdims=True) + NORM_EPS) * g


def _sigmoid(x):
    return 1.0 / (1.0 + jnp.exp(-x))


def _silu(x):
    return x * _sigmoid(x)


def _rope(x, cos_t, sin_t):
    rot = pltpu.roll(x, 32, 1) - pltpu.roll(x, 96, 1)
    return x * cos_t + rot * sin_t


def _rope_table_kernel(pos_ref, inv_ref, cos_ref, sin_ref):
    ang = pos_ref[...] * inv_ref[...]
    keep = lax.broadcasted_iota(jnp.int32, ang.shape, 1) < QK_ROPE_DIM
    cos_ref[...] = jnp.where(keep, jnp.cos(ang), 0.0)
    sin_ref[...] = jnp.where(keep, jnp.sin(ang), 0.0)


def _rope_tables(pos_f, inv):
    rows = pos_f.shape[0]
    tm = _row_tile(rows, 640)
    spec = pl.BlockSpec((tm, LANES), lambda i: (i, 0))
    return pl.pallas_call(
        _rope_table_kernel, name="rope_tables",
        grid=(rows // tm,),
        in_specs=[pl.BlockSpec((tm, 1), lambda i: (i, 0)), pl.BlockSpec((1, LANES), lambda i: (0, 0))],
        out_specs=[spec, spec],
        out_shape=[jax.ShapeDtypeStruct((rows, LANES), F32)] * 2,
        compiler_params=_cparams(1),
    )(pos_f, inv)


LAT_END = Q_LORA_RANK + KV_LORA_RANK + QK_ROPE_DIM


def _w_lat_kernel(w_ref, o_ref):
    row = lax.broadcasted_iota(jnp.int32, w_ref.shape, 0)
    o_ref[...] = jnp.where(row < LAT_END, w_ref[...], 0.0).astype(o_ref.dtype)


def _w_lat(w_in_t):
    d = w_in_t.shape[1]
    tk = 512
    return pl.pallas_call(
        _w_lat_kernel, name="w_latent_cast",
        grid=(d // tk,),
        in_specs=[pl.BlockSpec((LAT_COLS, tk), lambda c: (0, c))],
        out_specs=pl.BlockSpec((LAT_COLS, tk), lambda c: (0, c)),
        out_shape=jax.ShapeDtypeStruct((LAT_COLS, d), BF16),
        compiler_params=_cparams(1),
    )(w_in_t)


def _norm_cast_kernel(x_ref, g_ref, o_ref):
    o_ref[...] = _rms(x_ref[...], g_ref[...]).astype(o_ref.dtype)


def _norm_cast(x, g):
    rows, d = x.shape
    tm = _row_tile(rows, 640)
    return pl.pallas_call(
        _norm_cast_kernel, name="mix_norm",
        grid=(rows // tm,),
        in_specs=[pl.BlockSpec((tm, d), lambda i: (i, 0)), pl.BlockSpec((1, d), lambda i: (0, 0))],
        out_specs=pl.BlockSpec((tm, d), lambda i: (i, 0)),
        out_shape=jax.ShapeDtypeStruct((rows, d), BF16),
        compiler_params=_cparams(1),
    )(x, g)


def _proj_lat_kernel(u_ref, w_ref, gq_ref, gkv_ref, cos_ref, sin_ref, qn_ref, kvn_ref, kr_ref):
    y = _dot_nt(u_ref[...], w_ref[...])
    kv_end = Q_LORA_RANK + KV_LORA_RANK
    qn_ref[...] = _rms(y[:, :Q_LORA_RANK], gq_ref[...]).astype(qn_ref.dtype)
    kvn_ref[...] = _rms(y[:, Q_LORA_RANK:kv_end], gkv_ref[...]).astype(kvn_ref.dtype)
    kr_ref[...] = _rope(y[:, kv_end:], cos_ref[...], sin_ref[...]).astype(kr_ref.dtype)


def _proj_lat(u, w_lat, g_q, g_kv, cos_t, sin_t):
    rows, d = u.shape
    tm = _row_tile(rows, 640)
    row_spec = lambda n: pl.BlockSpec((tm, n), lambda i: (i, 0))
    full_spec = lambda r, n: pl.BlockSpec((r, n), lambda i: (0, 0))
    return pl.pallas_call(
        _proj_lat_kernel, name="proj_latents",
        grid=(rows // tm,),
        in_specs=[row_spec(d), full_spec(LAT_COLS, d), full_spec(1, Q_LORA_RANK), full_spec(1, KV_LORA_RANK),
                  row_spec(LANES), row_spec(LANES)],
        out_specs=[row_spec(Q_LORA_RANK), row_spec(KV_LORA_RANK), row_spec(LANES)],
        out_shape=[jax.ShapeDtypeStruct((rows, Q_LORA_RANK), BF16),
                   jax.ShapeDtypeStruct((rows, KV_LORA_RANK), BF16),
                   jax.ShapeDtypeStruct((rows, LANES), BF16)],
        compiler_params=_cparams(1),
    )(u, w_lat, g_q, g_kv, cos_t, sin_t)


def _matmul_t_kernel(x_ref, w_ref, o_ref, w_bf):
    @pl.when(pl.program_id(1) == 0)
    def _():
        w_bf[...] = w_ref[...].astype(BF16)

    o_ref[...] = _dot_nt(x_ref[...], w_bf[...]).astype(o_ref.dtype)


def _matmul_cols_t(x, w_t, col_start, n_cols, out_dtype, tn, name):
    rows, k = x.shape
    tm = _row_tile(rows, 640)
    assert col_start % 8 == 0 and n_cols % tn == 0
    return pl.pallas_call(
        _matmul_t_kernel, name=name,
        grid=(n_cols // tn, rows // tm),
        in_specs=[pl.BlockSpec((tm, k), lambda n, m: (m, 0)),
                  pl.BlockSpec((pl.Element(tn), pl.Element(k)),
                               lambda n, m: (pl.multiple_of(col_start + n * tn, 8), 0))],
        out_specs=pl.BlockSpec((tm, tn), lambda n, m: (m, n)),
        out_shape=jax.ShapeDtypeStruct((rows, n_cols), out_dtype),
        scratch_shapes=[pltpu.VMEM((tn, k), BF16)],
        compiler_params=_cparams(2),
    )(x, w_t)


def _kv_up_kernel(x_ref, w_ref, o_ref):
    y = _dot(x_ref[...], w_ref[...])
    for hh in range(o_ref.shape[0]):
        o_ref[hh] = y[:, hh * LANES:(hh + 1) * LANES].astype(o_ref.dtype)


def _kv_up(kvn, w_kv):
    rows, k = kvn.shape
    n_slabs = w_kv.shape[1] // LANES
    tm = _row_tile(rows, 640)
    per_tile = 8
    return pl.pallas_call(
        _kv_up_kernel, name="kv_up",
        grid=(n_slabs // per_tile, rows // tm),
        in_specs=[pl.BlockSpec((tm, k), lambda n, m: (m, 0)),
                  pl.BlockSpec((k, per_tile * LANES), lambda n, m: (0, n))],
        out_specs=pl.BlockSpec((per_tile, tm, LANES), lambda n, m: (n, m, 0)),
        out_shape=jax.ShapeDtypeStruct((n_slabs, rows, LANES), BF16),
        compiler_params=_cparams(2),
    )(kvn, w_kv)


def _q_up_kernel(x_ref, w_ref, cos_ref, sin_ref, o_ref, *, heads_per_tile):
    y = _dot(x_ref[...], w_ref[...]) * (QK_HEAD_DIM ** -0.5 * LOG2E)
    cos_t = cos_ref[...]
    sin_t = sin_ref[...]
    for hh in range(heads_per_tile):
        lo = hh * Q_CAT
        o_ref[:, lo:lo + LANES] = y[:, lo:lo + LANES].astype(o_ref.dtype)
        o_ref[:, lo + LANES:lo + Q_CAT] = _rope(y[:, lo + LANES:lo + Q_CAT], cos_t, sin_t).astype(o_ref.dtype)


def _q_up(qn, w_q, cos_t, sin_t):
    rows, k = qn.shape
    n_cols = w_q.shape[1]
    tm = _row_tile(rows, 640)
    tn = 4 * Q_CAT
    return pl.pallas_call(
        functools.partial(_q_up_kernel, heads_per_tile=tn // Q_CAT), name="q_up_rope",
        grid=(n_cols // tn, rows // tm),
        in_specs=[pl.BlockSpec((tm, k), lambda n, m: (m, 0)), pl.BlockSpec((k, tn), lambda n, m: (0, n)),
                  pl.BlockSpec((tm, LANES), lambda n, m: (m, 0)), pl.BlockSpec((tm, LANES), lambda n, m: (m, 0))],
        out_specs=pl.BlockSpec((tm, tn), lambda n, m: (m, n)),
        out_shape=jax.ShapeDtypeStruct((rows, n_cols), BF16),
        compiler_params=_cparams(2),
    )(qn, w_q, cos_t, sin_t)


def _attn_kernel(q_ref, kn_ref, v_ref, kr_ref, o_ref, kcat_ref, m_ref, l_ref, acc_ref, sa_ref, sb_ref, *, tq,
                 n_real_tiles):
    step_id = pl.program_id(1)
    qi = step_id - 1
    s_real = n_real_tiles * tq

    @pl.when(step_id == 0)
    def _():
        kcat_ref[:, :LANES] = kn_ref[...]
        kcat_ref[:, LANES:] = kr_ref[...]

    th = tq // 2
    halves = (0, th)

    def init():
        m_ref[...] = jnp.full(m_ref.shape, -jnp.inf, F32)
        l_ref[...] = jnp.zeros(l_ref.shape, F32)
        acc_ref[...] = jnp.zeros(acc_ref.shape, F32)

    def update(r0, src_ref, n_keys, v_blk, visible=None):
        rows = slice(r0, r0 + th)

        def chunk(c):
            s = src_ref[rows, c * LANES:(c + 1) * LANES]
            return s if visible is None else jnp.where(visible(c), s, -jnp.inf)

        n_chunks = n_keys // LANES
        m_prev = m_ref[rows, :]
        m_cur = functools.reduce(jnp.maximum, [chunk(c) for c in range(n_chunks)])
        m_new = jnp.maximum(m_prev, jnp.max(m_cur, axis=1, keepdims=True))
        alpha = jnp.exp2(m_prev - m_new)
        p_chunks = [jnp.exp2(chunk(c) - m_new) for c in range(n_chunks)]
        l_cur = functools.reduce(jnp.add, p_chunks)
        l_ref[rows, :] = alpha * l_ref[rows, :] + jnp.sum(l_cur, axis=1, keepdims=True)
        p = jnp.concatenate(p_chunks, axis=1) if n_chunks > 1 else p_chunks[0]
        acc_ref[rows, :] = alpha * acc_ref[rows, :] + _dot(p.astype(v_blk.dtype), v_blk)
        m_ref[rows, :] = m_new

    def finalize():
        o_ref[...] = (acc_ref[...] / l_ref[...]).astype(o_ref.dtype)

    def prefix_block(visible_fn):
        for r0 in halves:
            sa_ref[r0:r0 + th, :BLOCK] = _dot_nt(q_ref[r0:r0 + th, :], kcat_ref[s_real:s_real + BLOCK, :])
        for r0 in halves:
            update(r0, sa_ref, BLOCK, v_ref[s_real:s_real + BLOCK, :], visible_fn(r0))

    @pl.when(step_id > 0)
    def _():
        init()
        col = lax.broadcasted_iota(jnp.int32, (th, BLOCK), 1)
        prefix_block(lambda r0: lambda c: col >= PAD_LEN)

        def raw_scores(dst_ref, j):
            k_blk = kcat_ref[pl.ds(pl.multiple_of(j * tq, tq), tq), :]
            for r0 in halves:
                dst_ref[r0:r0 + th, :] = _dot_nt(q_ref[r0:r0 + th, :], k_blk)

        def step(j, src_ref, dst_ref):
            raw_scores(dst_ref, j + 1)
            v_blk = v_ref[pl.ds(pl.multiple_of(j * tq, tq), tq), :]
            for r0 in halves:
                update(r0, src_ref, tq, v_blk)

        def diagonal(src_ref):
            v_blk = v_ref[pl.ds(pl.multiple_of(qi * tq, tq), tq), :]
            row = lax.broadcasted_iota(jnp.int32, (th, LANES), 0)
            lane = lax.broadcasted_iota(jnp.int32, (th, LANES), 1)
            for r0 in halves:
                update(r0, src_ref, tq, v_blk, lambda c, r0=r0: lane + c * LANES <= row + r0)
            finalize()

        raw_scores(sb_ref, 0)

        def body(i, carry):
            step(2 * i, sb_ref, sa_ref)
            step(2 * i + 1, sa_ref, sb_ref)
            return carry

        lax.fori_loop(0, qi // 2, body, 0)

        @pl.when(qi % 2 == 1)
        def _():
            step(qi - 1, sb_ref, sa_ref)
            diagonal(sa_ref)

        @pl.when(qi % 2 == 0)
        def _():
            diagonal(sb_ref)

    @pl.when(step_id == 0)
    def _():
        init()
        row = lax.broadcasted_iota(jnp.int32, (th, BLOCK), 0)
        col = lax.broadcasted_iota(jnp.int32, (th, BLOCK), 1)
        prefix_block(lambda r0: lambda c: (col <= row + r0) & ((col >= PAD_LEN) | (col == row + r0)))
        finalize()


def _attention(q_cat, kv, kr, s_real):
    rows = q_cat.shape[0]
    tq = 512
    n_real_tiles = s_real // tq
    q_tile = lambda h, i: ((i + n_real_tiles) % (n_real_tiles + 1), h)
    return pl.pallas_call(
        functools.partial(_attn_kernel, tq=tq, n_real_tiles=n_real_tiles), name="mla_attention",
        grid=(MLA_HEADS, n_real_tiles + 1),
        in_specs=[pl.BlockSpec((tq, Q_CAT), q_tile),
                  pl.BlockSpec((None, rows, LANES), lambda h, i: (h, 0, 0)),
                  pl.BlockSpec((None, rows, LANES), lambda h, i: (MLA_HEADS + h, 0, 0)),
                  pl.BlockSpec((rows, LANES), lambda h, i: (0, 0))],
        out_specs=pl.BlockSpec((tq, V_HEAD_DIM), q_tile),
        out_shape=jax.ShapeDtypeStruct((rows, MLA_HEADS * V_HEAD_DIM), BF16),
        scratch_shapes=[pltpu.VMEM((rows, Q_CAT), BF16), pltpu.VMEM((tq, LANES), F32),
                        pltpu.VMEM((tq, LANES), F32), pltpu.VMEM((tq, V_HEAD_DIM), F32),
                        pltpu.VMEM((tq, tq), F32), pltpu.VMEM((tq, tq), F32)],
        compiler_params=_cparams(2),
    )(q_cat, kv, kv, kr)


def _split3(x):
    x1 = x.astype(BF16)
    r1 = x - x1.astype(F32)
    x2 = r1.astype(BF16)
    x3 = (r1 - x2.astype(F32)).astype(BF16)
    return x1, x2, x3


def _block_row(x, size, j):
    c, n = x.shape
    g = x.reshape(c // size, size, n)[:, j:j + 1, :]
    return jnp.broadcast_to(g, (c // size, size, n)).reshape(c, n)


HGRN_LEVELS = 7


def _hgrn_head(hq, hf, hi, hg, lb_raw, gain, st, valid, row, lev):
    n = BLOCK
    top = jnp.max(lb_raw, axis=0, keepdims=True)
    e = jnp.exp(lb_raw - top)
    lb = e[0:1, :] / jnp.sum(e, axis=0, keepdims=True)

    f = lb + (1.0 - lb) * _sigmoid(hf)
    f_eff = jnp.where(valid, f, 1.0)
    g = jnp.where(valid, jnp.log(f), 0.0)
    k = jnp.where(valid, 1.0 - f, 0.0)
    q = _silu(hq.astype(F32))

    col = lax.broadcasted_iota(jnp.int32, (n, n), 1)
    tri = (col <= row).astype(BF16)
    g1, g2, g3 = _split3(g)
    b = _dot(tri, g1) + _dot(tri, g2) + _dot(tri, g3)

    a = jnp.where(lev == 0, jnp.sum(q * k, axis=1, keepdims=True), 0.0)
    for level in range(1, HGRN_LEVELS + 1):
        size = 1 << level
        if level == 1:
            w = jnp.where((row & 1) == 1, f_eff, 1.0)
        elif level == 2:
            r4 = row & 3
            up1 = pltpu.roll(g, n - 1, 0)
            dn1 = pltpu.roll(g, 1, 0)
            w = jnp.exp(jnp.where(r4 == 0, up1, jnp.where(r4 == 1, 0.0, jnp.where(r4 == 2, g, g + dn1))))
        else:
            w = jnp.exp(-jnp.abs(b - _block_row(b, size, size // 2 - 1)))
        a_l = _dot_nt((q * w).astype(BF16), (k * w).astype(BF16))
        a = jnp.where(lev == level, a_l, a)

    o = _dot(a.astype(BF16), hi) + _dot_nt((q * jnp.exp(b)).astype(BF16), st.astype(BF16))
    b_last = b[n - 1:n, :]
    st_new = st * jnp.exp(b_last) + _dot_tn(hi, (k * jnp.exp(b_last - b)).astype(BF16))
    out = _rms(o, gain) * _silu(hg.astype(F32))
    return out, st_new


def _hgrn_kernel(hq_ref, hf_ref, hi_ref, hg_ref, lb_ref, g_ref, o_ref, st_ref):
    c = pl.program_id(0)
    n = BLOCK

    @pl.when(c == 0)
    def _():
        st_ref[...] = jnp.zeros(st_ref.shape, F32)

    row = lax.broadcasted_iota(jnp.int32, (n, n), 0)
    col = lax.broadcasted_iota(jnp.int32, (n, n), 1)
    valid = (c > 0) | (row >= PAD_LEN)
    x = row ^ col
    lev = functools.reduce(jnp.add, [(x >= (1 << i)).astype(jnp.int32) for i in range(HGRN_LEVELS)])
    lev = jnp.where(col > row, -1, lev)
    gain = g_ref[...]

    def head_pair(i, carry):
        for hh in range(2):
            head = 2 * i + hh
            cols = pl.ds(pl.multiple_of(head * BLOCK, BLOCK), BLOCK)
            out, st_new = _hgrn_head(hq_ref[:, cols], hf_ref[:, cols], hi_ref[:, cols], hg_ref[:, cols],
                                     lb_ref[:, cols], gain, st_ref[head], valid, row, lev)
            st_ref[head] = st_new
            o_ref[:, cols] = out.astype(o_ref.dtype)
        return carry

    lax.fori_loop(0, HGRN_HEADS // 2, head_pair, 0)


def _hgrn(hq, hf, hrest, lb_raw, g_hgrn):
    rows, width = hq.shape
    n_chunks = rows // BLOCK
    blk = lambda off: pl.BlockSpec((BLOCK, width), lambda c: ((c + n_chunks - 1) % n_chunks, off))
    return pl.pallas_call(
        _hgrn_kernel, name="hgrn2",
        grid=(n_chunks,),
        in_specs=[blk(0), blk(0), blk(0), blk(1),
                  pl.BlockSpec((lb_raw.shape[0], width), lambda c: (0, 0)),
                  pl.BlockSpec((1, HGRN_V_DIM), lambda c: (0, 0))],
        out_specs=blk(0),
        out_shape=jax.ShapeDtypeStruct((rows, width), BF16),
        scratch_shapes=[pltpu.VMEM((HGRN_HEADS, HGRN_V_DIM, HGRN_EXPAND), F32)],
        compiler_params=_cparams(1),
    )(hq, hf, hrest, hrest, lb_raw, g_hgrn)


def _merge_kernel(om_ref, oh_ref, wa_ref, wb_ref, ga_ref, gb_ref, o_ref, wa_bf, wb_bf):
    @pl.when(pl.program_id(1) == 0)
    def _():
        wa_bf[...] = wa_ref[...].astype(BF16)
        wb_bf[...] = wb_ref[...].astype(BF16)

    a = _dot(om_ref[...], wa_bf[...])
    b = _dot(oh_ref[...], wb_bf[...])
    o = _sigmoid(ga_ref[...].astype(F32)) * a + _sigmoid(gb_ref[...].astype(F32)) * b
    o_ref[...] = o.astype(o_ref.dtype)


def _merge(o_mla, o_hgrn, w_a, w_b, hrest, gate_off):
    rows, k = o_mla.shape
    d = w_a.shape[2]
    tm = _row_tile(rows, 640)
    tn = 512
    ga_off = gate_off // tn
    gb_off = (gate_off + d) // tn
    w_spec = pl.BlockSpec((None, k, tn), lambda n, m: (0, 0, n))
    return pl.pallas_call(
        _merge_kernel, name="branch_merge",
        grid=(d // tn, rows // tm),
        in_specs=[pl.BlockSpec((tm, k), lambda n, m: (m, 0)), pl.BlockSpec((tm, k), lambda n, m: (m, 0)),
                  w_spec, w_spec,
                  pl.BlockSpec((tm, tn), lambda n, m: (m, n + ga_off)),
                  pl.BlockSpec((tm, tn), lambda n, m: (m, n + gb_off))],
        out_specs=pl.BlockSpec((tm, tn), lambda n, m: (m, n)),
        out_shape=jax.ShapeDtypeStruct((rows, d), BF16),
        scratch_shapes=[pltpu.VMEM((k, tn), BF16), pltpu.VMEM((k, tn), BF16)],
        compiler_params=_cparams(2),
    )(o_mla, o_hgrn, w_a, w_b, hrest, hrest)


def _out_proj_kernel(mg_ref, w_ref, h_ref, g_ref, h2_ref, u2_ref):
    h2 = h_ref[...] + _dot(mg_ref[...], w_ref[...])
    h2_ref[...] = h2
    u2_ref[...] = _rms(h2, g_ref[...]).astype(u2_ref.dtype)


def _out_proj(merged, w_out, h, g_ffn):
    rows, d = h.shape
    tm = _row_tile(rows, 320)
    row_spec = pl.BlockSpec((tm, d), lambda i: (i, 0))
    return pl.pallas_call(
        _out_proj_kernel, name="mix_out_proj",
        grid=(rows // tm,),
        in_specs=[row_spec, pl.BlockSpec((d, d), lambda i: (0, 0)), row_spec, pl.BlockSpec((1, d), lambda i: (0, 0))],
        out_specs=[row_spec, row_spec],
        out_shape=[jax.ShapeDtypeStruct((rows, d), F32), jax.ShapeDtypeStruct((rows, d), BF16)],
        compiler_params=_cparams(1),
    )(merged, w_out, h, g_ffn)


HALO = 16


def _ffn_in_kernel(u_ref, halo_ref, wg_ref, wu_ref, cw_ref, cb_ref, o_ref, wg_bf, wu_bf, *, n_parts):
    @pl.when(pl.program_id(1) == 0)
    def _():
        wg_bf[...] = wg_ref[...].astype(BF16)
        wu_bf[...] = wu_ref[...].astype(BF16)

    wg = wg_bf[...]
    wu = wu_bf[...]
    cw = cw_ref[...]
    cb = cb_ref[...]
    tp = o_ref.shape[0] // n_parts
    row = lax.broadcasted_iota(jnp.int32, (tp, o_ref.shape[1]), 0)
    tail = _dot(halo_ref[...], wg)[HALO - 2:HALO, :]
    for part in range(n_parts):
        u = u_ref[part * tp:(part + 1) * tp, :]
        gate = _dot(u, wg)
        up = _dot(u, wu)
        back1 = jnp.where(row == 0, tail[1:2, :], pltpu.roll(gate, 1, 0))
        back2 = jnp.where(row == 0, tail[0:1, :], jnp.where(row == 1, tail[1:2, :], pltpu.roll(gate, 2, 0)))
        conv = cw[0:1, :] * back2 + cw[1:2, :] * back1 + cw[2:3, :] * gate + cb
        o_ref[part * tp:(part + 1) * tp, :] = (_silu(conv) * up).astype(o_ref.dtype)
        tail = gate[tp - 2:tp, :]


def _ffn_in(u2, w_fi, conv_w, conv_b, s_real):
    rows, d = u2.shape
    tm = 1024
    tn = 512
    up_off = D_FF // tn
    n_halo = rows // HALO
    halo_map = lambda n, m: ((m * (tm // HALO) + n_halo - 1) % n_halo, 0)
    return pl.pallas_call(
        functools.partial(_ffn_in_kernel, n_parts=2), name="ffn_in_conv_gate",
        grid=(D_FF // tn, s_real // tm),
        in_specs=[pl.BlockSpec((tm, d), lambda n, m: (m, 0)), pl.BlockSpec((HALO, d), halo_map),
                  pl.BlockSpec((None, d, tn), lambda n, m: (0, 0, n)),
                  pl.BlockSpec((None, d, tn), lambda n, m: (0, 0, n + up_off)),
                  pl.BlockSpec((None, conv_w.shape[1], tn), lambda n, m: (0, 0, n)),
                  pl.BlockSpec((1, tn), lambda n, m: (0, n))],
        out_specs=pl.BlockSpec((tm, tn), lambda n, m: (m, n)),
        out_shape=jax.ShapeDtypeStruct((s_real, D_FF), BF16),
        scratch_shapes=[pltpu.VMEM((d, tn), BF16), pltpu.VMEM((d, tn), BF16)],
        compiler_params=_cparams(2),
    )(u2, u2, w_fi, w_fi, conv_w, conv_b)


def _ffn_out_kernel(a_ref, w_ref, h2_ref, g_ref, o_ref):
    kk = pl.program_id(1)

    @pl.when(kk == 0)
    def _():
        o_ref[...] = h2_ref[...]

    o_ref[...] += _dot(a_ref[...], w_ref[...])

    @pl.when(kk == pl.num_programs(1) - 1)
    def _():
        o_ref[...] = _rms(o_ref[...], g_ref[...])


def _ffn_out(act, w_fo, h2, g_final):
    s_real, k = act.shape
    d = w_fo.shape[1]
    tm = 512
    tk = k // 4
    assert tk % LANES == 0
    return pl.pallas_call(
        _ffn_out_kernel, name="ffn_out_final_norm",
        grid=(s_real // tm, k // tk),
        in_specs=[pl.BlockSpec((tm, tk), lambda m, kk: (m, kk)), pl.BlockSpec((tk, d), lambda m, kk: (kk, 0)),
                  pl.BlockSpec((tm, d), lambda m, kk: (m, 0)), pl.BlockSpec((1, d), lambda m, kk: (0, 0))],
        out_specs=pl.BlockSpec((tm, d), lambda m, kk: (m, 0)),
        out_shape=jax.ShapeDtypeStruct((s_real, d), F32),
        compiler_params=_cparams(2),
    )(act, w_fo, h2, g_final)


def kernel(x, positions, meta_tokens, w_in, w_q_up, w_kv_up, w_branch_mla, w_branch_hgrn, w_out, w_ffn_in,
           w_ffn_out, conv_w, conv_b, g_mix_norm, g_q_norm, g_kv_norm, g_hgrn_norm, g_ffn_norm, g_final_norm,
           lb_raw):
    b, s_real, d = x.shape
    assert b == 1 and w_in.shape[0] == 1 and s_real % 512 == 0
    dt = x.dtype

    h = jnp.concatenate([x[0], jnp.zeros((PAD_LEN, d), dt), meta_tokens.astype(dt)], axis=0)
    pos = jnp.concatenate([positions[0].astype(jnp.int32) + N_META, jnp.zeros((PAD_LEN,), jnp.int32),
                           jnp.arange(N_META, dtype=jnp.int32)])
    inv = 1.0 / (ROPE_THETA ** (jnp.arange(0, QK_ROPE_DIM, 2, dtype=F32) / QK_ROPE_DIM))
    inv = jnp.concatenate([inv, inv, jnp.zeros((LANES - QK_ROPE_DIM,), F32)])[None, :]

    w_in_t = jnp.swapaxes(w_in, 1, 2)[0]
    w_lat = _w_lat(w_in_t)
    w_q = jnp.pad(w_q_up[0].reshape(Q_LORA_RANK, MLA_HEADS, QK_HEAD_DIM),
                  ((0, 0), (0, 0), (0, Q_CAT - QK_HEAD_DIM))).reshape(Q_LORA_RANK, MLA_HEADS * Q_CAT).astype(BF16)
    w_kv3 = w_kv_up[0].reshape(KV_LORA_RANK, MLA_HEADS, QK_NOPE_DIM + V_HEAD_DIM)
    w_kv = jnp.concatenate([w_kv3[:, :, :QK_NOPE_DIM].reshape(KV_LORA_RANK, -1),
                            w_kv3[:, :, QK_NOPE_DIM:].reshape(KV_LORA_RANK, -1)], axis=1).astype(BF16)
    w_o = w_out[0].astype(BF16)
    w_fo = w_ffn_out[0].astype(BF16)

    cos_t, sin_t = _rope_tables(pos.astype(F32)[:, None], inv)

    u = _norm_cast(h, g_mix_norm)
    qn, kvn, kr = _proj_lat(u, w_lat, g_q_norm, g_kv_norm, cos_t, sin_t)
    hq = _matmul_cols_t(u, w_in_t, LAT_END, d, BF16, 1024, "proj_hgrn_q")
    hf = _matmul_cols_t(u, w_in_t, LAT_END + d, d, F32, 1024, "proj_hgrn_forget")
    hrest = _matmul_cols_t(u, w_in_t, LAT_END + 2 * d, 4 * d, BF16, 1024, "proj_hgrn_rest")
    q_cat = _q_up(qn, w_q, cos_t, sin_t)
    kv = _kv_up(kvn, w_kv)
    o_mla = _attention(q_cat, kv, kr, s_real)
    o_hgrn = _hgrn(hq, hf, hrest, lb_raw, g_hgrn_norm)
    merged = _merge(o_mla, o_hgrn, w_branch_mla, w_branch_hgrn, hrest, 2 * d)
    h2, u2 = _out_proj(merged, w_o, h, g_ffn_norm)

    act = _ffn_in(u2, w_ffn_in, conv_w, conv_b, s_real)
    out = _ffn_out(act, w_fo, h2, g_final_norm[None, :])
    return out[None]
```

```python
import functools

import jax
import jax.numpy as jnp
from jax import lax
from jax.experimental import pallas as pl
from jax.experimental.pallas import tpu as pltpu

F32 = jnp.float32
BF16 = jnp.bfloat16

N_META = 16
BLOCK = 128
PAD_LEN = BLOCK - N_META
MLA_HEADS = 16
Q_LORA_RANK = 1536
KV_LORA_RANK = 512
QK_NOPE_DIM = 128
QK_ROPE_DIM = 64
QK_HEAD_DIM = QK_NOPE_DIM + QK_ROPE_DIM
V_HEAD_DIM = 128
ROPE_THETA = 10000.0
HGRN_HEADS = 16
HGRN_EXPAND = 128
HGRN_V_DIM = 128
D_FF = 5632
NORM_EPS = 1e-6

LANES = 128
Q_CAT = 2 * LANES
LAT_COLS = Q_LORA_RANK + KV_LORA_RANK + LANES
LOG2E = 1.4426950408889634
VMEM_LIMIT = 52 * 1024 * 1024


def _cparams(n_axes):
    return pltpu.CompilerParams(dimension_semantics=("arbitrary",) * n_axes, vmem_limit_bytes=VMEM_LIMIT)


def _row_tile(rows, target):
    for t in range(target, 0, -LANES):
        if rows % t == 0:
            return t
    raise ValueError(f"no 128-multiple row tile for {rows}")


def _dot(a, b):
    return jnp.dot(a, b, preferred_element_type=F32)


def _dot_nt(a, b):
    return lax.dot_general(a, b, (((1,), (1,)), ((), ())), preferred_element_type=F32)


def _dot_tn(a, b):
    return lax.dot_general(a, b, (((0,), (0,)), ((), ())), preferred_element_type=F32)


def _rms(x, g):
    return x * lax.rsqrt(jnp.mean(x * x, axis=-1, keepdims=True) + NORM_EPS) * g


def _sigmoid(x):
    return 1.0 / (1.0 + jnp.exp(-x))


def _silu(x):
    return x * _sigmoid(x)


def _rope(x, cos_t, sin_t):
    rot = pltpu.roll(x, 32, 1) - pltpu.roll(x, 96, 1)
    return x * cos_t + rot * sin_t


def _rope_table_kernel(pos_ref, inv_ref, cos_ref, sin_ref):
    ang = pos_ref[...] * inv_ref[...]
    keep = lax.broadcasted_iota(jnp.int32, ang.shape, 1) < QK_ROPE_DIM
    cos_ref[...] = jnp.where(keep, jnp.cos(ang), 0.0)
    sin_ref[...] = jnp.where(keep, jnp.sin(ang), 0.0)


def _rope_tables(pos_f, inv):
    rows = pos_f.shape[0]
    tm = _row_tile(rows, 640)
    spec = pl.BlockSpec((tm, LANES), lambda i: (i, 0))
    return pl.pallas_call(
        _rope_table_kernel, name="rope_tables",
        grid=(rows // tm,),
        in_specs=[pl.BlockSpec((tm, 1), lambda i: (i, 0)), pl.BlockSpec((1, LANES), lambda i: (0, 0))],
        out_specs=[spec, spec],
        out_shape=[jax.ShapeDtypeStruct((rows, LANES), F32)] * 2,
        compiler_params=_cparams(1),
    )(pos_f, inv)


LAT_END = Q_LORA_RANK + KV_LORA_RANK + QK_ROPE_DIM


def _w_lat_kernel(w_ref, o_ref):
    row = lax.broadcasted_iota(jnp.int32, w_ref.shape, 0)
    o_ref[...] = jnp.where(row < LAT_END, w_ref[...], 0.0).astype(o_ref.dtype)


def _w_lat(w_in_t):
    d = w_in_t.shape[1]
    tk = 512
    return pl.pallas_call(
        _w_lat_kernel, name="w_latent_cast",
        grid=(d // tk,),
        in_specs=[pl.BlockSpec((LAT_COLS, tk), lambda c: (0, c))],
        out_specs=pl.BlockSpec((LAT_COLS, tk), lambda c: (0, c)),
        out_shape=jax.ShapeDtypeStruct((LAT_COLS, d), BF16),
        compiler_params=_cparams(1),
    )(w_in_t)


def _norm_cast_kernel(x_ref, g_ref, o_ref):
    o_ref[...] = _rms(x_ref[...], g_ref[...]).astype(o_ref.dtype)


def _norm_cast(x, g):
    rows, d = x.shape
    tm = _row_tile(rows, 640)
    return pl.pallas_call(
        _norm_cast_kernel, name="mix_norm",
        grid=(rows // tm,),
        in_specs=[pl.BlockSpec((tm, d), lambda i: (i, 0)), pl.BlockSpec((1, d), lambda i: (0, 0))],
        out_specs=pl.BlockSpec((tm, d), lambda i: (i, 0)),
        out_shape=jax.ShapeDtypeStruct((rows, d), BF16),
        compiler_params=_cparams(1),
    )(x, g)


def _proj_lat_kernel(u_ref, w_ref, gq_ref, gkv_ref, cos_ref, sin_ref, qn_ref, kvn_ref, kr_ref):
    y = _dot_nt(u_ref[...], w_ref[...])
    kv_end = Q_LORA_RANK + KV_LORA_RANK
    qn_ref[...] = _rms(y[:, :Q_LORA_RANK], gq_ref[...]).astype(qn_ref.dtype)
    kvn_ref[...] = _rms(y[:, Q_LORA_RANK:kv_end], gkv_ref[...]).astype(kvn_ref.dtype)
    kr_ref[...] = _rope(y[:, kv_end:], cos_ref[...], sin_ref[...]).astype(kr_ref.dtype)


def _proj_lat(u, w_lat, g_q, g_kv, cos_t, sin_t):
    rows, d = u.shape
    tm = _row_tile(rows, 640)
    row_spec = lambda n: pl.BlockSpec((tm, n), lambda i: (i, 0))
    full_spec = lambda r, n: pl.BlockSpec((r, n), lambda i: (0, 0))
    return pl.pallas_call(
        _proj_lat_kernel, name="proj_latents",
        grid=(rows // tm,),
        in_specs=[row_spec(d), full_spec(LAT_COLS, d), full_spec(1, Q_LORA_RANK), full_spec(1, KV_LORA_RANK),
                  row_spec(LANES), row_spec(LANES)],
        out_specs=[row_spec(Q_LORA_RANK), row_spec(KV_LORA_RANK), row_spec(LANES)],
        out_shape=[jax.ShapeDtypeStruct((rows, Q_LORA_RANK), BF16),
                   jax.ShapeDtypeStruct((rows, KV_LORA_RANK), BF16),
                   jax.ShapeDtypeStruct((rows, LANES), BF16)],
        compiler_params=_cparams(1),
    )(u, w_lat, g_q, g_kv, cos_t, sin_t)


def _matmul_t_kernel(x_ref, w_ref, o_ref, w_bf):
    @pl.when(pl.program_id(1) == 0)
    def _():
        w_bf[...] = w_ref[...].astype(BF16)

    o_ref[...] = _dot_nt(x_ref[...], w_bf[...]).astype(o_ref.dtype)


def _matmul_cols_t(x, w_t, col_start, n_cols, out_dtype, tn, name):
    rows, k = x.shape
    tm = _row_tile(rows, 640)
    assert col_start % 8 == 0 and n_cols % tn == 0
    return pl.pallas_call(
        _matmul_t_kernel, name=name,
        grid=(n_cols // tn, rows // tm),
        in_specs=[pl.BlockSpec((tm, k), lambda n, m: (m, 0)),
                  pl.BlockSpec((pl.Element(tn), pl.Element(k)),
                               lambda n, m: (pl.multiple_of(col_start + n * tn, 8), 0))],
        out_specs=pl.BlockSpec((tm, tn), lambda n, m: (m, n)),
        out_shape=jax.ShapeDtypeStruct((rows, n_cols), out_dtype),
        scratch_shapes=[pltpu.VMEM((tn, k), BF16)],
        compiler_params=_cparams(2),
    )(x, w_t)


def _kv_up_kernel(x_ref, w_ref, o_ref):
    y = _dot(x_ref[...], w_ref[...])
    for hh in range(o_ref.shape[0]):
        o_ref[hh] = y[:, hh * LANES:(hh + 1) * LANES].astype(o_ref.dtype)


def _v_up_t_kernel(x_ref, w_ref, o_ref):
    y_t = _dot_nt(w_ref[...], x_ref[...])
    for hh in range(o_ref.shape[0]):
        o_ref[hh] = y_t[hh * LANES:(hh + 1) * LANES, :].astype(o_ref.dtype)


def _v_up_t(kvn, w_vt):
    rows, k = kvn.shape
    n_slabs = w_vt.shape[0] // LANES
    tm = _row_tile(rows, 640)
    per_tile = 8
    return pl.pallas_call(
        _v_up_t_kernel, name="v_up_transposed",
        grid=(n_slabs // per_tile, rows // tm),
        in_specs=[pl.BlockSpec((tm, k), lambda n, m: (m, 0)),
                  pl.BlockSpec((per_tile * LANES, k), lambda n, m: (n, 0))],
        out_specs=pl.BlockSpec((per_tile, LANES, tm), lambda n, m: (n, 0, m)),
        out_shape=jax.ShapeDtypeStruct((n_slabs, LANES, rows), BF16),
        compiler_params=_cparams(2),
    )(kvn, w_vt)


def _kv_up(kvn, w_kv):
    rows, k = kvn.shape
    n_slabs = w_kv.shape[1] // LANES
    tm = _row_tile(rows, 640)
    per_tile = 8
    return pl.pallas_call(
        _kv_up_kernel, name="kv_up",
        grid=(n_slabs // per_tile, rows // tm),
        in_specs=[pl.BlockSpec((tm, k), lambda n, m: (m, 0)),
                  pl.BlockSpec((k, per_tile * LANES), lambda n, m: (0, n))],
        out_specs=pl.BlockSpec((per_tile, tm, LANES), lambda n, m: (n, m, 0)),
        out_shape=jax.ShapeDtypeStruct((n_slabs, rows, LANES), BF16),
        compiler_params=_cparams(2),
    )(kvn, w_kv)


def _q_up_kernel(x_ref, w_ref, cos_ref, sin_ref, o_ref, *, heads_per_tile):
    y = _dot(x_ref[...], w_ref[...]) * (QK_HEAD_DIM ** -0.5 * LOG2E)
    cos_t = cos_ref[...]
    sin_t = sin_ref[...]
    for hh in range(heads_per_tile):
        lo = hh * Q_CAT
        o_ref[:, lo:lo + LANES] = y[:, lo:lo + LANES].astype(o_ref.dtype)
        o_ref[:, lo + LANES:lo + Q_CAT] = _rope(y[:, lo + LANES:lo + Q_CAT], cos_t, sin_t).astype(o_ref.dtype)


def _q_up(qn, w_q, cos_t, sin_t):
    rows, k = qn.shape
    n_cols = w_q.shape[1]
    tm = _row_tile(rows, 640)
    tn = 4 * Q_CAT
    return pl.pallas_call(
        functools.partial(_q_up_kernel, heads_per_tile=tn // Q_CAT), name="q_up_rope",
        grid=(n_cols // tn, rows // tm),
        in_specs=[pl.BlockSpec((tm, k), lambda n, m: (m, 0)), pl.BlockSpec((k, tn), lambda n, m: (0, n)),
                  pl.BlockSpec((tm, LANES), lambda n, m: (m, 0)), pl.BlockSpec((tm, LANES), lambda n, m: (m, 0))],
        out_specs=pl.BlockSpec((tm, tn), lambda n, m: (m, n)),
        out_shape=jax.ShapeDtypeStruct((rows, n_cols), BF16),
        compiler_params=_cparams(2),
    )(qn, w_q, cos_t, sin_t)


ATTN_KEY_BLOCK = 64


def _attn_kernel(q_ref, kn_ref, vt_ref, kr_ref, o_ref, kcat_ref, m_ref, l_ref, acc_ref, sa_ref, sb_ref, spre_ref,
                 p_ref, *, tq, n_real_tiles):
    step_id = pl.program_id(1)
    qi = step_id - 1
    s_real = n_real_tiles * tq
    tk = tq

    @pl.when(step_id == 0)
    def _():
        kcat_ref[:, :LANES] = kn_ref[...]
        kcat_ref[:, LANES:] = kr_ref[...]

    th = tq // 2
    halves = (0, th)

    def init():
        m_ref[...] = jnp.full(m_ref.shape, -jnp.inf, F32)
        l_ref[...] = jnp.zeros(l_ref.shape, F32)
        acc_ref[...] = jnp.zeros(acc_ref.shape, F32)

    def update(c0, blocks, pv):
        cols = slice(c0, c0 + th)

        def load(ref, row0, n, visible):
            s = ref[row0:row0 + n, cols]
            return s if visible is None else jnp.where(visible(row0, n), s, -jnp.inf)

        m_prev = m_ref[:, cols]
        m_cur = functools.reduce(
            jnp.maximum, [jnp.max(load(ref, r, n, vis), axis=0, keepdims=True) for ref, r, n, _, vis in blocks])
        m_new = jnp.maximum(m_prev, m_cur)
        alpha = jnp.exp2(m_prev - m_new)
        m_ref[:, cols] = m_new
        total = jnp.zeros((8, th), F32)
        for ref, r, n, p_row, vis in blocks:
            for b0 in range(0, n, ATTN_KEY_BLOCK):
                p = jnp.exp2(load(ref, r + b0, ATTN_KEY_BLOCK, vis) - m_new)
                total = total + jnp.sum(p.reshape(ATTN_KEY_BLOCK // 8, 8, th), axis=0)
                p_ref[p_row + b0:p_row + b0 + ATTN_KEY_BLOCK, cols] = p.astype(p_ref.dtype)
        l_ref[:, cols] = alpha * l_ref[:, cols] + jnp.sum(total, axis=0, keepdims=True)
        new = functools.reduce(jnp.add, [_dot(v_t, p_ref[p_row:p_row + n, cols]) for p_row, n, v_t in pv])
        acc_ref[:, cols] = alpha * acc_ref[:, cols] + new

    def finalize():
        o_ref[...] = (acc_ref[...] / l_ref[...]).T.astype(o_ref.dtype)

    def prefix_scores():
        k_pre = kcat_ref[s_real:s_real + BLOCK, :]
        for c0 in halves:
            spre_ref[:, c0:c0 + th] = _dot_nt(k_pre, q_ref[c0:c0 + th, :])

    v_pre_t = vt_ref[:, s_real:s_real + BLOCK]
    pre_row = tk

    def key_ids(row0, n):
        return lax.broadcasted_iota(jnp.int32, (n, th), 0) + row0

    def query_ids(c0, n):
        return lax.broadcasted_iota(jnp.int32, (n, th), 1) + c0

    @pl.when(step_id > 0)
    def _():
        init()
        prefix_scores()

        def raw_scores(dst_ref, j):
            k_blk = kcat_ref[pl.ds(pl.multiple_of(j * tk, tk), tk), :]
            for c0 in halves:
                dst_ref[:, c0:c0 + th] = _dot_nt(k_blk, q_ref[c0:c0 + th, :])

        def step(j, src_ref, dst_ref):
            raw_scores(dst_ref, j + 1)
            v_t = vt_ref[:, pl.ds(pl.multiple_of(j * tk, tk), tk)]
            for c0 in halves:
                update(c0, [(src_ref, 0, tk, 0, None)], [(0, tk, v_t)])

        def diagonal(src_ref):
            start = pl.multiple_of(qi * tk, tk)
            for c0 in halves:
                n_k = c0 + th
                causal = lambda row0, n, c0=c0: key_ids(row0, n) <= query_ids(c0, n)
                no_pads = lambda row0, n: key_ids(row0, n) >= PAD_LEN
                update(c0, [(src_ref, 0, n_k, 0, causal), (spre_ref, 0, BLOCK, pre_row, no_pads)],
                       [(0, n_k, vt_ref[:, pl.ds(start, n_k)]), (pre_row, BLOCK, v_pre_t)])
            finalize()

        raw_scores(sb_ref, 0)

        def body(i, carry):
            step(2 * i, sb_ref, sa_ref)
            step(2 * i + 1, sa_ref, sb_ref)
            return carry

        lax.fori_loop(0, qi // 2, body, 0)

        @pl.when(qi % 2 == 1)
        def _():
            step(qi - 1, sb_ref, sa_ref)
            diagonal(sa_ref)

        @pl.when(qi % 2 == 0)
        def _():
            diagonal(sb_ref)

    @pl.when(step_id == 0)
    def _():
        init()
        prefix_scores()
        for c0 in halves:
            def visible(row0, n, c0=c0):
                key, query = key_ids(row0, n), query_ids(c0, n)
                return (key <= query) & ((key >= PAD_LEN) | (key == query))

            update(c0, [(spre_ref, 0, BLOCK, pre_row, visible)], [(pre_row, BLOCK, v_pre_t)])
        finalize()


def _attention(q_cat, k_slabs, v_t, kr, s_real):
    rows = q_cat.shape[0]
    tq = 512
    n_real_tiles = s_real // tq
    q_tile = lambda h, i: ((i + n_real_tiles) % (n_real_tiles + 1), h)
    return pl.pallas_call(
        functools.partial(_attn_kernel, tq=tq, n_real_tiles=n_real_tiles), name="mla_attention",
        grid=(MLA_HEADS, n_real_tiles + 1),
        in_specs=[pl.BlockSpec((tq, Q_CAT), q_tile),
                  pl.BlockSpec((None, rows, LANES), lambda h, i: (h, 0, 0)),
                  pl.BlockSpec((None, V_HEAD_DIM, rows), lambda h, i: (h, 0, 0)),
                  pl.BlockSpec((rows, LANES), lambda h, i: (0, 0))],
        out_specs=pl.BlockSpec((tq, V_HEAD_DIM), q_tile),
        out_shape=jax.ShapeDtypeStruct((rows, MLA_HEADS * V_HEAD_DIM), BF16),
        scratch_shapes=[pltpu.VMEM((rows, Q_CAT), BF16), pltpu.VMEM((1, tq), F32), pltpu.VMEM((1, tq), F32),
                        pltpu.VMEM((V_HEAD_DIM, tq), F32), pltpu.VMEM((tq, tq), F32), pltpu.VMEM((tq, tq), F32),
                        pltpu.VMEM((BLOCK, tq), F32), pltpu.VMEM((tq + BLOCK, tq), BF16)],
        compiler_params=_cparams(2),
    )(q_cat, k_slabs, v_t, kr)


def _split3(x):
    x1 = x.astype(BF16)
    r1 = x - x1.astype(F32)
    x2 = r1.astype(BF16)
    x3 = (r1 - x2.astype(F32)).astype(BF16)
    return x1, x2, x3


def _block_row(x, size, j):
    c, n = x.shape
    g = x.reshape(c // size, size, n)[:, j:j + 1, :]
    return jnp.broadcast_to(g, (c // size, size, n)).reshape(c, n)


HGRN_LEVELS = 7


def _hgrn_head(hq, hf, hi, hg, lb_raw, gain, st, valid, row, lev):
    n = BLOCK
    top = jnp.max(lb_raw, axis=0, keepdims=True)
    e = jnp.exp(lb_raw - top)
    lb = e[0:1, :] / jnp.sum(e, axis=0, keepdims=True)

    f = lb + (1.0 - lb) * _sigmoid(hf)
    f_eff = jnp.where(valid, f, 1.0)
    g = jnp.where(valid, jnp.log(f), 0.0)
    k = jnp.where(valid, 1.0 - f, 0.0)
    q = _silu(hq.astype(F32))

    col = lax.broadcasted_iota(jnp.int32, (n, n), 1)
    tri = (col <= row).astype(BF16)
    g1, g2, g3 = _split3(g)
    b = _dot(tri, g1) + _dot(tri, g2) + _dot(tri, g3)

    a = jnp.where(lev == 0, jnp.sum(q * k, axis=1, keepdims=True), 0.0)
    for level in range(1, HGRN_LEVELS + 1):
        size = 1 << level
        if level == 1:
            w = jnp.where((row & 1) == 1, f_eff, 1.0)
        elif level == 2:
            r4 = row & 3
            up1 = pltpu.roll(g, n - 1, 0)
            dn1 = pltpu.roll(g, 1, 0)
            w = jnp.exp(jnp.where(r4 == 0, up1, jnp.where(r4 == 1, 0.0, jnp.where(r4 == 2, g, g + dn1))))
        else:
            w = jnp.exp(-jnp.abs(b - _block_row(b, size, size // 2 - 1)))
        a_l = _dot_nt((q * w).astype(BF16), (k * w).astype(BF16))
        a = jnp.where(lev == level, a_l, a)

    o = _dot(a.astype(BF16), hi) + _dot_nt((q * jnp.exp(b)).astype(BF16), st.astype(BF16))
    b_last = b[n - 1:n, :]
    st_new = st * jnp.exp(b_last) + _dot_tn(hi, (k * jnp.exp(b_last - b)).astype(BF16))
    out = _rms(o, gain) * _silu(hg.astype(F32))
    return out, st_new


def _hgrn_kernel(hq_ref, hf_ref, hi_ref, hg_ref, lb_ref, g_ref, o_ref, st_ref):
    c = pl.program_id(0)
    n = BLOCK

    @pl.when(c == 0)
    def _():
        st_ref[...] = jnp.zeros(st_ref.shape, F32)

    row = lax.broadcasted_iota(jnp.int32, (n, n), 0)
    col = lax.broadcasted_iota(jnp.int32, (n, n), 1)
    valid = (c > 0) | (row >= PAD_LEN)
    x = row ^ col
    lev = functools.reduce(jnp.add, [(x >= (1 << i)).astype(jnp.int32) for i in range(HGRN_LEVELS)])
    lev = jnp.where(col > row, -1, lev)
    gain = g_ref[...]

    def head_pair(i, carry):
        for hh in range(2):
            head = 2 * i + hh
            cols = pl.ds(pl.multiple_of(head * BLOCK, BLOCK), BLOCK)
            out, st_new = _hgrn_head(hq_ref[:, cols], hf_ref[:, cols], hi_ref[:, cols], hg_ref[:, cols],
                                     lb_ref[:, cols], gain, st_ref[head], valid, row, lev)
            st_ref[head] = st_new
            o_ref[:, cols] = out.astype(o_ref.dtype)
        return carry

    lax.fori_loop(0, HGRN_HEADS // 2, head_pair, 0)


def _hgrn(hq, hf, hrest, lb_raw, g_hgrn):
    rows, width = hq.shape
    n_chunks = rows // BLOCK
    blk = lambda off: pl.BlockSpec((BLOCK, width), lambda c: ((c + n_chunks - 1) % n_chunks, off))
    return pl.pallas_call(
        _hgrn_kernel, name="hgrn2",
        grid=(n_chunks,),
        in_specs=[blk(0), blk(0), blk(0), blk(1),
                  pl.BlockSpec((lb_raw.shape[0], width), lambda c: (0, 0)),
                  pl.BlockSpec((1, HGRN_V_DIM), lambda c: (0, 0))],
        out_specs=blk(0),
        out_shape=jax.ShapeDtypeStruct((rows, width), BF16),
        scratch_shapes=[pltpu.VMEM((HGRN_HEADS, HGRN_V_DIM, HGRN_EXPAND), F32)],
        compiler_params=_cparams(1),
    )(hq, hf, hrest, hrest, lb_raw, g_hgrn)


def _merge_kernel(om_ref, oh_ref, wa_ref, wb_ref, ga_ref, gb_ref, o_ref, wa_bf, wb_bf):
    @pl.when(pl.program_id(1) == 0)
    def _():
        wa_bf[...] = wa_ref[...].astype(BF16)
        wb_bf[...] = wb_ref[...].astype(BF16)

    a = _dot(om_ref[...], wa_bf[...])
    b = _dot(oh_ref[...], wb_bf[...])
    o = _sigmoid(ga_ref[...].astype(F32)) * a + _sigmoid(gb_ref[...].astype(F32)) * b
    o_ref[...] = o.astype(o_ref.dtype)


def _merge(o_mla, o_hgrn, w_a, w_b, hrest, gate_off):
    rows, k = o_mla.shape
    d = w_a.shape[2]
    tm = _row_tile(rows, 640)
    tn = 512
    ga_off = gate_off // tn
    gb_off = (gate_off + d) // tn
    w_spec = pl.BlockSpec((None, k, tn), lambda n, m: (0, 0, n))
    return pl.pallas_call(
        _merge_kernel, name="branch_merge",
        grid=(d // tn, rows // tm),
        in_specs=[pl.BlockSpec((tm, k), lambda n, m: (m, 0)), pl.BlockSpec((tm, k), lambda n, m: (m, 0)),
                  w_spec, w_spec,
                  pl.BlockSpec((tm, tn), lambda n, m: (m, n + ga_off)),
                  pl.BlockSpec((tm, tn), lambda n, m: (m, n + gb_off))],
        out_specs=pl.BlockSpec((tm, tn), lambda n, m: (m, n)),
        out_shape=jax.ShapeDtypeStruct((rows, d), BF16),
        scratch_shapes=[pltpu.VMEM((k, tn), BF16), pltpu.VMEM((k, tn), BF16)],
        compiler_params=_cparams(2),
    )(o_mla, o_hgrn, w_a, w_b, hrest, hrest)


def _out_proj_kernel(mg_ref, w_ref, h_ref, g_ref, h2_ref, u2_ref):
    h2 = h_ref[...] + _dot(mg_ref[...], w_ref[...])
    h2_ref[...] = h2
    u2_ref[...] = _rms(h2, g_ref[...]).astype(u2_ref.dtype)


def _out_proj(merged, w_out, h, g_ffn):
    rows, d = h.shape
    tm = _row_tile(rows, 320)
    row_spec = pl.BlockSpec((tm, d), lambda i: (i, 0))
    return pl.pallas_call(
        _out_proj_kernel, name="mix_out_proj",
        grid=(rows // tm,),
        in_specs=[row_spec, pl.BlockSpec((d, d), lambda i: (0, 0)), row_spec, pl.BlockSpec((1, d), lambda i: (0, 0))],
        out_specs=[row_spec, row_spec],
        out_shape=[jax.ShapeDtypeStruct((rows, d), F32), jax.ShapeDtypeStruct((rows, d), BF16)],
        compiler_params=_cparams(1),
    )(merged, w_out, h, g_ffn)


HALO = 16


def _ffn_in_kernel(u_ref, halo_ref, wg_ref, wu_ref, cw_ref, cb_ref, o_ref, wg_bf, wu_bf, *, n_parts):
    @pl.when(pl.program_id(1) == 0)
    def _():
        wg_bf[...] = wg_ref[...].astype(BF16)
        wu_bf[...] = wu_ref[...].astype(BF16)

    wg = wg_bf[...]
    wu = wu_bf[...]
    cw = cw_ref[...]
    cb = cb_ref[...]
    tp = o_ref.shape[0] // n_parts
    row = lax.broadcasted_iota(jnp.int32, (tp, o_ref.shape[1]), 0)
    tail = _dot(halo_ref[...], wg)[HALO - 2:HALO, :]
    for part in range(n_parts):
        u = u_ref[part * tp:(part + 1) * tp, :]
        gate = _dot(u, wg)
        up = _dot(u, wu)
        back1 = jnp.where(row == 0, tail[1:2, :], pltpu.roll(gate, 1, 0))
        back2 = jnp.where(row == 0, tail[0:1, :], jnp.where(row == 1, tail[1:2, :], pltpu.roll(gate, 2, 0)))
        conv = cw[0:1, :] * back2 + cw[1:2, :] * back1 + cw[2:3, :] * gate + cb
        o_ref[part * tp:(part + 1) * tp, :] = (_silu(conv) * up).astype(o_ref.dtype)
        tail = gate[tp - 2:tp, :]


def _ffn_in(u2, w_fi, conv_w, conv_b, s_real):
    rows, d = u2.shape
    tm = 1024
    tn = 512
    up_off = D_FF // tn
    n_halo = rows // HALO
    halo_map = lambda n, m: ((m * (tm // HALO) + n_halo - 1) % n_halo, 0)
    return pl.pallas_call(
        functools.partial(_ffn_in_kernel, n_parts=2), name="ffn_in_conv_gate",
        grid=(D_FF // tn, s_real // tm),
        in_specs=[pl.BlockSpec((tm, d), lambda n, m: (m, 0)), pl.BlockSpec((HALO, d), halo_map),
                  pl.BlockSpec((None, d, tn), lambda n, m: (0, 0, n)),
                  pl.BlockSpec((None, d, tn), lambda n, m: (0, 0, n + up_off)),
                  pl.BlockSpec((None, conv_w.shape[1], tn), lambda n, m: (0, 0, n)),
                  pl.BlockSpec((1, tn), lambda n, m: (0, n))],
        out_specs=pl.BlockSpec((tm, tn), lambda n, m: (m, n)),
        out_shape=jax.ShapeDtypeStruct((s_real, D_FF), BF16),
        scratch_shapes=[pltpu.VMEM((d, tn), BF16), pltpu.VMEM((d, tn), BF16)],
        compiler_params=_cparams(2),
    )(u2, u2, w_fi, w_fi, conv_w, conv_b)


def _ffn_out_kernel(a_ref, w_ref, h2_ref, g_ref, o_ref):
    kk = pl.program_id(1)

    @pl.when(kk == 0)
    def _():
        o_ref[...] = h2_ref[...]

    o_ref[...] += _dot(a_ref[...], w_ref[...])

    @pl.when(kk == pl.num_programs(1) - 1)
    def _():
        o_ref[...] = _rms(o_ref[...], g_ref[...])


def _ffn_out(act, w_fo, h2, g_final):
    s_real, k = act.shape
    d = w_fo.shape[1]
    tm = 512
    tk = k // 4
    assert tk % LANES == 0
    return pl.pallas_call(
        _ffn_out_kernel, name="ffn_out_final_norm",
        grid=(s_real // tm, k // tk),
        in_specs=[pl.BlockSpec((tm, tk), lambda m, kk: (m, kk)), pl.BlockSpec((tk, d), lambda m, kk: (kk, 0)),
                  pl.BlockSpec((tm, d), lambda m, kk: (m, 0)), pl.BlockSpec((1, d), lambda m, kk: (0, 0))],
        out_specs=pl.BlockSpec((tm, d), lambda m, kk: (m, 0)),
        out_shape=jax.ShapeDtypeStruct((s_real, d), F32),
        compiler_params=_cparams(2),
    )(act, w_fo, h2, g_final)


def kernel(x, positions, meta_tokens, w_in, w_q_up, w_kv_up, w_branch_mla, w_branch_hgrn, w_out, w_ffn_in,
           w_ffn_out, conv_w, conv_b, g_mix_norm, g_q_norm, g_kv_norm, g_hgrn_norm, g_ffn_norm, g_final_norm,
           lb_raw):
    b, s_real, d = x.shape
    assert b == 1 and w_in.shape[0] == 1 and s_real % 512 == 0
    dt = x.dtype

    h = jnp.concatenate([x[0], jnp.zeros((PAD_LEN, d), dt), meta_tokens.astype(dt)], axis=0)
    pos = jnp.concatenate([positions[0].astype(jnp.int32) + N_META, jnp.zeros((PAD_LEN,), jnp.int32),
                           jnp.arange(N_META, dtype=jnp.int32)])
    inv = 1.0 / (ROPE_THETA ** (jnp.arange(0, QK_ROPE_DIM, 2, dtype=F32) / QK_ROPE_DIM))
    inv = jnp.concatenate([inv, inv, jnp.zeros((LANES - QK_ROPE_DIM,), F32)])[None, :]

    w_in_t = jnp.swapaxes(w_in, 1, 2)[0]
    w_lat = _w_lat(w_in_t)
    w_q = jnp.pad(w_q_up[0].reshape(Q_LORA_RANK, MLA_HEADS, QK_HEAD_DIM),
                  ((0, 0), (0, 0), (0, Q_CAT - QK_HEAD_DIM))).reshape(Q_LORA_RANK, MLA_HEADS * Q_CAT).astype(BF16)
    w_kv3 = w_kv_up[0].reshape(KV_LORA_RANK, MLA_HEADS, QK_NOPE_DIM + V_HEAD_DIM)
    w_k = w_kv3[:, :, :QK_NOPE_DIM].reshape(KV_LORA_RANK, -1).astype(BF16)
    w_vt = w_kv3[:, :, QK_NOPE_DIM:].reshape(KV_LORA_RANK, -1).T.astype(BF16)
    w_o = w_out[0].astype(BF16)
    w_fo = w_ffn_out[0].astype(BF16)

    cos_t, sin_t = _rope_tables(pos.astype(F32)[:, None], inv)

    u = _norm_cast(h, g_mix_norm)
    qn, kvn, kr = _proj_lat(u, w_lat, g_q_norm, g_kv_norm, cos_t, sin_t)
    hq = _matmul_cols_t(u, w_in_t, LAT_END, d, BF16, 1024, "proj_hgrn_q")
    hf = _matmul_cols_t(u, w_in_t, LAT_END + d, d, F32, 1024, "proj_hgrn_forget")
    hrest = _matmul_cols_t(u, w_in_t, LAT_END + 2 * d, 4 * d, BF16, 1024, "proj_hgrn_rest")
    q_cat = _q_up(qn, w_q, cos_t, sin_t)
    k_slabs = _kv_up(kvn, w_k)
    v_t = _v_up_t(kvn, w_vt)
    o_mla = _attention(q_cat, k_slabs, v_t, kr, s_real)
    o_hgrn = _hgrn(hq, hf, hrest, lb_raw, g_hgrn_norm)
    merged = _merge(o_mla, o_hgrn, w_branch_mla, w_branch_hgrn, hrest, 2 * d)
    h2, u2 = _out_proj(merged, w_o, h, g_ffn_norm)

    act = _ffn_in(u2, w_ffn_in, conv_w, conv_b, s_real)
    out = _ffn_out(act, w_fo, h2, g_final_norm[None, :])
    return out[None]
```

```python
import functools

import jax
import jax.numpy as jnp
from jax import lax
from jax.experimental import pallas as pl
from jax.experimental.pallas import tpu as pltpu

F32 = jnp.float32
BF16 = jnp.bfloat16

N_META = 16
BLOCK = 128
PAD_LEN = BLOCK - N_META
MLA_HEADS = 16
Q_LORA_RANK = 1536
KV_LORA_RANK = 512
QK_NOPE_DIM = 128
QK_ROPE_DIM = 64
QK_HEAD_DIM = QK_NOPE_DIM + QK_ROPE_DIM
V_HEAD_DIM = 128
ROPE_THETA = 10000.0
HGRN_HEADS = 16
HGRN_EXPAND = 128
HGRN_V_DIM = 128
D_FF = 5632
NORM_EPS = 1e-6

LANES = 128
Q_CAT = 2 * LANES
LAT_COLS = Q_LORA_RANK + KV_LORA_RANK + LANES
LOG2E = 1.4426950408889634
VMEM_LIMIT = 52 * 1024 * 1024


def _cparams(n_axes):
    return pltpu.CompilerParams(dimension_semantics=("arbitrary",) * n_axes, vmem_limit_bytes=VMEM_LIMIT)


def _row_tile(rows, target):
    for t in range(target, 0, -LANES):
        if rows % t == 0:
            return t
    raise ValueError(f"no 128-multiple row tile for {rows}")


def _dot(a, b):
    return jnp.dot(a, b, preferred_element_type=F32)


def _dot_nt(a, b):
    return lax.dot_general(a, b, (((1,), (1,)), ((), ())), preferred_element_type=F32)


def _dot_tn(a, b):
    return lax.dot_general(a, b, (((0,), (0,)), ((), ())), preferred_element_type=F32)


def _rms(x, g):
    return x * lax.rsqrt(jnp.mean(x * x, axis=-1, keepdims=True) + NORM_EPS) * g


def _sigmoid(x):
    return 0.5 * jnp.tanh(0.5 * x) + 0.5


def _silu(x):
    return x * _sigmoid(x)


def _rope(x, cos_t, sin_t):
    rot = pltpu.roll(x, 32, 1) - pltpu.roll(x, 96, 1)
    return x * cos_t + rot * sin_t


def _rope_table_kernel(pos_ref, inv_ref, cos_ref, sin_ref):
    ang = pos_ref[...] * inv_ref[...]
    keep = lax.broadcasted_iota(jnp.int32, ang.shape, 1) < QK_ROPE_DIM
    cos_ref[...] = jnp.where(keep, jnp.cos(ang), 0.0)
    sin_ref[...] = jnp.where(keep, jnp.sin(ang), 0.0)


def _rope_tables(pos_f, inv):
    rows = pos_f.shape[0]
    tm = _row_tile(rows, 640)
    spec = pl.BlockSpec((tm, LANES), lambda i: (i, 0))
    return pl.pallas_call(
        _rope_table_kernel, name="rope_tables",
        grid=(rows // tm,),
        in_specs=[pl.BlockSpec((tm, 1), lambda i: (i, 0)), pl.BlockSpec((1, LANES), lambda i: (0, 0))],
        out_specs=[spec, spec],
        out_shape=[jax.ShapeDtypeStruct((rows, LANES), F32)] * 2,
        compiler_params=_cparams(1),
    )(pos_f, inv)


LAT_END = Q_LORA_RANK + KV_LORA_RANK + QK_ROPE_DIM


def _w_lat_kernel(w_ref, o_ref):
    row = lax.broadcasted_iota(jnp.int32, w_ref.shape, 0)
    o_ref[...] = jnp.where(row < LAT_END, w_ref[...], 0.0).astype(o_ref.dtype)


def _w_lat(w_in_t):
    d = w_in_t.shape[1]
    tk = 512
    return pl.pallas_call(
        _w_lat_kernel, name="w_latent_cast",
        grid=(d // tk,),
        in_specs=[pl.BlockSpec((LAT_COLS, tk), lambda c: (0, c))],
        out_specs=pl.BlockSpec((LAT_COLS, tk), lambda c: (0, c)),
        out_shape=jax.ShapeDtypeStruct((LAT_COLS, d), BF16),
        compiler_params=_cparams(1),
    )(w_in_t)


def _norm_cast_kernel(x_ref, g_ref, o_ref):
    o_ref[...] = _rms(x_ref[...], g_ref[...]).astype(o_ref.dtype)


def _norm_cast(x, g):
    rows, d = x.shape
    tm = _row_tile(rows, 640)
    return pl.pallas_call(
        _norm_cast_kernel, name="mix_norm",
        grid=(rows // tm,),
        in_specs=[pl.BlockSpec((tm, d), lambda i: (i, 0)), pl.BlockSpec((1, d), lambda i: (0, 0))],
        out_specs=pl.BlockSpec((tm, d), lambda i: (i, 0)),
        out_shape=jax.ShapeDtypeStruct((rows, d), BF16),
        compiler_params=_cparams(1),
    )(x, g)


def _proj_lat_kernel(u_ref, w_ref, gq_ref, gkv_ref, cos_ref, sin_ref, qn_ref, kvn_ref, kr_ref):
    y = _dot_nt(u_ref[...], w_ref[...])
    kv_end = Q_LORA_RANK + KV_LORA_RANK
    qn_ref[...] = _rms(y[:, :Q_LORA_RANK], gq_ref[...]).astype(qn_ref.dtype)
    kvn_ref[...] = _rms(y[:, Q_LORA_RANK:kv_end], gkv_ref[...]).astype(kvn_ref.dtype)
    kr_ref[...] = _rope(y[:, kv_end:], cos_ref[...], sin_ref[...]).astype(kr_ref.dtype)


def _proj_lat(u, w_lat, g_q, g_kv, cos_t, sin_t):
    rows, d = u.shape
    tm = _row_tile(rows, 640)
    row_spec = lambda n: pl.BlockSpec((tm, n), lambda i: (i, 0))
    full_spec = lambda r, n: pl.BlockSpec((r, n), lambda i: (0, 0))
    return pl.pallas_call(
        _proj_lat_kernel, name="proj_latents",
        grid=(rows // tm,),
        in_specs=[row_spec(d), full_spec(LAT_COLS, d), full_spec(1, Q_LORA_RANK), full_spec(1, KV_LORA_RANK),
                  row_spec(LANES), row_spec(LANES)],
        out_specs=[row_spec(Q_LORA_RANK), row_spec(KV_LORA_RANK), row_spec(LANES)],
        out_shape=[jax.ShapeDtypeStruct((rows, Q_LORA_RANK), BF16),
                   jax.ShapeDtypeStruct((rows, KV_LORA_RANK), BF16),
                   jax.ShapeDtypeStruct((rows, LANES), BF16)],
        compiler_params=_cparams(1),
    )(u, w_lat, g_q, g_kv, cos_t, sin_t)


def _matmul_t_kernel(x_ref, w_ref, o_ref, w_bf):
    @pl.when(pl.program_id(1) == 0)
    def _():
        w_bf[...] = w_ref[...].astype(BF16)

    o_ref[...] = _dot_nt(x_ref[...], w_bf[...]).astype(o_ref.dtype)


def _matmul_cols_t(x, w_t, col_start, n_cols, out_dtype, tn, name):
    rows, k = x.shape
    tm = _row_tile(rows, 640)
    assert col_start % 8 == 0 and n_cols % tn == 0
    return pl.pallas_call(
        _matmul_t_kernel, name=name,
        grid=(n_cols // tn, rows // tm),
        in_specs=[pl.BlockSpec((tm, k), lambda n, m: (m, 0)),
                  pl.BlockSpec((pl.Element(tn), pl.Element(k)),
                               lambda n, m: (pl.multiple_of(col_start + n * tn, 8), 0))],
        out_specs=pl.BlockSpec((tm, tn), lambda n, m: (m, n)),
        out_shape=jax.ShapeDtypeStruct((rows, n_cols), out_dtype),
        scratch_shapes=[pltpu.VMEM((tn, k), BF16)],
        compiler_params=_cparams(2),
    )(x, w_t)


def _kv_up_kernel(x_ref, w_ref, o_ref):
    y = _dot(x_ref[...], w_ref[...])
    for hh in range(o_ref.shape[0]):
        o_ref[hh] = y[:, hh * LANES:(hh + 1) * LANES].astype(o_ref.dtype)


V_EXT = V_HEAD_DIM + 16


def _v_up_t_kernel(x_ref, w_ref, o_ref):
    y_t = _dot_nt(w_ref[...], x_ref[...])
    pad_shape = (V_EXT - V_HEAD_DIM, o_ref.shape[2])
    ones_row = (lax.broadcasted_iota(jnp.int32, pad_shape, 0) == 0).astype(o_ref.dtype)
    for hh in range(o_ref.shape[0]):
        o_ref[hh, :LANES, :] = y_t[hh * LANES:(hh + 1) * LANES, :].astype(o_ref.dtype)
        o_ref[hh, LANES:, :] = ones_row


def _v_up_t(kvn, w_vt):
    rows, k = kvn.shape
    n_slabs = w_vt.shape[0] // LANES
    tm = _row_tile(rows, 640)
    per_tile = 8
    return pl.pallas_call(
        _v_up_t_kernel, name="v_up_transposed",
        grid=(n_slabs // per_tile, rows // tm),
        in_specs=[pl.BlockSpec((tm, k), lambda n, m: (m, 0)),
                  pl.BlockSpec((per_tile * LANES, k), lambda n, m: (n, 0))],
        out_specs=pl.BlockSpec((per_tile, V_EXT, tm), lambda n, m: (n, 0, m)),
        out_shape=jax.ShapeDtypeStruct((n_slabs, V_EXT, rows), BF16),
        compiler_params=_cparams(2),
    )(kvn, w_vt)


def _kv_up(kvn, w_kv):
    rows, k = kvn.shape
    n_slabs = w_kv.shape[1] // LANES
    tm = _row_tile(rows, 640)
    per_tile = 8
    return pl.pallas_call(
        _kv_up_kernel, name="kv_up",
        grid=(n_slabs // per_tile, rows // tm),
        in_specs=[pl.BlockSpec((tm, k), lambda n, m: (m, 0)),
                  pl.BlockSpec((k, per_tile * LANES), lambda n, m: (0, n))],
        out_specs=pl.BlockSpec((per_tile, tm, LANES), lambda n, m: (n, m, 0)),
        out_shape=jax.ShapeDtypeStruct((n_slabs, rows, LANES), BF16),
        compiler_params=_cparams(2),
    )(kvn, w_kv)


def _q_up_kernel(x_ref, w_ref, cos_ref, sin_ref, o_ref, *, heads_per_tile):
    y = _dot(x_ref[...], w_ref[...]) * (QK_HEAD_DIM ** -0.5 * LOG2E)
    cos_t = cos_ref[...]
    sin_t = sin_ref[...]
    for hh in range(heads_per_tile):
        lo = hh * Q_CAT
        o_ref[:, lo:lo + LANES] = y[:, lo:lo + LANES].astype(o_ref.dtype)
        o_ref[:, lo + LANES:lo + Q_CAT] = _rope(y[:, lo + LANES:lo + Q_CAT], cos_t, sin_t).astype(o_ref.dtype)


def _q_up(qn, w_q, cos_t, sin_t):
    rows, k = qn.shape
    n_cols = w_q.shape[1]
    tm = _row_tile(rows, 640)
    tn = 4 * Q_CAT
    return pl.pallas_call(
        functools.partial(_q_up_kernel, heads_per_tile=tn // Q_CAT), name="q_up_rope",
        grid=(n_cols // tn, rows // tm),
        in_specs=[pl.BlockSpec((tm, k), lambda n, m: (m, 0)), pl.BlockSpec((k, tn), lambda n, m: (0, n)),
                  pl.BlockSpec((tm, LANES), lambda n, m: (m, 0)), pl.BlockSpec((tm, LANES), lambda n, m: (m, 0))],
        out_specs=pl.BlockSpec((tm, tn), lambda n, m: (m, n)),
        out_shape=jax.ShapeDtypeStruct((rows, n_cols), BF16),
        compiler_params=_cparams(2),
    )(qn, w_q, cos_t, sin_t)


ATTN_KEY_BLOCK = 64
ATTN_PART = 2 * LANES


def _attn_kernel(q_ref, kn_ref, vt_ref, kr_ref, o_ref, kcat_ref, m_ref, acc_ref, sa0_ref, sa1_ref, sb0_ref,
                 sb1_ref, spre_ref, p_ref, *, tq, n_real_tiles):
    step_id = pl.program_id(1)
    qi = step_id - 1
    s_real = n_real_tiles * tq
    tk = tq // 2

    @pl.when(step_id == 0)
    def _():
        kcat_ref[:, :LANES] = kn_ref[...]
        kcat_ref[:, LANES:] = kr_ref[...]

    th = ATTN_PART
    parts = tuple(range(0, tq, th))

    def init():
        m_ref[...] = jnp.full(m_ref.shape, -jnp.inf, F32)
        acc_ref[...] = jnp.zeros(acc_ref.shape, F32)

    def update(c0, blocks, pv):
        cols = slice(c0, c0 + th)

        def load(ref, row0, n, visible):
            s = ref[row0:row0 + n, cols]
            return s if visible is None else jnp.where(visible(row0, n), s, -jnp.inf)

        m_prev = m_ref[:, cols]
        m_cur = functools.reduce(
            jnp.maximum, [jnp.max(load(ref, r, n, vis), axis=0, keepdims=True) for ref, r, n, _, vis in blocks])
        m_new = jnp.maximum(m_prev, m_cur)
        alpha = jnp.exp2(m_prev - m_new)
        m_ref[:, cols] = m_new
        for ref, r, n, p_row, vis in blocks:
            for b0 in range(0, n, ATTN_KEY_BLOCK):
                p = jnp.exp2(load(ref, r + b0, ATTN_KEY_BLOCK, vis) - m_new)
                p_ref[p_row + b0:p_row + b0 + ATTN_KEY_BLOCK, cols] = p.astype(p_ref.dtype)
        new = functools.reduce(jnp.add, [_dot(v_t, p_ref[p_row:p_row + n, cols]) for p_row, n, v_t in pv])
        acc_ref[:, cols] = alpha * acc_ref[:, cols] + new

    def finalize():
        acc = acc_ref[...]
        o_ref[...] = (acc[:V_HEAD_DIM, :] / acc[V_HEAD_DIM:V_HEAD_DIM + 1, :]).T.astype(o_ref.dtype)

    def prefix_scores():
        k_pre = kcat_ref[s_real:s_real + BLOCK, :]
        for c0 in parts:
            spre_ref[:, c0:c0 + th] = _dot_nt(k_pre, q_ref[c0:c0 + th, :])

    v_pre_t = vt_ref[:, s_real:s_real + BLOCK]
    pre_row = tk

    def key_ids(row0, n):
        return lax.broadcasted_iota(jnp.int32, (n, th), 0) + row0

    def query_ids(c0, n):
        return lax.broadcasted_iota(jnp.int32, (n, th), 1) + c0

    @pl.when(step_id > 0)
    def _():
        init()
        prefix_scores()

        def raw_scores(dst_ref, j, which=parts):
            k_blk = kcat_ref[pl.ds(pl.multiple_of(j * tk, tk), tk), :]
            for c0 in which:
                dst_ref[:, c0:c0 + th] = _dot_nt(k_blk, q_ref[c0:c0 + th, :])

        def full_update(src_ref, tile, which=parts):
            v_t = vt_ref[:, pl.ds(pl.multiple_of(tile * tk, tk), tk)]
            for c0 in which:
                update(c0, [(src_ref, 0, tk, 0, None)], [(0, tk, v_t)])

        def pair(a, src, dst):
            raw_scores(dst[0], a + 2)
            raw_scores(dst[1], a + 3)
            full_update(src[0], a)
            full_update(src[1], a + 1)

        def diagonal(c0, src_ref, tile):
            q0 = c0 % tk
            n_k = q0 + th
            causal = lambda row0, n: key_ids(row0, n) <= query_ids(q0, n)
            no_pads = lambda row0, n: key_ids(row0, n) >= PAD_LEN
            start = pl.multiple_of(tile * tk, tk)
            update(c0, [(src_ref, 0, n_k, 0, causal), (spre_ref, 0, BLOCK, pre_row, no_pads)],
                   [(0, n_k, vt_ref[:, pl.ds(start, n_k)]), (pre_row, BLOCK, v_pre_t)])

        def own_tiles(src):
            lower = tuple(c0 for c0 in parts if c0 < tk)
            upper = tuple(c0 for c0 in parts if c0 >= tk)
            for c0 in lower:
                diagonal(c0, src[0], 2 * qi)
            full_update(src[0], 2 * qi, upper)
            for c0 in upper:
                diagonal(c0, src[1], 2 * qi + 1)
            finalize()

        buf_a = (sa0_ref, sa1_ref)
        buf_b = (sb0_ref, sb1_ref)
        raw_scores(buf_a[0], 0)
        raw_scores(buf_a[1], 1)

        def body(i, carry):
            pair(4 * i, buf_a, buf_b)
            pair(4 * i + 2, buf_b, buf_a)
            return carry

        lax.fori_loop(0, qi // 2, body, 0)

        @pl.when(qi % 2 == 1)
        def _():
            pair(2 * qi - 2, buf_a, buf_b)
            own_tiles(buf_b)

        @pl.when(qi % 2 == 0)
        def _():
            own_tiles(buf_a)

    @pl.when(step_id == 0)
    def _():
        init()
        prefix_scores()
        for c0 in parts:
            def visible(row0, n, c0=c0):
                key, query = key_ids(row0, n), query_ids(c0, n)
                return (key <= query) & ((key >= PAD_LEN) | (key == query))

            update(c0, [(spre_ref, 0, BLOCK, pre_row, visible)], [(pre_row, BLOCK, v_pre_t)])
        finalize()


def _attention(q_cat, k_slabs, v_t, kr, s_real):
    rows = q_cat.shape[0]
    tq = 1024
    tk = tq // 2
    n_real_tiles = s_real // tq
    q_tile = lambda h, i: ((i + n_real_tiles) % (n_real_tiles + 1), h)
    return pl.pallas_call(
        functools.partial(_attn_kernel, tq=tq, n_real_tiles=n_real_tiles), name="mla_attention",
        grid=(MLA_HEADS, n_real_tiles + 1),
        in_specs=[pl.BlockSpec((tq, Q_CAT), q_tile),
                  pl.BlockSpec((None, rows, LANES), lambda h, i: (h, 0, 0)),
                  pl.BlockSpec((None, V_EXT, rows), lambda h, i: (h, 0, 0)),
                  pl.BlockSpec((rows, LANES), lambda h, i: (0, 0))],
        out_specs=pl.BlockSpec((tq, V_HEAD_DIM), q_tile),
        out_shape=jax.ShapeDtypeStruct((rows, MLA_HEADS * V_HEAD_DIM), BF16),
        scratch_shapes=[pltpu.VMEM((rows, Q_CAT), BF16), pltpu.VMEM((1, tq), F32),
                        pltpu.VMEM((V_EXT, tq), F32)] + [pltpu.VMEM((tk, tq), F32)] * 4
                       + [pltpu.VMEM((BLOCK, tq), F32), pltpu.VMEM((tk + BLOCK, tq), BF16)],
        compiler_params=_cparams(2),
    )(q_cat, k_slabs, v_t, kr)


def _split3(x):
    x1 = x.astype(BF16)
    r1 = x - x1.astype(F32)
    x2 = r1.astype(BF16)
    x3 = (r1 - x2.astype(F32)).astype(BF16)
    return x1, x2, x3


def _block_row(x, size, j):
    c, n = x.shape
    g = x.reshape(c // size, size, n)[:, j:j + 1, :]
    return jnp.broadcast_to(g, (c // size, size, n)).reshape(c, n)


HGRN_LEVELS = 7


def _hgrn_head(hq, hf, hi, hg, lb_raw, gain, st, valid, row, lev):
    n = BLOCK
    top = jnp.max(lb_raw, axis=0, keepdims=True)
    e = jnp.exp(lb_raw - top)
    lb = e[0:1, :] / jnp.sum(e, axis=0, keepdims=True)

    f = lb + (1.0 - lb) * _sigmoid(hf)
    f_eff = jnp.where(valid, f, 1.0)
    g = jnp.where(valid, jnp.log(f), 0.0)
    k = jnp.where(valid, 1.0 - f, 0.0)
    q = _silu(hq.astype(F32))

    col = lax.broadcasted_iota(jnp.int32, (n, n), 1)
    tri = (col <= row).astype(BF16)
    g1, g2, g3 = _split3(g)
    b = _dot(tri, g1) + _dot(tri, g2) + _dot(tri, g3)

    a = jnp.where(lev == 0, jnp.sum(q * k, axis=1, keepdims=True), 0.0)
    for level in range(1, HGRN_LEVELS + 1):
        size = 1 << level
        if level == 1:
            w = jnp.where((row & 1) == 1, f_eff, 1.0)
        elif level == 2:
            r4 = row & 3
            up1 = pltpu.roll(g, n - 1, 0)
            dn1 = pltpu.roll(g, 1, 0)
            w = jnp.exp(jnp.where(r4 == 0, up1, jnp.where(r4 == 1, 0.0, jnp.where(r4 == 2, g, g + dn1))))
        else:
            w = jnp.exp(-jnp.abs(b - _block_row(b, size, size // 2 - 1)))
        a_l = _dot_nt((q * w).astype(BF16), (k * w).astype(BF16))
        a = jnp.where(lev == level, a_l, a)

    o = _dot(a.astype(BF16), hi) + _dot_nt((q * jnp.exp(b)).astype(BF16), st.astype(BF16))
    b_last = b[n - 1:n, :]
    st_new = st * jnp.exp(b_last) + _dot_tn(hi, (k * jnp.exp(b_last - b)).astype(BF16))
    out = _rms(o, gain) * _silu(hg.astype(F32))
    return out, st_new


def _hgrn_kernel(hq_ref, hf_ref, hi_ref, hg_ref, lb_ref, g_ref, o_ref, st_ref):
    c = pl.program_id(0)
    n = BLOCK

    @pl.when(c == 0)
    def _():
        st_ref[...] = jnp.zeros(st_ref.shape, F32)

    row = lax.broadcasted_iota(jnp.int32, (n, n), 0)
    col = lax.broadcasted_iota(jnp.int32, (n, n), 1)
    valid = (c > 0) | (row >= PAD_LEN)
    x = row ^ col
    lev = functools.reduce(jnp.add, [(x >= (1 << i)).astype(jnp.int32) for i in range(HGRN_LEVELS)])
    lev = jnp.where(col > row, -1, lev)
    gain = g_ref[...]

    def head_pair(i, carry):
        for hh in range(2):
            head = 2 * i + hh
            cols = pl.ds(pl.multiple_of(head * BLOCK, BLOCK), BLOCK)
            out, st_new = _hgrn_head(hq_ref[:, cols], hf_ref[:, cols], hi_ref[:, cols], hg_ref[:, cols],
                                     lb_ref[:, cols], gain, st_ref[head], valid, row, lev)
            st_ref[head] = st_new
            o_ref[:, cols] = out.astype(o_ref.dtype)
        return carry

    lax.fori_loop(0, HGRN_HEADS // 2, head_pair, 0)


def _hgrn(hq, hf, hrest, lb_raw, g_hgrn):
    rows, width = hq.shape
    n_chunks = rows // BLOCK
    blk = lambda off: pl.BlockSpec((BLOCK, width), lambda c: ((c + n_chunks - 1) % n_chunks, off))
    return pl.pallas_call(
        _hgrn_kernel, name="hgrn2",
        grid=(n_chunks,),
        in_specs=[blk(0), blk(0), blk(0), blk(1),
                  pl.BlockSpec((lb_raw.shape[0], width), lambda c: (0, 0)),
                  pl.BlockSpec((1, HGRN_V_DIM), lambda c: (0, 0))],
        out_specs=blk(0),
        out_shape=jax.ShapeDtypeStruct((rows, width), BF16),
        scratch_shapes=[pltpu.VMEM((HGRN_HEADS, HGRN_V_DIM, HGRN_EXPAND), F32)],
        compiler_params=_cparams(1),
    )(hq, hf, hrest, hrest, lb_raw, g_hgrn)


def _merge_kernel(om_ref, oh_ref, wa_ref, wb_ref, ga_ref, gb_ref, o_ref, wa_bf, wb_bf):
    @pl.when(pl.program_id(1) == 0)
    def _():
        wa_bf[...] = wa_ref[...].astype(BF16)
        wb_bf[...] = wb_ref[...].astype(BF16)

    a = _dot(om_ref[...], wa_bf[...])
    b = _dot(oh_ref[...], wb_bf[...])
    o = _sigmoid(ga_ref[...].astype(F32)) * a + _sigmoid(gb_ref[...].astype(F32)) * b
    o_ref[...] = o.astype(o_ref.dtype)


def _merge(o_mla, o_hgrn, w_a, w_b, hrest, gate_off):
    rows, k = o_mla.shape
    d = w_a.shape[2]
    tm = _row_tile(rows, 640)
    tn = 512
    ga_off = gate_off // tn
    gb_off = (gate_off + d) // tn
    w_spec = pl.BlockSpec((None, k, tn), lambda n, m: (0, 0, n))
    return pl.pallas_call(
        _merge_kernel, name="branch_merge",
        grid=(d // tn, rows // tm),
        in_specs=[pl.BlockSpec((tm, k), lambda n, m: (m, 0)), pl.BlockSpec((tm, k), lambda n, m: (m, 0)),
                  w_spec, w_spec,
                  pl.BlockSpec((tm, tn), lambda n, m: (m, n + ga_off)),
                  pl.BlockSpec((tm, tn), lambda n, m: (m, n + gb_off))],
        out_specs=pl.BlockSpec((tm, tn), lambda n, m: (m, n)),
        out_shape=jax.ShapeDtypeStruct((rows, d), BF16),
        scratch_shapes=[pltpu.VMEM((k, tn), BF16), pltpu.VMEM((k, tn), BF16)],
        compiler_params=_cparams(2),
    )(o_mla, o_hgrn, w_a, w_b, hrest, hrest)


def _out_proj_kernel(mg_ref, w_ref, h_ref, g_ref, h2_ref, u2_ref):
    h2 = h_ref[...] + _dot(mg_ref[...], w_ref[...])
    h2_ref[...] = h2
    u2_ref[...] = _rms(h2, g_ref[...]).astype(u2_ref.dtype)


def _out_proj(merged, w_out, h, g_ffn):
    rows, d = h.shape
    tm = _row_tile(rows, 320)
    row_spec = pl.BlockSpec((tm, d), lambda i: (i, 0))
    return pl.pallas_call(
        _out_proj_kernel, name="mix_out_proj",
        grid=(rows // tm,),
        in_specs=[row_spec, pl.BlockSpec((d, d), lambda i: (0, 0)), row_spec, pl.BlockSpec((1, d), lambda i: (0, 0))],
        out_specs=[row_spec, row_spec],
        out_shape=[jax.ShapeDtypeStruct((rows, d), F32), jax.ShapeDtypeStruct((rows, d), BF16)],
        compiler_params=_cparams(1),
    )(merged, w_out, h, g_ffn)


HALO = 16


def _ffn_in_kernel(u_ref, halo_ref, wg_ref, wu_ref, cw_ref, cb_ref, o_ref, wg_bf, wu_bf, *, n_parts):
    @pl.when(pl.program_id(1) == 0)
    def _():
        wg_bf[...] = wg_ref[...].astype(BF16)
        wu_bf[...] = wu_ref[...].astype(BF16)

    wg = wg_bf[...]
    wu = wu_bf[...]
    cw = cw_ref[...]
    cb = cb_ref[...]
    tp = o_ref.shape[0] // n_parts
    row = lax.broadcasted_iota(jnp.int32, (tp, o_ref.shape[1]), 0)
    tail = _dot(halo_ref[...], wg)[HALO - 2:HALO, :]
    for part in range(n_parts):
        u = u_ref[part * tp:(part + 1) * tp, :]
        gate = _dot(u, wg)
        up = _dot(u, wu)
        back1 = jnp.where(row == 0, tail[1:2, :], pltpu.roll(gate, 1, 0))
        back2 = jnp.where(row == 0, tail[0:1, :], jnp.where(row == 1, tail[1:2, :], pltpu.roll(gate, 2, 0)))
        conv = cw[0:1, :] * back2 + cw[1:2, :] * back1 + cw[2:3, :] * gate + cb
        o_ref[part * tp:(part + 1) * tp, :] = (_silu(conv) * up).astype(o_ref.dtype)
        tail = gate[tp - 2:tp, :]


def _ffn_in(u2, w_fi, conv_w, conv_b, s_real):
    rows, d = u2.shape
    tm = 1024
    tn = 512
    up_off = D_FF // tn
    n_halo = rows // HALO
    halo_map = lambda n, m: ((m * (tm // HALO) + n_halo - 1) % n_halo, 0)
    return pl.pallas_call(
        functools.partial(_ffn_in_kernel, n_parts=2), name="ffn_in_conv_gate",
        grid=(D_FF // tn, s_real // tm),
        in_specs=[pl.BlockSpec((tm, d), lambda n, m: (m, 0)), pl.BlockSpec((HALO, d), halo_map),
                  pl.BlockSpec((None, d, tn), lambda n, m: (0, 0, n)),
                  pl.BlockSpec((None, d, tn), lambda n, m: (0, 0, n + up_off)),
                  pl.BlockSpec((None, conv_w.shape[1], tn), lambda n, m: (0, 0, n)),
                  pl.BlockSpec((1, tn), lambda n, m: (0, n))],
        out_specs=pl.BlockSpec((tm, tn), lambda n, m: (m, n)),
        out_shape=jax.ShapeDtypeStruct((s_real, D_FF), BF16),
        scratch_shapes=[pltpu.VMEM((d, tn), BF16), pltpu.VMEM((d, tn), BF16)],
        compiler_params=_cparams(2),
    )(u2, u2, w_fi, w_fi, conv_w, conv_b)


def _ffn_out_kernel(a_ref, w_ref, h2_ref, g_ref, o_ref):
    kk = pl.program_id(1)

    @pl.when(kk == 0)
    def _():
        o_ref[...] = h2_ref[...]

    o_ref[...] += _dot(a_ref[...], w_ref[...])

    @pl.when(kk == pl.num_programs(1) - 1)
    def _():
        o_ref[...] = _rms(o_ref[...], g_ref[...])


def _ffn_out(act, w_fo, h2, g_final):
    s_real, k = act.shape
    d = w_fo.shape[1]
    tm = 512
    tk = k // 4
    assert tk % LANES == 0
    return pl.pallas_call(
        _ffn_out_kernel, name="ffn_out_final_norm",
        grid=(s_real // tm, k // tk),
        in_specs=[pl.BlockSpec((tm, tk), lambda m, kk: (m, kk)), pl.BlockSpec((tk, d), lambda m, kk: (kk, 0)),
                  pl.BlockSpec((tm, d), lambda m, kk: (m, 0)), pl.BlockSpec((1, d), lambda m, kk: (0, 0))],
        out_specs=pl.BlockSpec((tm, d), lambda m, kk: (m, 0)),
        out_shape=jax.ShapeDtypeStruct((s_real, d), F32),
        compiler_params=_cparams(2),
    )(act, w_fo, h2, g_final)


def kernel(x, positions, meta_tokens, w_in, w_q_up, w_kv_up, w_branch_mla, w_branch_hgrn, w_out, w_ffn_in,
           w_ffn_out, conv_w, conv_b, g_mix_norm, g_q_norm, g_kv_norm, g_hgrn_norm, g_ffn_norm, g_final_norm,
           lb_raw):
    b, s_real, d = x.shape
    assert b == 1 and w_in.shape[0] == 1 and s_real % 1024 == 0
    dt = x.dtype

    h = jnp.concatenate([x[0], jnp.zeros((PAD_LEN, d), dt), meta_tokens.astype(dt)], axis=0)
    pos = jnp.concatenate([positions[0].astype(jnp.int32) + N_META, jnp.zeros((PAD_LEN,), jnp.int32),
                           jnp.arange(N_META, dtype=jnp.int32)])
    inv = 1.0 / (ROPE_THETA ** (jnp.arange(0, QK_ROPE_DIM, 2, dtype=F32) / QK_ROPE_DIM))
    inv = jnp.concatenate([inv, inv, jnp.zeros((LANES - QK_ROPE_DIM,), F32)])[None, :]

    w_in_t = jnp.swapaxes(w_in, 1, 2)[0]
    w_lat = _w_lat(w_in_t)
    w_q = jnp.pad(w_q_up[0].reshape(Q_LORA_RANK, MLA_HEADS, QK_HEAD_DIM),
                  ((0, 0), (0, 0), (0, Q_CAT - QK_HEAD_DIM))).reshape(Q_LORA_RANK, MLA_HEADS * Q_CAT).astype(BF16)
    w_kv3 = w_kv_up[0].reshape(KV_LORA_RANK, MLA_HEADS, QK_NOPE_DIM + V_HEAD_DIM)
    w_k = w_kv3[:, :, :QK_NOPE_DIM].reshape(KV_LORA_RANK, -1).astype(BF16)
    w_vt = w_kv3[:, :, QK_NOPE_DIM:].reshape(KV_LORA_RANK, -1).T.astype(BF16)
    w_o = w_out[0].astype(BF16)
    w_fo = w_ffn_out[0].astype(BF16)

    cos_t, sin_t = _rope_tables(pos.astype(F32)[:, None], inv)

    u = _norm_cast(h, g_mix_norm)
    qn, kvn, kr = _proj_lat(u, w_lat, g_q_norm, g_kv_norm, cos_t, sin_t)
    hq = _matmul_cols_t(u, w_in_t, LAT_END, d, BF16, 1024, "proj_hgrn_q")
    hf = _matmul_cols_t(u, w_in_t, LAT_END + d, d, F32, 1024, "proj_hgrn_forget")
    hrest = _matmul_cols_t(u, w_in_t, LAT_END + 2 * d, 4 * d, BF16, 1024, "proj_hgrn_rest")
    q_cat = _q_up(qn, w_q, cos_t, sin_t)
    k_slabs = _kv_up(kvn, w_k)
    v_t = _v_up_t(kvn, w_vt)
    o_mla = _attention(q_cat, k_slabs, v_t, kr, s_real)
    o_hgrn = _hgrn(hq, hf, hrest, lb_raw, g_hgrn_norm)
    merged = _merge(o_mla, o_hgrn, w_branch_mla, w_branch_hgrn, hrest, 2 * d)
    h2, u2 = _out_proj(merged, w_o, h, g_ffn_norm)

    act = _ffn_in(u2, w_ffn_in, conv_w, conv_b, s_real)
    out = _ffn_out(act, w_fo, h2, g_final_norm[None, :])
    return out[None]
```

```python
import functools

import jax
import jax.numpy as jnp
from jax import lax
from jax.experimental import pallas as pl
from jax.experimental.pallas import tpu as pltpu

F32 = jnp.float32
BF16 = jnp.bfloat16

N_META = 16
BLOCK = 128
PAD_LEN = BLOCK - N_META
MLA_HEADS = 16
Q_LORA_RANK = 1536
KV_LORA_RANK = 512
QK_NOPE_DIM = 128
QK_ROPE_DIM = 64
QK_HEAD_DIM = QK_NOPE_DIM + QK_ROPE_DIM
V_HEAD_DIM = 128
ROPE_THETA = 10000.0
HGRN_HEADS = 16
HGRN_EXPAND = 128
HGRN_V_DIM = 128
D_FF = 5632
NORM_EPS = 1e-6

LANES = 128
Q_CAT = 2 * LANES
LAT_COLS = Q_LORA_RANK + KV_LORA_RANK + LANES
LOG2E = 1.4426950408889634
VMEM_LIMIT = 52 * 1024 * 1024


def _cparams(n_axes):
    return pltpu.CompilerParams(dimension_semantics=("arbitrary",) * n_axes, vmem_limit_bytes=VMEM_LIMIT)


def _row_tile(rows, target):
    for t in range(target, 0, -LANES):
        if rows % t == 0:
            return t
    raise ValueError(f"no 128-multiple row tile for {rows}")


def _dot(a, b):
    return jnp.dot(a, b, preferred_element_type=F32)


def _dot_nt(a, b):
    return lax.dot_general(a, b, (((1,), (1,)), ((), ())), preferred_element_type=F32)


def _dot_tn(a, b):
    return lax.dot_general(a, b, (((0,), (0,)), ((), ())), preferred_element_type=F32)


def _rms(x, g):
    return x * lax.rsqrt(jnp.mean(x * x, axis=-1, keepdims=True) + NORM_EPS) * g


def _sigmoid(x):
    return 0.5 * jnp.tanh(0.5 * x) + 0.5


def _silu(x):
    return x * _sigmoid(x)


def _rope(x, cos_t, sin_t):
    rot = pltpu.roll(x, 32, 1) - pltpu.roll(x, 96, 1)
    return x * cos_t + rot * sin_t


def _rope_table_kernel(pos_ref, inv_ref, cos_ref, sin_ref):
    ang = pos_ref[...] * inv_ref[...]
    keep = lax.broadcasted_iota(jnp.int32, ang.shape, 1) < QK_ROPE_DIM
    cos_ref[...] = jnp.where(keep, jnp.cos(ang), 0.0)
    sin_ref[...] = jnp.where(keep, jnp.sin(ang), 0.0)


def _rope_tables(pos_f, inv):
    rows = pos_f.shape[0]
    tm = _row_tile(rows, 640)
    spec = pl.BlockSpec((tm, LANES), lambda i: (i, 0))
    return pl.pallas_call(
        _rope_table_kernel, name="rope_tables",
        grid=(rows // tm,),
        in_specs=[pl.BlockSpec((tm, 1), lambda i: (i, 0)), pl.BlockSpec((1, LANES), lambda i: (0, 0))],
        out_specs=[spec, spec],
        out_shape=[jax.ShapeDtypeStruct((rows, LANES), F32)] * 2,
        compiler_params=_cparams(1),
    )(pos_f, inv)


LAT_END = Q_LORA_RANK + KV_LORA_RANK + QK_ROPE_DIM


def _w_lat_kernel(w_ref, o_ref):
    row = lax.broadcasted_iota(jnp.int32, w_ref.shape, 0)
    o_ref[...] = jnp.where(row < LAT_END, w_ref[...], 0.0).astype(o_ref.dtype)


def _w_lat(w_in_t):
    d = w_in_t.shape[1]
    tk = 512
    return pl.pallas_call(
        _w_lat_kernel, name="w_latent_cast",
        grid=(d // tk,),
        in_specs=[pl.BlockSpec((LAT_COLS, tk), lambda c: (0, c))],
        out_specs=pl.BlockSpec((LAT_COLS, tk), lambda c: (0, c)),
        out_shape=jax.ShapeDtypeStruct((LAT_COLS, d), BF16),
        compiler_params=_cparams(1),
    )(w_in_t)


def _norm_cast_kernel(x_ref, g_ref, o_ref):
    o_ref[...] = _rms(x_ref[...], g_ref[...]).astype(o_ref.dtype)


def _norm_cast(x, g):
    rows, d = x.shape
    tm = _row_tile(rows, 640)
    return pl.pallas_call(
        _norm_cast_kernel, name="mix_norm",
        grid=(rows // tm,),
        in_specs=[pl.BlockSpec((tm, d), lambda i: (i, 0)), pl.BlockSpec((1, d), lambda i: (0, 0))],
        out_specs=pl.BlockSpec((tm, d), lambda i: (i, 0)),
        out_shape=jax.ShapeDtypeStruct((rows, d), BF16),
        compiler_params=_cparams(1),
    )(x, g)


def _proj_lat_kernel(u_ref, w_ref, gq_ref, gkv_ref, cos_ref, sin_ref, qn_ref, kvn_ref, kr_ref):
    y = _dot_nt(u_ref[...], w_ref[...])
    kv_end = Q_LORA_RANK + KV_LORA_RANK
    qn_ref[...] = _rms(y[:, :Q_LORA_RANK], gq_ref[...]).astype(qn_ref.dtype)
    kvn_ref[...] = _rms(y[:, Q_LORA_RANK:kv_end], gkv_ref[...]).astype(kvn_ref.dtype)
    kr_ref[...] = _rope(y[:, kv_end:], cos_ref[...], sin_ref[...]).astype(kr_ref.dtype)


def _proj_lat(u, w_lat, g_q, g_kv, cos_t, sin_t):
    rows, d = u.shape
    tm = _row_tile(rows, 640)
    row_spec = lambda n: pl.BlockSpec((tm, n), lambda i: (i, 0))
    full_spec = lambda r, n: pl.BlockSpec((r, n), lambda i: (0, 0))
    return pl.pallas_call(
        _proj_lat_kernel, name="proj_latents",
        grid=(rows // tm,),
        in_specs=[row_spec(d), full_spec(LAT_COLS, d), full_spec(1, Q_LORA_RANK), full_spec(1, KV_LORA_RANK),
                  row_spec(LANES), row_spec(LANES)],
        out_specs=[row_spec(Q_LORA_RANK), row_spec(KV_LORA_RANK), row_spec(LANES)],
        out_shape=[jax.ShapeDtypeStruct((rows, Q_LORA_RANK), BF16),
                   jax.ShapeDtypeStruct((rows, KV_LORA_RANK), BF16),
                   jax.ShapeDtypeStruct((rows, LANES), BF16)],
        compiler_params=_cparams(1),
    )(u, w_lat, g_q, g_kv, cos_t, sin_t)


def _matmul_t_kernel(x_ref, w_ref, o_ref, w_bf):
    @pl.when(pl.program_id(1) == 0)
    def _():
        w_bf[...] = w_ref[...].T.astype(BF16)

    o_ref[...] = _dot(x_ref[...], w_bf[...]).astype(o_ref.dtype)


def _matmul_cols_t(x, w_t, col_start, n_cols, out_dtype, tn, name):
    rows, k = x.shape
    tm = _row_tile(rows, 640)
    assert col_start % 8 == 0 and n_cols % tn == 0
    return pl.pallas_call(
        _matmul_t_kernel, name=name,
        grid=(n_cols // tn, rows // tm),
        in_specs=[pl.BlockSpec((tm, k), lambda n, m: (m, 0)),
                  pl.BlockSpec((pl.Element(tn), pl.Element(k)),
                               lambda n, m: (pl.multiple_of(col_start + n * tn, 8), 0))],
        out_specs=pl.BlockSpec((tm, tn), lambda n, m: (m, n)),
        out_shape=jax.ShapeDtypeStruct((rows, n_cols), out_dtype),
        scratch_shapes=[pltpu.VMEM((k, tn), BF16)],
        compiler_params=_cparams(2),
    )(x, w_t)


def _kv_up_kernel(x_ref, w_ref, o_ref):
    y = _dot(x_ref[...], w_ref[...])
    for hh in range(o_ref.shape[0]):
        o_ref[hh] = y[:, hh * LANES:(hh + 1) * LANES].astype(o_ref.dtype)


V_EXT = V_HEAD_DIM + 16


def _v_up_t_kernel(x_ref, w_ref, o_ref):
    y_t = _dot_nt(w_ref[...], x_ref[...])
    pad_shape = (V_EXT - V_HEAD_DIM, o_ref.shape[2])
    ones_row = (lax.broadcasted_iota(jnp.int32, pad_shape, 0) == 0).astype(o_ref.dtype)
    for hh in range(o_ref.shape[0]):
        o_ref[hh, :LANES, :] = y_t[hh * LANES:(hh + 1) * LANES, :].astype(o_ref.dtype)
        o_ref[hh, LANES:, :] = ones_row


def _v_up_t(kvn, w_vt):
    rows, k = kvn.shape
    n_slabs = w_vt.shape[0] // LANES
    tm = _row_tile(rows, 640)
    per_tile = 8
    return pl.pallas_call(
        _v_up_t_kernel, name="v_up_transposed",
        grid=(n_slabs // per_tile, rows // tm),
        in_specs=[pl.BlockSpec((tm, k), lambda n, m: (m, 0)),
                  pl.BlockSpec((per_tile * LANES, k), lambda n, m: (n, 0))],
        out_specs=pl.BlockSpec((per_tile, V_EXT, tm), lambda n, m: (n, 0, m)),
        out_shape=jax.ShapeDtypeStruct((n_slabs, V_EXT, rows), BF16),
        compiler_params=_cparams(2),
    )(kvn, w_vt)


def _kv_up(kvn, w_kv):
    rows, k = kvn.shape
    n_slabs = w_kv.shape[1] // LANES
    tm = _row_tile(rows, 640)
    per_tile = 8
    return pl.pallas_call(
        _kv_up_kernel, name="kv_up",
        grid=(n_slabs // per_tile, rows // tm),
        in_specs=[pl.BlockSpec((tm, k), lambda n, m: (m, 0)),
                  pl.BlockSpec((k, per_tile * LANES), lambda n, m: (0, n))],
        out_specs=pl.BlockSpec((per_tile, tm, LANES), lambda n, m: (n, m, 0)),
        out_shape=jax.ShapeDtypeStruct((n_slabs, rows, LANES), BF16),
        compiler_params=_cparams(2),
    )(kvn, w_kv)


def _q_up_kernel(x_ref, w_ref, cos_ref, sin_ref, o_ref, *, heads_per_tile):
    y = _dot(x_ref[...], w_ref[...]) * (QK_HEAD_DIM ** -0.5 * LOG2E)
    cos_t = cos_ref[...]
    sin_t = sin_ref[...]
    for hh in range(heads_per_tile):
        lo = hh * Q_CAT
        o_ref[:, lo:lo + LANES] = y[:, lo:lo + LANES].astype(o_ref.dtype)
        o_ref[:, lo + LANES:lo + Q_CAT] = _rope(y[:, lo + LANES:lo + Q_CAT], cos_t, sin_t).astype(o_ref.dtype)


def _q_up(qn, w_q, cos_t, sin_t):
    rows, k = qn.shape
    n_cols = w_q.shape[1]
    tm = _row_tile(rows, 640)
    tn = 4 * Q_CAT
    return pl.pallas_call(
        functools.partial(_q_up_kernel, heads_per_tile=tn // Q_CAT), name="q_up_rope",
        grid=(n_cols // tn, rows // tm),
        in_specs=[pl.BlockSpec((tm, k), lambda n, m: (m, 0)), pl.BlockSpec((k, tn), lambda n, m: (0, n)),
                  pl.BlockSpec((tm, LANES), lambda n, m: (m, 0)), pl.BlockSpec((tm, LANES), lambda n, m: (m, 0))],
        out_specs=pl.BlockSpec((tm, tn), lambda n, m: (m, n)),
        out_shape=jax.ShapeDtypeStruct((rows, n_cols), BF16),
        compiler_params=_cparams(2),
    )(qn, w_q, cos_t, sin_t)


ATTN_KEY_BLOCK = 64
ATTN_PART = 2 * LANES


def _attn_kernel(q_ref, kn_ref, vt_ref, kr_ref, o_ref, kcat_ref, m_ref, acc_ref, sa0_ref, sa1_ref, sb0_ref,
                 sb1_ref, spre_ref, p_ref, *, tq, n_real_tiles):
    step_id = pl.program_id(1)
    qi = step_id - 1
    s_real = n_real_tiles * tq
    tk = tq // 2

    @pl.when(step_id == 0)
    def _():
        kcat_ref[:, :LANES] = kn_ref[...]
        kcat_ref[:, LANES:] = kr_ref[...]

    th = ATTN_PART
    parts = tuple(range(0, tq, th))

    def init():
        m_ref[...] = jnp.full(m_ref.shape, -jnp.inf, F32)
        acc_ref[...] = jnp.zeros(acc_ref.shape, F32)

    def update(c0, blocks, pv):
        cols = slice(c0, c0 + th)

        def load(ref, row0, n, visible):
            s = ref[row0:row0 + n, cols]
            return s if visible is None else jnp.where(visible(row0, n), s, -jnp.inf)

        m_prev = m_ref[:, cols]
        m_cur = functools.reduce(
            jnp.maximum, [jnp.max(load(ref, r, n, vis), axis=0, keepdims=True) for ref, r, n, _, vis in blocks])
        m_new = jnp.maximum(m_prev, m_cur)
        alpha = jnp.exp2(m_prev - m_new)
        m_ref[:, cols] = m_new
        for ref, r, n, p_row, vis in blocks:
            for b0 in range(0, n, ATTN_KEY_BLOCK):
                p = jnp.exp2(load(ref, r + b0, ATTN_KEY_BLOCK, vis) - m_new)
                p_ref[p_row + b0:p_row + b0 + ATTN_KEY_BLOCK, cols] = p.astype(p_ref.dtype)
        new = functools.reduce(jnp.add, [_dot(v_t, p_ref[p_row:p_row + n, cols]) for p_row, n, v_t in pv])
        acc_ref[:, cols] = alpha * acc_ref[:, cols] + new

    def finalize():
        acc = acc_ref[...]
        o_ref[...] = (acc[:V_HEAD_DIM, :] / acc[V_HEAD_DIM:V_HEAD_DIM + 1, :]).T.astype(o_ref.dtype)

    def prefix_scores():
        k_pre = kcat_ref[s_real:s_real + BLOCK, :]
        for c0 in parts:
            spre_ref[:, c0:c0 + th] = _dot_nt(k_pre, q_ref[c0:c0 + th, :])

    v_pre_t = vt_ref[:, s_real:s_real + BLOCK]
    pre_row = tk

    def key_ids(row0, n):
        return lax.broadcasted_iota(jnp.int32, (n, th), 0) + row0

    def query_ids(c0, n):
        return lax.broadcasted_iota(jnp.int32, (n, th), 1) + c0

    @pl.when(step_id > 0)
    def _():
        init()
        prefix_scores()

        def raw_scores(dst_ref, j, which=parts):
            k_blk = kcat_ref[pl.ds(pl.multiple_of(j * tk, tk), tk), :]
            for c0 in which:
                dst_ref[:, c0:c0 + th] = _dot_nt(k_blk, q_ref[c0:c0 + th, :])

        def full_update(src_ref, tile, which=parts):
            v_t = vt_ref[:, pl.ds(pl.multiple_of(tile * tk, tk), tk)]
            for c0 in which:
                update(c0, [(src_ref, 0, tk, 0, None)], [(0, tk, v_t)])

        def pair(a, src, dst):
            raw_scores(dst[0], a + 2)
            raw_scores(dst[1], a + 3)
            full_update(src[0], a)
            full_update(src[1], a + 1)

        def diagonal(c0, src_ref, tile):
            q0 = c0 % tk
            n_k = q0 + th
            causal = lambda row0, n: key_ids(row0, n) <= query_ids(q0, n)
            no_pads = lambda row0, n: key_ids(row0, n) >= PAD_LEN
            start = pl.multiple_of(tile * tk, tk)
            update(c0, [(src_ref, 0, n_k, 0, causal), (spre_ref, 0, BLOCK, pre_row, no_pads)],
                   [(0, n_k, vt_ref[:, pl.ds(start, n_k)]), (pre_row, BLOCK, v_pre_t)])

        def own_tiles(src):
            lower = tuple(c0 for c0 in parts if c0 < tk)
            upper = tuple(c0 for c0 in parts if c0 >= tk)
            for c0 in lower:
                diagonal(c0, src[0], 2 * qi)
            full_update(src[0], 2 * qi, upper)
            for c0 in upper:
                diagonal(c0, src[1], 2 * qi + 1)
            finalize()

        buf_a = (sa0_ref, sa1_ref)
        buf_b = (sb0_ref, sb1_ref)
        raw_scores(buf_a[0], 0)
        raw_scores(buf_a[1], 1)

        def body(i, carry):
            pair(4 * i, buf_a, buf_b)
            pair(4 * i + 2, buf_b, buf_a)
            return carry

        lax.fori_loop(0, qi // 2, body, 0)

        @pl.when(qi % 2 == 1)
        def _():
            pair(2 * qi - 2, buf_a, buf_b)
            own_tiles(buf_b)

        @pl.when(qi % 2 == 0)
        def _():
            own_tiles(buf_a)

    @pl.when(step_id == 0)
    def _():
        init()
        prefix_scores()
        for c0 in parts:
            def visible(row0, n, c0=c0):
                key, query = key_ids(row0, n), query_ids(c0, n)
                return (key <= query) & ((key >= PAD_LEN) | (key == query))

            update(c0, [(spre_ref, 0, BLOCK, pre_row, visible)], [(pre_row, BLOCK, v_pre_t)])
        finalize()


def _attention(q_cat, k_slabs, v_t, kr, s_real):
    rows = q_cat.shape[0]
    tq = 1024
    tk = tq // 2
    n_real_tiles = s_real // tq
    q_tile = lambda h, i: ((i + n_real_tiles) % (n_real_tiles + 1), h)
    return pl.pallas_call(
        functools.partial(_attn_kernel, tq=tq, n_real_tiles=n_real_tiles), name="mla_attention",
        grid=(MLA_HEADS, n_real_tiles + 1),
        in_specs=[pl.BlockSpec((tq, Q_CAT), q_tile),
                  pl.BlockSpec((None, rows, LANES), lambda h, i: (h, 0, 0)),
                  pl.BlockSpec((None, V_EXT, rows), lambda h, i: (h, 0, 0)),
                  pl.BlockSpec((rows, LANES), lambda h, i: (0, 0))],
        out_specs=pl.BlockSpec((tq, V_HEAD_DIM), q_tile),
        out_shape=jax.ShapeDtypeStruct((rows, MLA_HEADS * V_HEAD_DIM), BF16),
        scratch_shapes=[pltpu.VMEM((rows, Q_CAT), BF16), pltpu.VMEM((1, tq), F32),
                        pltpu.VMEM((V_EXT, tq), F32)] + [pltpu.VMEM((tk, tq), F32)] * 4
                       + [pltpu.VMEM((BLOCK, tq), F32), pltpu.VMEM((tk + BLOCK, tq), BF16)],
        compiler_params=_cparams(2),
    )(q_cat, k_slabs, v_t, kr)


def _split3(x):
    x1 = x.astype(BF16)
    r1 = x - x1.astype(F32)
    x2 = r1.astype(BF16)
    x3 = (r1 - x2.astype(F32)).astype(BF16)
    return x1, x2, x3


def _block_row(x, size, j):
    c, n = x.shape
    g = x.reshape(c // size, size, n)[:, j:j + 1, :]
    return jnp.broadcast_to(g, (c // size, size, n)).reshape(c, n)


HGRN_LEVELS = 7
HGRN_GROUP = 8


def _hgrn_head(hq, hf, hi, hg, lb_raw, gain, st, valid, row, lev):
    n = BLOCK
    top = jnp.max(lb_raw, axis=0, keepdims=True)
    e = jnp.exp(lb_raw - top)
    lb = e[0:1, :] / jnp.sum(e, axis=0, keepdims=True)

    f = lb + (1.0 - lb) * _sigmoid(hf)
    f_eff = jnp.where(valid, f, 1.0)
    g = jnp.where(valid, jnp.log2(f), 0.0)
    k = jnp.where(valid, 1.0 - f, 0.0)
    q = _silu(hq.astype(F32))

    col = lax.broadcasted_iota(jnp.int32, (n, n), 1)
    tri = (col <= row).astype(BF16)
    g1, g2, g3 = _split3(g)
    b = _dot(tri, g1) + _dot(tri, g2) + _dot(tri, g3)

    q_bf = q.astype(BF16)
    k_bf = k.astype(BF16)
    a = jnp.where(lev == 0, jnp.sum(q * k, axis=1, keepdims=True), 0.0)
    for level in range(1, HGRN_LEVELS + 1):
        size = 1 << level
        if level == 1:
            w = jnp.where((row & 1) == 1, f_eff, 1.0)
        elif level == 2:
            r4 = row & 3
            up1 = pltpu.roll(g, n - 1, 0)
            dn1 = pltpu.roll(g, 1, 0)
            w = jnp.exp2(jnp.where(r4 == 0, up1, jnp.where(r4 == 1, 0.0, jnp.where(r4 == 2, g, g + dn1))))
        else:
            w = jnp.exp2(-jnp.abs(b - _block_row(b, size, size // 2 - 1)))
        w_bf = w.astype(BF16)
        a_l = _dot_nt(q_bf * w_bf, k_bf * w_bf)
        a = jnp.where(lev == level, a_l, a)

    o = _dot(a.astype(BF16), hi) + _dot_nt((q * jnp.exp2(b)).astype(BF16), st.astype(BF16))
    b_last = b[n - 1:n, :]
    st_new = st * jnp.exp2(b_last) + _dot_tn(hi, (k * jnp.exp2(b_last - b)).astype(BF16))
    out = _rms(o, gain) * _silu(hg.astype(F32))
    return out, st_new


def _hgrn_kernel(hq_ref, hf_ref, hi_ref, hg_ref, lb_ref, g_ref, o_ref, st_ref):
    c = pl.program_id(0)
    n = BLOCK

    @pl.when(c == 0)
    def _():
        st_ref[...] = jnp.zeros(st_ref.shape, F32)

    row = lax.broadcasted_iota(jnp.int32, (n, n), 0)
    col = lax.broadcasted_iota(jnp.int32, (n, n), 1)
    valid = (c > 0) | (row >= PAD_LEN)
    x = row ^ col
    lev = functools.reduce(jnp.add, [(x >= (1 << i)).astype(jnp.int32) for i in range(HGRN_LEVELS)])
    lev = jnp.where(col > row, -1, lev)
    gain = g_ref[...]

    def head_group(i, carry):
        for hh in range(HGRN_GROUP):
            head = HGRN_GROUP * i + hh
            cols = pl.ds(pl.multiple_of(head * BLOCK, BLOCK), BLOCK)
            out, st_new = _hgrn_head(hq_ref[:, cols], hf_ref[:, cols], hi_ref[:, cols], hg_ref[:, cols],
                                     lb_ref[:, cols], gain, st_ref[head], valid, row, lev)
            st_ref[head] = st_new
            o_ref[:, cols] = out.astype(o_ref.dtype)
        return carry

    lax.fori_loop(0, HGRN_HEADS // HGRN_GROUP, head_group, 0)


def _hgrn(hq, hf, hrest, lb_raw, g_hgrn):
    rows, width = hq.shape
    n_chunks = rows // BLOCK
    blk = lambda off: pl.BlockSpec((BLOCK, width), lambda c: ((c + n_chunks - 1) % n_chunks, off))
    return pl.pallas_call(
        _hgrn_kernel, name="hgrn2",
        grid=(n_chunks,),
        in_specs=[blk(0), blk(0), blk(0), blk(1),
                  pl.BlockSpec((lb_raw.shape[0], width), lambda c: (0, 0)),
                  pl.BlockSpec((1, HGRN_V_DIM), lambda c: (0, 0))],
        out_specs=blk(0),
        out_shape=jax.ShapeDtypeStruct((rows, width), BF16),
        scratch_shapes=[pltpu.VMEM((HGRN_HEADS, HGRN_V_DIM, HGRN_EXPAND), F32)],
        compiler_params=_cparams(1),
    )(hq, hf, hrest, hrest, lb_raw, g_hgrn)


def _merge_kernel(om_ref, oh_ref, wa_ref, wb_ref, ga_ref, gb_ref, o_ref, wa_bf, wb_bf):
    @pl.when(pl.program_id(1) == 0)
    def _():
        wa_bf[...] = wa_ref[...].astype(BF16)
        wb_bf[...] = wb_ref[...].astype(BF16)

    a = _dot(om_ref[...], wa_bf[...])
    b = _dot(oh_ref[...], wb_bf[...])
    o = _sigmoid(ga_ref[...].astype(F32)) * a + _sigmoid(gb_ref[...].astype(F32)) * b
    o_ref[...] = o.astype(o_ref.dtype)


def _merge(o_mla, o_hgrn, w_a, w_b, hrest, gate_off):
    rows, k = o_mla.shape
    d = w_a.shape[2]
    tm = _row_tile(rows, 640)
    tn = 512
    ga_off = gate_off // tn
    gb_off = (gate_off + d) // tn
    w_spec = pl.BlockSpec((None, k, tn), lambda n, m: (0, 0, n))
    return pl.pallas_call(
        _merge_kernel, name="branch_merge",
        grid=(d // tn, rows // tm),
        in_specs=[pl.BlockSpec((tm, k), lambda n, m: (m, 0)), pl.BlockSpec((tm, k), lambda n, m: (m, 0)),
                  w_spec, w_spec,
                  pl.BlockSpec((tm, tn), lambda n, m: (m, n + ga_off)),
                  pl.BlockSpec((tm, tn), lambda n, m: (m, n + gb_off))],
        out_specs=pl.BlockSpec((tm, tn), lambda n, m: (m, n)),
        out_shape=jax.ShapeDtypeStruct((rows, d), BF16),
        scratch_shapes=[pltpu.VMEM((k, tn), BF16), pltpu.VMEM((k, tn), BF16)],
        compiler_params=_cparams(2),
    )(o_mla, o_hgrn, w_a, w_b, hrest, hrest)


def _out_proj_kernel(mg_ref, w_ref, h_ref, g_ref, h2_ref, u2_ref):
    h2 = h_ref[...] + _dot(mg_ref[...], w_ref[...])
    h2_ref[...] = h2
    u2_ref[...] = _rms(h2, g_ref[...]).astype(u2_ref.dtype)


def _out_proj(merged, w_out, h, g_ffn):
    rows, d = h.shape
    tm = _row_tile(rows, 320)
    row_spec = pl.BlockSpec((tm, d), lambda i: (i, 0))
    return pl.pallas_call(
        _out_proj_kernel, name="mix_out_proj",
        grid=(rows // tm,),
        in_specs=[row_spec, pl.BlockSpec((d, d), lambda i: (0, 0)), row_spec, pl.BlockSpec((1, d), lambda i: (0, 0))],
        out_specs=[row_spec, row_spec],
        out_shape=[jax.ShapeDtypeStruct((rows, d), F32), jax.ShapeDtypeStruct((rows, d), BF16)],
        compiler_params=_cparams(1),
    )(merged, w_out, h, g_ffn)


HALO = 16


def _ffn_in_kernel(u_ref, halo_ref, wg_ref, wu_ref, cw_ref, cb_ref, o_ref, wg_bf, wu_bf, *, n_parts):
    @pl.when(pl.program_id(1) == 0)
    def _():
        wg_bf[...] = wg_ref[...].astype(BF16)
        wu_bf[...] = wu_ref[...].astype(BF16)

    wg = wg_bf[...]
    wu = wu_bf[...]
    cw = cw_ref[...]
    cb = cb_ref[...]
    tp = o_ref.shape[0] // n_parts
    row = lax.broadcasted_iota(jnp.int32, (tp, o_ref.shape[1]), 0)
    tail = _dot(halo_ref[...], wg)[HALO - 2:HALO, :]
    for part in range(n_parts):
        u = u_ref[part * tp:(part + 1) * tp, :]
        gate = _dot(u, wg)
        up = _dot(u, wu)
        back1 = jnp.where(row == 0, tail[1:2, :], pltpu.roll(gate, 1, 0))
        back2 = jnp.where(row == 0, tail[0:1, :], jnp.where(row == 1, tail[1:2, :], pltpu.roll(gate, 2, 0)))
        conv = cw[0:1, :] * back2 + cw[1:2, :] * back1 + cw[2:3, :] * gate + cb
        o_ref[part * tp:(part + 1) * tp, :] = (_silu(conv) * up).astype(o_ref.dtype)
        tail = gate[tp - 2:tp, :]


def _ffn_in(u2, w_fi, conv_w, conv_b, s_real):
    rows, d = u2.shape
    tm = 1024
    tn = 512
    up_off = D_FF // tn
    n_halo = rows // HALO
    halo_map = lambda n, m: ((m * (tm // HALO) + n_halo - 1) % n_halo, 0)
    return pl.pallas_call(
        functools.partial(_ffn_in_kernel, n_parts=2), name="ffn_in_conv_gate",
        grid=(D_FF // tn, s_real // tm),
        in_specs=[pl.BlockSpec((tm, d), lambda n, m: (m, 0)), pl.BlockSpec((HALO, d), halo_map),
                  pl.BlockSpec((None, d, tn), lambda n, m: (0, 0, n)),
                  pl.BlockSpec((None, d, tn), lambda n, m: (0, 0, n + up_off)),
                  pl.BlockSpec((None, conv_w.shape[1], tn), lambda n, m: (0, 0, n)),
                  pl.BlockSpec((1, tn), lambda n, m: (0, n))],
        out_specs=pl.BlockSpec((tm, tn), lambda n, m: (m, n)),
        out_shape=jax.ShapeDtypeStruct((s_real, D_FF), BF16),
        scratch_shapes=[pltpu.VMEM((d, tn), BF16), pltpu.VMEM((d, tn), BF16)],
        compiler_params=_cparams(2),
    )(u2, u2, w_fi, w_fi, conv_w, conv_b)


def _ffn_out_kernel(a_ref, w_ref, h2_ref, g_ref, o_ref):
    kk = pl.program_id(1)

    @pl.when(kk == 0)
    def _():
        o_ref[...] = h2_ref[...]

    o_ref[...] += _dot(a_ref[...], w_ref[...])

    @pl.when(kk == pl.num_programs(1) - 1)
    def _():
        o_ref[...] = _rms(o_ref[...], g_ref[...])


def _ffn_out(act, w_fo, h2, g_final):
    s_real, k = act.shape
    d = w_fo.shape[1]
    tm = 512
    tk = k // 4
    assert tk % LANES == 0
    return pl.pallas_call(
        _ffn_out_kernel, name="ffn_out_final_norm",
        grid=(s_real // tm, k // tk),
        in_specs=[pl.BlockSpec((tm, tk), lambda m, kk: (m, kk)), pl.BlockSpec((tk, d), lambda m, kk: (kk, 0)),
                  pl.BlockSpec((tm, d), lambda m, kk: (m, 0)), pl.BlockSpec((1, d), lambda m, kk: (0, 0))],
        out_specs=pl.BlockSpec((tm, d), lambda m, kk: (m, 0)),
        out_shape=jax.ShapeDtypeStruct((s_real, d), F32),
        compiler_params=_cparams(2),
    )(act, w_fo, h2, g_final)


def kernel(x, positions, meta_tokens, w_in, w_q_up, w_kv_up, w_branch_mla, w_branch_hgrn, w_out, w_ffn_in,
           w_ffn_out, conv_w, conv_b, g_mix_norm, g_q_norm, g_kv_norm, g_hgrn_norm, g_ffn_norm, g_final_norm,
           lb_raw):
    b, s_real, d = x.shape
    assert b == 1 and w_in.shape[0] == 1 and s_real % 1024 == 0
    dt = x.dtype

    h = jnp.concatenate([x[0], jnp.zeros((PAD_LEN, d), dt), meta_tokens.astype(dt)], axis=0)
    pos = jnp.concatenate([positions[0].astype(jnp.int32) + N_META, jnp.zeros((PAD_LEN,), jnp.int32),
                           jnp.arange(N_META, dtype=jnp.int32)])
    inv = 1.0 / (ROPE_THETA ** (jnp.arange(0, QK_ROPE_DIM, 2, dtype=F32) / QK_ROPE_DIM))
    inv = jnp.concatenate([inv, inv, jnp.zeros((LANES - QK_ROPE_DIM,), F32)])[None, :]

    w_in_t = jnp.swapaxes(w_in, 1, 2)[0]
    w_lat = _w_lat(w_in_t)
    w_q = jnp.pad(w_q_up[0].reshape(Q_LORA_RANK, MLA_HEADS, QK_HEAD_DIM),
                  ((0, 0), (0, 0), (0, Q_CAT - QK_HEAD_DIM))).reshape(Q_LORA_RANK, MLA_HEADS * Q_CAT).astype(BF16)
    w_kv3 = w_kv_up[0].reshape(KV_LORA_RANK, MLA_HEADS, QK_NOPE_DIM + V_HEAD_DIM)
    w_k = w_kv3[:, :, :QK_NOPE_DIM].reshape(KV_LORA_RANK, -1).astype(BF16)
    w_vt = w_kv3[:, :, QK_NOPE_DIM:].reshape(KV_LORA_RANK, -1).T.astype(BF16)
    w_o = w_out[0].astype(BF16)
    w_fo = w_ffn_out[0].astype(BF16)

    cos_t, sin_t = _rope_tables(pos.astype(F32)[:, None], inv)

    u = _norm_cast(h, g_mix_norm)
    qn, kvn, kr = _proj_lat(u, w_lat, g_q_norm, g_kv_norm, cos_t, sin_t)
    hq = _matmul_cols_t(u, w_in_t, LAT_END, d, BF16, 1024, "proj_hgrn_q")
    hf = _matmul_cols_t(u, w_in_t, LAT_END + d, d, F32, 1024, "proj_hgrn_forget")
    hrest = _matmul_cols_t(u, w_in_t, LAT_END + 2 * d, 4 * d, BF16, 1024, "proj_hgrn_rest")
    q_cat = _q_up(qn, w_q, cos_t, sin_t)
    k_slabs = _kv_up(kvn, w_k)
    v_t = _v_up_t(kvn, w_vt)
    o_mla = _attention(q_cat, k_slabs, v_t, kr, s_real)
    o_hgrn = _hgrn(hq, hf, hrest, lb_raw, g_hgrn_norm)
    merged = _merge(o_mla, o_hgrn, w_branch_mla, w_branch_hgrn, hrest, 2 * d)
    h2, u2 = _out_proj(merged, w_o, h, g_ffn_norm)

    act = _ffn_in(u2, w_ffn_in, conv_w, conv_b, s_real)
    out = _ffn_out(act, w_fo, h2, g_final_norm[None, :])
    return out[None]
```

```python
import functools

import jax
import jax.numpy as jnp
from jax import lax
from jax.experimental import pallas as pl
from jax.experimental.pallas import tpu as pltpu

F32 = jnp.float32
BF16 = jnp.bfloat16

N_META = 16
BLOCK = 128
PAD_LEN = BLOCK - N_META
MLA_HEADS = 16
Q_LORA_RANK = 1536
KV_LORA_RANK = 512
QK_NOPE_DIM = 128
QK_ROPE_DIM = 64
QK_HEAD_DIM = QK_NOPE_DIM + QK_ROPE_DIM
V_HEAD_DIM = 128
ROPE_THETA = 10000.0
HGRN_HEADS = 16
HGRN_EXPAND = 128
HGRN_V_DIM = 128
D_FF = 5632
NORM_EPS = 1e-6

LANES = 128
Q_CAT = 2 * LANES
LAT_COLS = Q_LORA_RANK + KV_LORA_RANK + LANES
LOG2E = 1.4426950408889634
VMEM_LIMIT = 52 * 1024 * 1024


def _cparams(n_axes):
    return pltpu.CompilerParams(dimension_semantics=("arbitrary",) * n_axes, vmem_limit_bytes=VMEM_LIMIT)


def _row_tile(rows, target):
    for t in range(target, 0, -LANES):
        if rows % t == 0:
            return t
    raise ValueError(f"no 128-multiple row tile for {rows}")


def _dot(a, b):
    return jnp.dot(a, b, preferred_element_type=F32)


def _dot_nt(a, b):
    return lax.dot_general(a, b, (((1,), (1,)), ((), ())), preferred_element_type=F32)


def _dot_tn(a, b):
    return lax.dot_general(a, b, (((0,), (0,)), ((), ())), preferred_element_type=F32)


def _rms(x, g):
    return x * lax.rsqrt(jnp.mean(x * x, axis=-1, keepdims=True) + NORM_EPS) * g


def _sigmoid(x):
    return 0.5 * jnp.tanh(0.5 * x) + 0.5


def _silu(x):
    return x * _sigmoid(x)


def _rope(x, cos_t, sin_t):
    rot = pltpu.roll(x, 32, 1) - pltpu.roll(x, 96, 1)
    return x * cos_t + rot * sin_t


def _rope_table_kernel(pos_ref, inv_ref, cos_ref, sin_ref):
    ang = pos_ref[...] * inv_ref[...]
    keep = lax.broadcasted_iota(jnp.int32, ang.shape, 1) < QK_ROPE_DIM
    cos_ref[...] = jnp.where(keep, jnp.cos(ang), 0.0)
    sin_ref[...] = jnp.where(keep, jnp.sin(ang), 0.0)


def _rope_tables(pos_f, inv):
    rows = pos_f.shape[0]
    tm = _row_tile(rows, 640)
    spec = pl.BlockSpec((tm, LANES), lambda i: (i, 0))
    return pl.pallas_call(
        _rope_table_kernel, name="rope_tables",
        grid=(rows // tm,),
        in_specs=[pl.BlockSpec((tm, 1), lambda i: (i, 0)), pl.BlockSpec((1, LANES), lambda i: (0, 0))],
        out_specs=[spec, spec],
        out_shape=[jax.ShapeDtypeStruct((rows, LANES), F32)] * 2,
        compiler_params=_cparams(1),
    )(pos_f, inv)


LAT_END = Q_LORA_RANK + KV_LORA_RANK + QK_ROPE_DIM


def _w_lat_kernel(w_ref, o_ref):
    row = lax.broadcasted_iota(jnp.int32, w_ref.shape, 0)
    o_ref[...] = jnp.where(row < LAT_END, w_ref[...], 0.0).astype(o_ref.dtype)


def _w_lat(w_in_t):
    d = w_in_t.shape[1]
    tk = 512
    return pl.pallas_call(
        _w_lat_kernel, name="w_latent_cast",
        grid=(d // tk,),
        in_specs=[pl.BlockSpec((LAT_COLS, tk), lambda c: (0, c))],
        out_specs=pl.BlockSpec((LAT_COLS, tk), lambda c: (0, c)),
        out_shape=jax.ShapeDtypeStruct((LAT_COLS, d), BF16),
        compiler_params=_cparams(1),
    )(w_in_t)


def _two_source_rows(n_token_tiles, body, token_ref, prefix_ref):
    i = pl.program_id(0)

    @pl.when(i < n_token_tiles)
    def _():
        body(token_ref[...], slice(0, token_ref.shape[0]))

    @pl.when(i == n_token_tiles)
    def _():
        body(prefix_ref[...], slice(0, BLOCK))


def _proj_lat_kernel(x_ref, pre_ref, gm_ref, w_ref, gq_ref, gkv_ref, cos_ref, sin_ref, u_ref, qn_ref, kvn_ref,
                     kr_ref, *, n_token_tiles):
    kv_end = Q_LORA_RANK + KV_LORA_RANK

    def body(h, rows):
        u = _rms(h, gm_ref[...]).astype(u_ref.dtype)
        u_ref[rows, :] = u
        y = _dot_nt(u, w_ref[...])
        qn_ref[rows, :] = _rms(y[:, :Q_LORA_RANK], gq_ref[...]).astype(qn_ref.dtype)
        kvn_ref[rows, :] = _rms(y[:, Q_LORA_RANK:kv_end], gkv_ref[...]).astype(kvn_ref.dtype)
        kr_ref[rows, :] = _rope(y[:, kv_end:], cos_ref[rows, :], sin_ref[rows, :]).astype(kr_ref.dtype)

    _two_source_rows(n_token_tiles, body, x_ref, pre_ref)


def _proj_lat(x2d, prefix, g_mix, w_lat, g_q, g_kv, cos_t, sin_t):
    s_real, d = x2d.shape
    rows = s_real + BLOCK
    tm = 512
    n_tok = s_real // tm
    row_spec = lambda n: pl.BlockSpec((tm, n), lambda i: (i, 0))
    full_spec = lambda r, n: pl.BlockSpec((r, n), lambda i: (0, 0))
    return pl.pallas_call(
        functools.partial(_proj_lat_kernel, n_token_tiles=n_tok), name="proj_latents",
        grid=(n_tok + 1,),
        in_specs=[pl.BlockSpec((tm, d), lambda i: (jnp.minimum(i, n_tok - 1), 0)), full_spec(BLOCK, d),
                  full_spec(1, d), full_spec(LAT_COLS, d), full_spec(1, Q_LORA_RANK), full_spec(1, KV_LORA_RANK),
                  row_spec(LANES), row_spec(LANES)],
        out_specs=[row_spec(d), row_spec(Q_LORA_RANK), row_spec(KV_LORA_RANK), row_spec(LANES)],
        out_shape=[jax.ShapeDtypeStruct((rows, d), BF16),
                   jax.ShapeDtypeStruct((rows, Q_LORA_RANK), BF16),
                   jax.ShapeDtypeStruct((rows, KV_LORA_RANK), BF16),
                   jax.ShapeDtypeStruct((rows, LANES), BF16)],
        compiler_params=_cparams(1),
    )(x2d, prefix, g_mix, w_lat, g_q, g_kv, cos_t, sin_t)


def _matmul_t_kernel(x_ref, w_ref, o_ref, w_bf):
    @pl.when(pl.program_id(1) == 0)
    def _():
        w_bf[...] = w_ref[...].T.astype(BF16)

    o_ref[...] = _dot(x_ref[...], w_bf[...]).astype(o_ref.dtype)


def _matmul_cols_t(x, w_t, col_start, n_cols, out_dtype, tn, name):
    rows, k = x.shape
    tm = _row_tile(rows, 640)
    assert col_start % 8 == 0 and n_cols % tn == 0
    return pl.pallas_call(
        _matmul_t_kernel, name=name,
        grid=(n_cols // tn, rows // tm),
        in_specs=[pl.BlockSpec((tm, k), lambda n, m: (m, 0)),
                  pl.BlockSpec((pl.Element(tn), pl.Element(k)),
                               lambda n, m: (pl.multiple_of(col_start + n * tn, 8), 0))],
        out_specs=pl.BlockSpec((tm, tn), lambda n, m: (m, n)),
        out_shape=jax.ShapeDtypeStruct((rows, n_cols), out_dtype),
        scratch_shapes=[pltpu.VMEM((k, tn), BF16)],
        compiler_params=_cparams(2),
    )(x, w_t)


def _kv_up_kernel(x_ref, w_ref, o_ref):
    y = _dot(x_ref[...], w_ref[...])
    for hh in range(o_ref.shape[0]):
        o_ref[hh] = y[:, hh * LANES:(hh + 1) * LANES].astype(o_ref.dtype)


V_EXT = V_HEAD_DIM + 16


def _v_up_t_kernel(x_ref, w_ref, o_ref):
    y_t = _dot_nt(w_ref[...], x_ref[...])
    pad_shape = (V_EXT - V_HEAD_DIM, o_ref.shape[2])
    ones_row = (lax.broadcasted_iota(jnp.int32, pad_shape, 0) == 0).astype(o_ref.dtype)
    for hh in range(o_ref.shape[0]):
        o_ref[hh, :LANES, :] = y_t[hh * LANES:(hh + 1) * LANES, :].astype(o_ref.dtype)
        o_ref[hh, LANES:, :] = ones_row


def _v_up_t(kvn, w_vt):
    rows, k = kvn.shape
    n_slabs = w_vt.shape[0] // LANES
    tm = _row_tile(rows, 640)
    per_tile = 8
    return pl.pallas_call(
        _v_up_t_kernel, name="v_up_transposed",
        grid=(n_slabs // per_tile, rows // tm),
        in_specs=[pl.BlockSpec((tm, k), lambda n, m: (m, 0)),
                  pl.BlockSpec((per_tile * LANES, k), lambda n, m: (n, 0))],
        out_specs=pl.BlockSpec((per_tile, V_EXT, tm), lambda n, m: (n, 0, m)),
        out_shape=jax.ShapeDtypeStruct((n_slabs, V_EXT, rows), BF16),
        compiler_params=_cparams(2),
    )(kvn, w_vt)


def _kv_up(kvn, w_kv):
    rows, k = kvn.shape
    n_slabs = w_kv.shape[1] // LANES
    tm = _row_tile(rows, 640)
    per_tile = 8
    return pl.pallas_call(
        _kv_up_kernel, name="kv_up",
        grid=(n_slabs // per_tile, rows // tm),
        in_specs=[pl.BlockSpec((tm, k), lambda n, m: (m, 0)),
                  pl.BlockSpec((k, per_tile * LANES), lambda n, m: (0, n))],
        out_specs=pl.BlockSpec((per_tile, tm, LANES), lambda n, m: (n, m, 0)),
        out_shape=jax.ShapeDtypeStruct((n_slabs, rows, LANES), BF16),
        compiler_params=_cparams(2),
    )(kvn, w_kv)


def _q_up_kernel(x_ref, w_ref, cos_ref, sin_ref, o_ref, w_bf, *, heads_per_tile):
    @pl.when(pl.program_id(1) == 0)
    def _():
        w_bf[...] = jnp.zeros(w_bf.shape, w_bf.dtype)
        for hh in range(heads_per_tile):
            w_bf[:, hh * Q_CAT:hh * Q_CAT + QK_HEAD_DIM] = (
                w_ref[:, hh * QK_HEAD_DIM:(hh + 1) * QK_HEAD_DIM].astype(w_bf.dtype))

    y = _dot(x_ref[...], w_bf[...]) * (QK_HEAD_DIM ** -0.5 * LOG2E)
    cos_t = cos_ref[...]
    sin_t = sin_ref[...]
    for hh in range(heads_per_tile):
        lo = hh * Q_CAT
        o_ref[:, lo:lo + LANES] = y[:, lo:lo + LANES].astype(o_ref.dtype)
        o_ref[:, lo + LANES:lo + Q_CAT] = _rope(y[:, lo + LANES:lo + Q_CAT], cos_t, sin_t).astype(o_ref.dtype)


def _q_up(qn, w_q_up, cos_t, sin_t):
    rows, k = qn.shape
    heads_per_tile = 4
    tm = _row_tile(rows, 640)
    tn = heads_per_tile * Q_CAT
    n_tiles = w_q_up.shape[2] // (heads_per_tile * QK_HEAD_DIM)
    return pl.pallas_call(
        functools.partial(_q_up_kernel, heads_per_tile=heads_per_tile), name="q_up_rope",
        grid=(n_tiles, rows // tm),
        in_specs=[pl.BlockSpec((tm, k), lambda n, m: (m, 0)),
                  pl.BlockSpec((None, k, heads_per_tile * QK_HEAD_DIM), lambda n, m: (0, 0, n)),
                  pl.BlockSpec((tm, LANES), lambda n, m: (m, 0)), pl.BlockSpec((tm, LANES), lambda n, m: (m, 0))],
        out_specs=pl.BlockSpec((tm, tn), lambda n, m: (m, n)),
        out_shape=jax.ShapeDtypeStruct((rows, n_tiles * tn), BF16),
        scratch_shapes=[pltpu.VMEM((k, tn), BF16)],
        compiler_params=_cparams(2),
    )(qn, w_q_up, cos_t, sin_t)


ATTN_KEY_BLOCK = 64
ATTN_PART = 2 * LANES


def _attn_kernel(q_ref, kn_ref, vt_ref, kr_ref, o_ref, kcat_ref, m_ref, acc_ref, sa0_ref, sa1_ref, sb0_ref,
                 sb1_ref, spre_ref, p_ref, *, tq, n_real_tiles):
    step_id = pl.program_id(1)
    qi = step_id - 1
    s_real = n_real_tiles * tq
    tk = tq // 2

    @pl.when(step_id == 0)
    def _():
        kcat_ref[:, :LANES] = kn_ref[...]
        kcat_ref[:, LANES:] = kr_ref[...]

    th = ATTN_PART
    parts = tuple(range(0, tq, th))

    def init():
        m_ref[...] = jnp.full(m_ref.shape, -jnp.inf, F32)
        acc_ref[...] = jnp.zeros(acc_ref.shape, F32)

    def update(c0, blocks, pv):
        cols = slice(c0, c0 + th)

        def load(ref, row0, n, visible):
            s = ref[row0:row0 + n, cols]
            return s if visible is None else jnp.where(visible(row0, n), s, -jnp.inf)

        m_prev = m_ref[:, cols]
        m_cur = functools.reduce(
            jnp.maximum, [jnp.max(load(ref, r, n, vis), axis=0, keepdims=True) for ref, r, n, _, vis in blocks])
        m_new = jnp.maximum(m_prev, m_cur)
        alpha = jnp.exp2(m_prev - m_new)
        m_ref[:, cols] = m_new
        for ref, r, n, p_row, vis in blocks:
            for b0 in range(0, n, ATTN_KEY_BLOCK):
                p = jnp.exp2(load(ref, r + b0, ATTN_KEY_BLOCK, vis) - m_new)
                p_ref[p_row + b0:p_row + b0 + ATTN_KEY_BLOCK, cols] = p.astype(p_ref.dtype)
        new = functools.reduce(jnp.add, [_dot(v_t, p_ref[p_row:p_row + n, cols]) for p_row, n, v_t in pv])
        acc_ref[:, cols] = alpha * acc_ref[:, cols] + new

    def finalize():
        acc = acc_ref[...]
        o_ref[...] = (acc[:V_HEAD_DIM, :] / acc[V_HEAD_DIM:V_HEAD_DIM + 1, :]).T.astype(o_ref.dtype)

    def prefix_scores():
        k_pre = kcat_ref[s_real:s_real + BLOCK, :]
        for c0 in parts:
            spre_ref[:, c0:c0 + th] = _dot_nt(k_pre, q_ref[c0:c0 + th, :])

    v_pre_t = vt_ref[:, s_real:s_real + BLOCK]
    pre_row = tk

    def key_ids(row0, n):
        return lax.broadcasted_iota(jnp.int32, (n, th), 0) + row0

    def query_ids(c0, n):
        return lax.broadcasted_iota(jnp.int32, (n, th), 1) + c0

    @pl.when(step_id > 0)
    def _():
        init()
        prefix_scores()

        def raw_scores(dst_ref, j, which=parts):
            k_blk = kcat_ref[pl.ds(pl.multiple_of(j * tk, tk), tk), :]
            for c0 in which:
                dst_ref[:, c0:c0 + th] = _dot_nt(k_blk, q_ref[c0:c0 + th, :])

        def full_update(src_ref, tile, which=parts):
            v_t = vt_ref[:, pl.ds(pl.multiple_of(tile * tk, tk), tk)]
            for c0 in which:
                update(c0, [(src_ref, 0, tk, 0, None)], [(0, tk, v_t)])

        def pair(a, src, dst):
            raw_scores(dst[0], a + 2)
            raw_scores(dst[1], a + 3)
            full_update(src[0], a)
            full_update(src[1], a + 1)

        def diagonal(c0, src_ref, tile):
            q0 = c0 % tk
            n_k = q0 + th
            causal = lambda row0, n: key_ids(row0, n) <= query_ids(q0, n)
            no_pads = lambda row0, n: key_ids(row0, n) >= PAD_LEN
            start = pl.multiple_of(tile * tk, tk)
            update(c0, [(src_ref, 0, n_k, 0, causal), (spre_ref, 0, BLOCK, pre_row, no_pads)],
                   [(0, n_k, vt_ref[:, pl.ds(start, n_k)]), (pre_row, BLOCK, v_pre_t)])

        def own_tiles(src):
            lower = tuple(c0 for c0 in parts if c0 < tk)
            upper = tuple(c0 for c0 in parts if c0 >= tk)
            for c0 in lower:
                diagonal(c0, src[0], 2 * qi)
            full_update(src[0], 2 * qi, upper)
            for c0 in upper:
                diagonal(c0, src[1], 2 * qi + 1)
            finalize()

        buf_a = (sa0_ref, sa1_ref)
        buf_b = (sb0_ref, sb1_ref)
        raw_scores(buf_a[0], 0)
        raw_scores(buf_a[1], 1)

        def body(i, carry):
            pair(4 * i, buf_a, buf_b)
            pair(4 * i + 2, buf_b, buf_a)
            return carry

        lax.fori_loop(0, qi // 2, body, 0)

        @pl.when(qi % 2 == 1)
        def _():
            pair(2 * qi - 2, buf_a, buf_b)
            own_tiles(buf_b)

        @pl.when(qi % 2 == 0)
        def _():
            own_tiles(buf_a)

    @pl.when(step_id == 0)
    def _():
        init()
        prefix_scores()
        for c0 in parts:
            def visible(row0, n, c0=c0):
                key, query = key_ids(row0, n), query_ids(c0, n)
                return (key <= query) & ((key >= PAD_LEN) | (key == query))

            update(c0, [(spre_ref, 0, BLOCK, pre_row, visible)], [(pre_row, BLOCK, v_pre_t)])
        finalize()


def _attention(q_cat, k_slabs, v_t, kr, s_real):
    rows = q_cat.shape[0]
    tq = 1024
    tk = tq // 2
    n_real_tiles = s_real // tq
    q_tile = lambda h, i: ((i + n_real_tiles) % (n_real_tiles + 1), h)
    return pl.pallas_call(
        functools.partial(_attn_kernel, tq=tq, n_real_tiles=n_real_tiles), name="mla_attention",
        grid=(MLA_HEADS, n_real_tiles + 1),
        in_specs=[pl.BlockSpec((tq, Q_CAT), q_tile),
                  pl.BlockSpec((None, rows, LANES), lambda h, i: (h, 0, 0)),
                  pl.BlockSpec((None, V_EXT, rows), lambda h, i: (h, 0, 0)),
                  pl.BlockSpec((rows, LANES), lambda h, i: (0, 0))],
        out_specs=pl.BlockSpec((tq, V_HEAD_DIM), q_tile),
        out_shape=jax.ShapeDtypeStruct((rows, MLA_HEADS * V_HEAD_DIM), BF16),
        scratch_shapes=[pltpu.VMEM((rows, Q_CAT), BF16), pltpu.VMEM((1, tq), F32),
                        pltpu.VMEM((V_EXT, tq), F32)] + [pltpu.VMEM((tk, tq), F32)] * 4
                       + [pltpu.VMEM((BLOCK, tq), F32), pltpu.VMEM((tk + BLOCK, tq), BF16)],
        compiler_params=_cparams(2),
    )(q_cat, k_slabs, v_t, kr)


def _split3(x):
    x1 = x.astype(BF16)
    r1 = x - x1.astype(F32)
    x2 = r1.astype(BF16)
    x3 = (r1 - x2.astype(F32)).astype(BF16)
    return x1, x2, x3


def _block_row(x, size, j):
    c, n = x.shape
    g = x.reshape(c // size, size, n)[:, j:j + 1, :]
    return jnp.broadcast_to(g, (c // size, size, n)).reshape(c, n)


HGRN_LEVELS = 7
HGRN_GROUP = 8


def _hgrn_head(hq, hf, hi, hg, lb_raw, gain, st, valid, row, lev):
    n = BLOCK
    top = jnp.max(lb_raw, axis=0, keepdims=True)
    e = jnp.exp(lb_raw - top)
    lb = e[0:1, :] / jnp.sum(e, axis=0, keepdims=True)

    f = lb + (1.0 - lb) * _sigmoid(hf)
    f_eff = jnp.where(valid, f, 1.0)
    g = jnp.where(valid, jnp.log2(f), 0.0)
    k = jnp.where(valid, 1.0 - f, 0.0)
    q = _silu(hq.astype(F32))

    col = lax.broadcasted_iota(jnp.int32, (n, n), 1)
    tri = (col <= row).astype(BF16)
    g1, g2, g3 = _split3(g)
    b = _dot(tri, g1) + _dot(tri, g2) + _dot(tri, g3)

    q_bf = q.astype(BF16)
    k_bf = k.astype(BF16)
    a = jnp.where(lev == 0, jnp.sum(q * k, axis=1, keepdims=True), 0.0)
    for level in range(1, HGRN_LEVELS + 1):
        size = 1 << level
        if level == 1:
            w = jnp.where((row & 1) == 1, f_eff, 1.0)
        elif level == 2:
            r4 = row & 3
            up1 = pltpu.roll(g, n - 1, 0)
            dn1 = pltpu.roll(g, 1, 0)
            w = jnp.exp2(jnp.where(r4 == 0, up1, jnp.where(r4 == 1, 0.0, jnp.where(r4 == 2, g, g + dn1))))
        else:
            w = jnp.exp2(-jnp.abs(b - _block_row(b, size, size // 2 - 1)))
        w_bf = w.astype(BF16)
        a_l = _dot_nt(q_bf * w_bf, k_bf * w_bf)
        a = jnp.where(lev == level, a_l, a)

    o = _dot(a.astype(BF16), hi) + _dot_nt((q * jnp.exp2(b)).astype(BF16), st.astype(BF16))
    b_last = b[n - 1:n, :]
    st_new = st * jnp.exp2(b_last) + _dot_tn(hi, (k * jnp.exp2(b_last - b)).astype(BF16))
    out = _rms(o, gain) * _silu(hg.astype(F32))
    return out, st_new


def _hgrn_kernel(hq_ref, hf_ref, hi_ref, hg_ref, lb_ref, g_ref, o_ref, st_ref):
    c = pl.program_id(0)
    n = BLOCK

    @pl.when(c == 0)
    def _():
        st_ref[...] = jnp.zeros(st_ref.shape, F32)

    row = lax.broadcasted_iota(jnp.int32, (n, n), 0)
    col = lax.broadcasted_iota(jnp.int32, (n, n), 1)
    valid = (c > 0) | (row >= PAD_LEN)
    x = row ^ col
    lev = functools.reduce(jnp.add, [(x >= (1 << i)).astype(jnp.int32) for i in range(HGRN_LEVELS)])
    lev = jnp.where(col > row, -1, lev)
    gain = g_ref[...]

    def head_group(i, carry):
        for hh in range(HGRN_GROUP):
            head = HGRN_GROUP * i + hh
            cols = pl.ds(pl.multiple_of(head * BLOCK, BLOCK), BLOCK)
            out, st_new = _hgrn_head(hq_ref[:, cols], hf_ref[:, cols], hi_ref[:, cols], hg_ref[:, cols],
                                     lb_ref[:, cols], gain, st_ref[head], valid, row, lev)
            st_ref[head] = st_new
            o_ref[:, cols] = out.astype(o_ref.dtype)
        return carry

    lax.fori_loop(0, HGRN_HEADS // HGRN_GROUP, head_group, 0)


def _hgrn(hq, hf, hrest, lb_raw, g_hgrn):
    rows, width = hq.shape
    n_chunks = rows // BLOCK
    blk = lambda off: pl.BlockSpec((BLOCK, width), lambda c: ((c + n_chunks - 1) % n_chunks, off))
    return pl.pallas_call(
        _hgrn_kernel, name="hgrn2",
        grid=(n_chunks,),
        in_specs=[blk(0), blk(0), blk(0), blk(1),
                  pl.BlockSpec((lb_raw.shape[0], width), lambda c: (0, 0)),
                  pl.BlockSpec((1, HGRN_V_DIM), lambda c: (0, 0))],
        out_specs=blk(0),
        out_shape=jax.ShapeDtypeStruct((rows, width), BF16),
        scratch_shapes=[pltpu.VMEM((HGRN_HEADS, HGRN_V_DIM, HGRN_EXPAND), F32)],
        compiler_params=_cparams(1),
    )(hq, hf, hrest, hrest, lb_raw, g_hgrn)


def _merge_kernel(om_ref, oh_ref, wa_ref, wb_ref, ga_ref, gb_ref, o_ref, wa_bf, wb_bf):
    @pl.when(pl.program_id(1) == 0)
    def _():
        wa_bf[...] = wa_ref[...].astype(BF16)
        wb_bf[...] = wb_ref[...].astype(BF16)

    a = _dot(om_ref[...], wa_bf[...])
    b = _dot(oh_ref[...], wb_bf[...])
    o = _sigmoid(ga_ref[...].astype(F32)) * a + _sigmoid(gb_ref[...].astype(F32)) * b
    o_ref[...] = o.astype(o_ref.dtype)


def _merge(o_mla, o_hgrn, w_a, w_b, hrest, gate_off):
    rows, k = o_mla.shape
    d = w_a.shape[2]
    tm = _row_tile(rows, 640)
    tn = 512
    ga_off = gate_off // tn
    gb_off = (gate_off + d) // tn
    w_spec = pl.BlockSpec((None, k, tn), lambda n, m: (0, 0, n))
    return pl.pallas_call(
        _merge_kernel, name="branch_merge",
        grid=(d // tn, rows // tm),
        in_specs=[pl.BlockSpec((tm, k), lambda n, m: (m, 0)), pl.BlockSpec((tm, k), lambda n, m: (m, 0)),
                  w_spec, w_spec,
                  pl.BlockSpec((tm, tn), lambda n, m: (m, n + ga_off)),
                  pl.BlockSpec((tm, tn), lambda n, m: (m, n + gb_off))],
        out_specs=pl.BlockSpec((tm, tn), lambda n, m: (m, n)),
        out_shape=jax.ShapeDtypeStruct((rows, d), BF16),
        scratch_shapes=[pltpu.VMEM((k, tn), BF16), pltpu.VMEM((k, tn), BF16)],
        compiler_params=_cparams(2),
    )(o_mla, o_hgrn, w_a, w_b, hrest, hrest)


def _out_proj_kernel(x_ref, pre_ref, mg_ref, w_ref, g_ref, h2_ref, u2_ref, w_bf, *, n_token_tiles):
    @pl.when(pl.program_id(0) == 0)
    def _():
        w_bf[...] = w_ref[...].astype(BF16)

    def body(h, rows):
        h2 = h + _dot(mg_ref[rows, :], w_bf[...])
        h2_ref[rows, :] = h2
        u2_ref[rows, :] = _rms(h2, g_ref[...]).astype(u2_ref.dtype)

    _two_source_rows(n_token_tiles, body, x_ref, pre_ref)


def _out_proj(merged, w_out, x2d, prefix, g_ffn):
    s_real, d = x2d.shape
    rows = s_real + BLOCK
    tm = 256
    n_tok = s_real // tm
    row_spec = pl.BlockSpec((tm, d), lambda i: (i, 0))
    return pl.pallas_call(
        functools.partial(_out_proj_kernel, n_token_tiles=n_tok), name="mix_out_proj",
        grid=(n_tok + 1,),
        in_specs=[pl.BlockSpec((tm, d), lambda i: (jnp.minimum(i, n_tok - 1), 0)),
                  pl.BlockSpec((BLOCK, d), lambda i: (0, 0)), row_spec,
                  pl.BlockSpec((None, d, d), lambda i: (0, 0, 0), pipeline_mode=pl.Buffered(1)),
                  pl.BlockSpec((1, d), lambda i: (0, 0))],
        out_specs=[row_spec, row_spec],
        out_shape=[jax.ShapeDtypeStruct((rows, d), F32), jax.ShapeDtypeStruct((rows, d), BF16)],
        scratch_shapes=[pltpu.VMEM((d, d), BF16)],
        compiler_params=_cparams(1),
    )(x2d, prefix, merged, w_out, g_ffn)


HALO = 16


def _ffn_in_kernel(u_ref, halo_ref, wg_ref, wu_ref, cw_ref, cb_ref, o_ref, wg_bf, wu_bf, *, n_parts):
    @pl.when(pl.program_id(1) == 0)
    def _():
        wg_bf[...] = wg_ref[...].astype(BF16)
        wu_bf[...] = wu_ref[...].astype(BF16)

    wg = wg_bf[...]
    wu = wu_bf[...]
    cw = cw_ref[...]
    cb = cb_ref[...]
    tp = o_ref.shape[0] // n_parts
    row = lax.broadcasted_iota(jnp.int32, (tp, o_ref.shape[1]), 0)
    tail = _dot(halo_ref[...], wg)[HALO - 2:HALO, :]
    for part in range(n_parts):
        u = u_ref[part * tp:(part + 1) * tp, :]
        gate = _dot(u, wg)
        up = _dot(u, wu)
        back1 = jnp.where(row == 0, tail[1:2, :], pltpu.roll(gate, 1, 0))
        back2 = jnp.where(row == 0, tail[0:1, :], jnp.where(row == 1, tail[1:2, :], pltpu.roll(gate, 2, 0)))
        conv = cw[0:1, :] * back2 + cw[1:2, :] * back1 + cw[2:3, :] * gate + cb
        o_ref[part * tp:(part + 1) * tp, :] = (_silu(conv) * up).astype(o_ref.dtype)
        tail = gate[tp - 2:tp, :]


def _ffn_in(u2, w_fi, conv_w, conv_b, s_real):
    rows, d = u2.shape
    tm = 1024
    tn = 512
    up_off = D_FF // tn
    n_halo = rows // HALO
    halo_map = lambda n, m: ((m * (tm // HALO) + n_halo - 1) % n_halo, 0)
    return pl.pallas_call(
        functools.partial(_ffn_in_kernel, n_parts=2), name="ffn_in_conv_gate",
        grid=(D_FF // tn, s_real // tm),
        in_specs=[pl.BlockSpec((tm, d), lambda n, m: (m, 0)), pl.BlockSpec((HALO, d), halo_map),
                  pl.BlockSpec((None, d, tn), lambda n, m: (0, 0, n)),
                  pl.BlockSpec((None, d, tn), lambda n, m: (0, 0, n + up_off)),
                  pl.BlockSpec((None, conv_w.shape[1], tn), lambda n, m: (0, 0, n)),
                  pl.BlockSpec((1, tn), lambda n, m: (0, n))],
        out_specs=pl.BlockSpec((tm, tn), lambda n, m: (m, n)),
        out_shape=jax.ShapeDtypeStruct((s_real, D_FF), BF16),
        scratch_shapes=[pltpu.VMEM((d, tn), BF16), pltpu.VMEM((d, tn), BF16)],
        compiler_params=_cparams(2),
    )(u2, u2, w_fi, w_fi, conv_w, conv_b)


def _ffn_out_kernel(a_ref, w_ref, h2_ref, g_ref, o_ref):
    kk = pl.program_id(1)

    @pl.when(kk == 0)
    def _():
        o_ref[...] = h2_ref[...]

    o_ref[...] += _dot(a_ref[...], w_ref[...])

    @pl.when(kk == pl.num_programs(1) - 1)
    def _():
        o_ref[...] = _rms(o_ref[...], g_ref[...])


def _ffn_out(act, w_fo, h2, g_final):
    s_real, k = act.shape
    d = w_fo.shape[1]
    tm = 512
    tk = k // 4
    assert tk % LANES == 0
    return pl.pallas_call(
        _ffn_out_kernel, name="ffn_out_final_norm",
        grid=(s_real // tm, k // tk),
        in_specs=[pl.BlockSpec((tm, tk), lambda m, kk: (m, kk)), pl.BlockSpec((tk, d), lambda m, kk: (kk, 0)),
                  pl.BlockSpec((tm, d), lambda m, kk: (m, 0)), pl.BlockSpec((1, d), lambda m, kk: (0, 0))],
        out_specs=pl.BlockSpec((tm, d), lambda m, kk: (m, 0)),
        out_shape=jax.ShapeDtypeStruct((s_real, d), F32),
        compiler_params=_cparams(2),
    )(act, w_fo, h2, g_final)


def kernel(x, positions, meta_tokens, w_in, w_q_up, w_kv_up, w_branch_mla, w_branch_hgrn, w_out, w_ffn_in,
           w_ffn_out, conv_w, conv_b, g_mix_norm, g_q_norm, g_kv_norm, g_hgrn_norm, g_ffn_norm, g_final_norm,
           lb_raw):
    b, s_real, d = x.shape
    assert b == 1 and w_in.shape[0] == 1 and s_real % 1024 == 0
    dt = x.dtype

    prefix = jnp.concatenate([jnp.zeros((PAD_LEN, d), dt), meta_tokens.astype(dt)], axis=0)
    pos = jnp.concatenate([positions[0].astype(jnp.int32) + N_META, jnp.zeros((PAD_LEN,), jnp.int32),
                           jnp.arange(N_META, dtype=jnp.int32)])
    inv = 1.0 / (ROPE_THETA ** (jnp.arange(0, QK_ROPE_DIM, 2, dtype=F32) / QK_ROPE_DIM))
    inv = jnp.concatenate([inv, inv, jnp.zeros((LANES - QK_ROPE_DIM,), F32)])[None, :]

    w_in_t = jnp.swapaxes(w_in, 1, 2)[0]
    w_lat = _w_lat(w_in_t)
    w_kv3 = w_kv_up[0].reshape(KV_LORA_RANK, MLA_HEADS, QK_NOPE_DIM + V_HEAD_DIM)
    w_k = w_kv3[:, :, :QK_NOPE_DIM].reshape(KV_LORA_RANK, -1).astype(BF16)
    w_vt = w_kv3[:, :, QK_NOPE_DIM:].reshape(KV_LORA_RANK, -1).T.astype(BF16)
    w_fo = w_ffn_out[0].astype(BF16)

    cos_t, sin_t = _rope_tables(pos.astype(F32)[:, None], inv)

    u, qn, kvn, kr = _proj_lat(x[0], prefix, g_mix_norm, w_lat, g_q_norm, g_kv_norm, cos_t, sin_t)
    hq = _matmul_cols_t(u, w_in_t, LAT_END, d, BF16, 1024, "proj_hgrn_q")
    hf = _matmul_cols_t(u, w_in_t, LAT_END + d, d, F32, 1024, "proj_hgrn_forget")
    hrest = _matmul_cols_t(u, w_in_t, LAT_END + 2 * d, 4 * d, BF16, 1024, "proj_hgrn_rest")
    q_cat = _q_up(qn, w_q_up, cos_t, sin_t)
    k_slabs = _kv_up(kvn, w_k)
    v_t = _v_up_t(kvn, w_vt)
    o_mla = _attention(q_cat, k_slabs, v_t, kr, s_real)
    o_hgrn = _hgrn(hq, hf, hrest, lb_raw, g_hgrn_norm)
    merged = _merge(o_mla, o_hgrn, w_branch_mla, w_branch_hgrn, hrest, 2 * d)
    h2, u2 = _out_proj(merged, w_out, x[0], prefix, g_ffn_norm)

    act = _ffn_in(u2, w_ffn_in, conv_w, conv_b, s_real)
    out = _ffn_out(act, w_fo, h2, g_final_norm[None, :])
    return out[None]
```

```python
import functools

import jax
import jax.numpy as jnp
from jax import lax
from jax.experimental import pallas as pl
from jax.experimental.pallas import tpu as pltpu

F32 = jnp.float32
BF16 = jnp.bfloat16

N_META = 16
BLOCK = 128
PAD_LEN = BLOCK - N_META
MLA_HEADS = 16
Q_LORA_RANK = 1536
KV_LORA_RANK = 512
QK_NOPE_DIM = 128
QK_ROPE_DIM = 64
QK_HEAD_DIM = QK_NOPE_DIM + QK_ROPE_DIM
V_HEAD_DIM = 128
ROPE_THETA = 10000.0
HGRN_HEADS = 16
HGRN_EXPAND = 128
HGRN_V_DIM = 128
D_FF = 5632
NORM_EPS = 1e-6

LANES = 128
Q_CAT = 2 * LANES
LAT_COLS = Q_LORA_RANK + KV_LORA_RANK + LANES
LOG2E = 1.4426950408889634
VMEM_LIMIT = 52 * 1024 * 1024


def _cparams(n_axes):
    return pltpu.CompilerParams(dimension_semantics=("arbitrary",) * n_axes, vmem_limit_bytes=VMEM_LIMIT)


def _row_tile(rows, target):
    for t in range(target, 0, -LANES):
        if rows % t == 0:
            return t
    raise ValueError(f"no 128-multiple row tile for {rows}")


def _dot(a, b):
    return jnp.dot(a, b, preferred_element_type=F32)


def _dot_nt(a, b):
    return lax.dot_general(a, b, (((1,), (1,)), ((), ())), preferred_element_type=F32)


def _dot_tn(a, b):
    return lax.dot_general(a, b, (((0,), (0,)), ((), ())), preferred_element_type=F32)


def _rms(x, g):
    return x * lax.rsqrt(jnp.mean(x * x, axis=-1, keepdims=True) + NORM_EPS) * g


def _sigmoid(x):
    return 0.5 * jnp.tanh(0.5 * x) + 0.5


def _silu(x):
    return x * _sigmoid(x)


def _rope(x, cos_t, sin_t):
    rot = pltpu.roll(x, 32, 1) - pltpu.roll(x, 96, 1)
    return x * cos_t + rot * sin_t


def _rope_table_kernel(pos_ref, inv_ref, cos_ref, sin_ref):
    ang = pos_ref[...] * inv_ref[...]
    keep = lax.broadcasted_iota(jnp.int32, ang.shape, 1) < QK_ROPE_DIM
    cos_ref[...] = jnp.where(keep, jnp.cos(ang), 0.0)
    sin_ref[...] = jnp.where(keep, jnp.sin(ang), 0.0)


def _rope_tables(pos_f, inv):
    rows = pos_f.shape[0]
    tm = _row_tile(rows, 640)
    spec = pl.BlockSpec((tm, LANES), lambda i: (i, 0))
    return pl.pallas_call(
        _rope_table_kernel, name="rope_tables",
        grid=(rows // tm,),
        in_specs=[pl.BlockSpec((tm, 1), lambda i: (i, 0)), pl.BlockSpec((1, LANES), lambda i: (0, 0))],
        out_specs=[spec, spec],
        out_shape=[jax.ShapeDtypeStruct((rows, LANES), F32)] * 2,
        compiler_params=_cparams(1),
    )(pos_f, inv)


LAT_END = Q_LORA_RANK + KV_LORA_RANK + QK_ROPE_DIM


def _w_lat_kernel(w_ref, o_ref):
    row = lax.broadcasted_iota(jnp.int32, w_ref.shape, 0)
    o_ref[...] = jnp.where(row < LAT_END, w_ref[...], 0.0).astype(o_ref.dtype)


def _w_lat(w_in_t):
    d = w_in_t.shape[1]
    tk = 512
    return pl.pallas_call(
        _w_lat_kernel, name="w_latent_cast",
        grid=(d // tk,),
        in_specs=[pl.BlockSpec((LAT_COLS, tk), lambda c: (0, c))],
        out_specs=pl.BlockSpec((LAT_COLS, tk), lambda c: (0, c)),
        out_shape=jax.ShapeDtypeStruct((LAT_COLS, d), BF16),
        compiler_params=_cparams(1),
    )(w_in_t)


def _two_source_rows(n_token_tiles, body, token_ref, prefix_ref):
    i = pl.program_id(0)

    @pl.when(i < n_token_tiles)
    def _():
        body(token_ref[...], slice(0, token_ref.shape[0]))

    @pl.when(i == n_token_tiles)
    def _():
        body(prefix_ref[...], slice(0, BLOCK))


def _proj_lat_kernel(x_ref, pre_ref, gm_ref, w_ref, gq_ref, gkv_ref, cos_ref, sin_ref, u_ref, qn_ref, kvn_ref,
                     kr_ref, *, n_token_tiles):
    kv_end = Q_LORA_RANK + KV_LORA_RANK

    def body(h, rows):
        u = _rms(h, gm_ref[...]).astype(u_ref.dtype)
        u_ref[rows, :] = u
        y = _dot_nt(u, w_ref[...])
        qn_ref[rows, :] = _rms(y[:, :Q_LORA_RANK], gq_ref[...]).astype(qn_ref.dtype)
        kvn_ref[rows, :] = _rms(y[:, Q_LORA_RANK:kv_end], gkv_ref[...]).astype(kvn_ref.dtype)
        kr_ref[rows, :] = _rope(y[:, kv_end:], cos_ref[rows, :], sin_ref[rows, :]).astype(kr_ref.dtype)

    _two_source_rows(n_token_tiles, body, x_ref, pre_ref)


def _proj_lat(x2d, prefix, g_mix, w_lat, g_q, g_kv, cos_t, sin_t):
    s_real, d = x2d.shape
    rows = s_real + BLOCK
    tm = 512
    n_tok = s_real // tm
    row_spec = lambda n: pl.BlockSpec((tm, n), lambda i: (i, 0))
    full_spec = lambda r, n: pl.BlockSpec((r, n), lambda i: (0, 0))
    return pl.pallas_call(
        functools.partial(_proj_lat_kernel, n_token_tiles=n_tok), name="proj_latents",
        grid=(n_tok + 1,),
        in_specs=[pl.BlockSpec((tm, d), lambda i: (jnp.minimum(i, n_tok - 1), 0)), full_spec(BLOCK, d),
                  full_spec(1, d), full_spec(LAT_COLS, d), full_spec(1, Q_LORA_RANK), full_spec(1, KV_LORA_RANK),
                  row_spec(LANES), row_spec(LANES)],
        out_specs=[row_spec(d), row_spec(Q_LORA_RANK), row_spec(KV_LORA_RANK), row_spec(LANES)],
        out_shape=[jax.ShapeDtypeStruct((rows, d), BF16),
                   jax.ShapeDtypeStruct((rows, Q_LORA_RANK), BF16),
                   jax.ShapeDtypeStruct((rows, KV_LORA_RANK), BF16),
                   jax.ShapeDtypeStruct((rows, LANES), BF16)],
        compiler_params=_cparams(1),
    )(x2d, prefix, g_mix, w_lat, g_q, g_kv, cos_t, sin_t)


def _matmul_t_kernel(x_ref, w_ref, o_ref, w_bf):
    @pl.when(pl.program_id(1) == 0)
    def _():
        w_bf[...] = w_ref[...].T.astype(BF16)

    o_ref[...] = _dot(x_ref[...], w_bf[...]).astype(o_ref.dtype)


def _matmul_cols_t(x, w_t, col_start, n_cols, out_dtype, tn, name):
    rows, k = x.shape
    tm = _row_tile(rows, 640)
    assert col_start % 8 == 0 and n_cols % tn == 0
    return pl.pallas_call(
        _matmul_t_kernel, name=name,
        grid=(n_cols // tn, rows // tm),
        in_specs=[pl.BlockSpec((tm, k), lambda n, m: (m, 0)),
                  pl.BlockSpec((pl.Element(tn), pl.Element(k)),
                               lambda n, m: (pl.multiple_of(col_start + n * tn, 8), 0))],
        out_specs=pl.BlockSpec((tm, tn), lambda n, m: (m, n)),
        out_shape=jax.ShapeDtypeStruct((rows, n_cols), out_dtype),
        scratch_shapes=[pltpu.VMEM((k, tn), BF16)],
        compiler_params=_cparams(2),
    )(x, w_t)


def _kv_up_kernel(x_ref, w_ref, o_ref):
    y = _dot(x_ref[...], w_ref[...])
    for hh in range(o_ref.shape[0]):
        o_ref[hh] = y[:, hh * LANES:(hh + 1) * LANES].astype(o_ref.dtype)


V_EXT = V_HEAD_DIM + 16


def _v_up_t_kernel(x_ref, w_ref, o_ref):
    y_t = _dot_nt(w_ref[...], x_ref[...])
    pad_shape = (V_EXT - V_HEAD_DIM, o_ref.shape[2])
    ones_row = (lax.broadcasted_iota(jnp.int32, pad_shape, 0) == 0).astype(o_ref.dtype)
    for hh in range(o_ref.shape[0]):
        o_ref[hh, :LANES, :] = y_t[hh * LANES:(hh + 1) * LANES, :].astype(o_ref.dtype)
        o_ref[hh, LANES:, :] = ones_row


def _v_up_t(kvn, w_vt):
    rows, k = kvn.shape
    n_slabs = w_vt.shape[0] // LANES
    tm = _row_tile(rows, 640)
    per_tile = 8
    return pl.pallas_call(
        _v_up_t_kernel, name="v_up_transposed",
        grid=(n_slabs // per_tile, rows // tm),
        in_specs=[pl.BlockSpec((tm, k), lambda n, m: (m, 0)),
                  pl.BlockSpec((per_tile * LANES, k), lambda n, m: (n, 0))],
        out_specs=pl.BlockSpec((per_tile, V_EXT, tm), lambda n, m: (n, 0, m)),
        out_shape=jax.ShapeDtypeStruct((n_slabs, V_EXT, rows), BF16),
        compiler_params=_cparams(2),
    )(kvn, w_vt)


def _kv_up(kvn, w_kv):
    rows, k = kvn.shape
    n_slabs = w_kv.shape[1] // LANES
    tm = _row_tile(rows, 640)
    per_tile = 8
    return pl.pallas_call(
        _kv_up_kernel, name="kv_up",
        grid=(n_slabs // per_tile, rows // tm),
        in_specs=[pl.BlockSpec((tm, k), lambda n, m: (m, 0)),
                  pl.BlockSpec((k, per_tile * LANES), lambda n, m: (0, n))],
        out_specs=pl.BlockSpec((per_tile, tm, LANES), lambda n, m: (n, m, 0)),
        out_shape=jax.ShapeDtypeStruct((n_slabs, rows, LANES), BF16),
        compiler_params=_cparams(2),
    )(kvn, w_kv)


def _q_up_kernel(x_ref, w_ref, cos_ref, sin_ref, o_ref, w_bf, *, heads_per_tile):
    @pl.when(pl.program_id(1) == 0)
    def _():
        w_bf[...] = jnp.zeros(w_bf.shape, w_bf.dtype)
        for hh in range(heads_per_tile):
            w_bf[:, hh * Q_CAT:hh * Q_CAT + QK_HEAD_DIM] = (
                w_ref[:, hh * QK_HEAD_DIM:(hh + 1) * QK_HEAD_DIM].astype(w_bf.dtype))

    y = _dot(x_ref[...], w_bf[...]) * (QK_HEAD_DIM ** -0.5 * LOG2E)
    cos_t = cos_ref[...]
    sin_t = sin_ref[...]
    for hh in range(heads_per_tile):
        lo = hh * Q_CAT
        o_ref[:, lo:lo + LANES] = y[:, lo:lo + LANES].astype(o_ref.dtype)
        o_ref[:, lo + LANES:lo + Q_CAT] = _rope(y[:, lo + LANES:lo + Q_CAT], cos_t, sin_t).astype(o_ref.dtype)


def _q_up(qn, w_q_up, cos_t, sin_t):
    rows, k = qn.shape
    heads_per_tile = 4
    tm = _row_tile(rows, 640)
    tn = heads_per_tile * Q_CAT
    n_tiles = w_q_up.shape[2] // (heads_per_tile * QK_HEAD_DIM)
    return pl.pallas_call(
        functools.partial(_q_up_kernel, heads_per_tile=heads_per_tile), name="q_up_rope",
        grid=(n_tiles, rows // tm),
        in_specs=[pl.BlockSpec((tm, k), lambda n, m: (m, 0)),
                  pl.BlockSpec((None, k, heads_per_tile * QK_HEAD_DIM), lambda n, m: (0, 0, n)),
                  pl.BlockSpec((tm, LANES), lambda n, m: (m, 0)), pl.BlockSpec((tm, LANES), lambda n, m: (m, 0))],
        out_specs=pl.BlockSpec((tm, tn), lambda n, m: (m, n)),
        out_shape=jax.ShapeDtypeStruct((rows, n_tiles * tn), BF16),
        scratch_shapes=[pltpu.VMEM((k, tn), BF16)],
        compiler_params=_cparams(2),
    )(qn, w_q_up, cos_t, sin_t)


ATTN_KEY_BLOCK = 64
ATTN_PART = 2 * LANES


def _attn_kernel(q_ref, kn_ref, vt_ref, kr_ref, o_ref, kcat_ref, m_ref, acc_ref, sa0_ref, sa1_ref, sb0_ref,
                 sb1_ref, spre_ref, p_ref, *, tq, n_real_tiles):
    step_id = pl.program_id(1)
    qi = step_id - 1
    s_real = n_real_tiles * tq
    tk = tq // 2

    @pl.when(step_id == 0)
    def _():
        kcat_ref[:, :LANES] = kn_ref[...]
        kcat_ref[:, LANES:] = kr_ref[...]

    th = ATTN_PART
    parts = tuple(range(0, tq, th))

    def init():
        m_ref[...] = jnp.full(m_ref.shape, -jnp.inf, F32)
        acc_ref[...] = jnp.zeros(acc_ref.shape, F32)

    def update(c0, blocks, pv):
        cols = slice(c0, c0 + th)

        def load(ref, row0, n, visible):
            s = ref[row0:row0 + n, cols]
            return s if visible is None else jnp.where(visible(row0, n), s, -jnp.inf)

        m_prev = m_ref[:, cols]
        m_cur = functools.reduce(
            jnp.maximum, [jnp.max(load(ref, r, n, vis), axis=0, keepdims=True) for ref, r, n, _, vis in blocks])
        m_new = jnp.maximum(m_prev, m_cur)
        alpha = jnp.exp2(m_prev - m_new)
        m_ref[:, cols] = m_new
        for ref, r, n, p_row, vis in blocks:
            for b0 in range(0, n, ATTN_KEY_BLOCK):
                p = jnp.exp2(load(ref, r + b0, ATTN_KEY_BLOCK, vis) - m_new)
                p_ref[p_row + b0:p_row + b0 + ATTN_KEY_BLOCK, cols] = p.astype(p_ref.dtype)
        new = functools.reduce(jnp.add, [_dot(v_t, p_ref[p_row:p_row + n, cols]) for p_row, n, v_t in pv])
        acc_ref[:, cols] = alpha * acc_ref[:, cols] + new

    def finalize():
        acc = acc_ref[...]
        o_ref[...] = (acc[:V_HEAD_DIM, :] / acc[V_HEAD_DIM:V_HEAD_DIM + 1, :]).T.astype(o_ref.dtype)

    def prefix_scores():
        k_pre = kcat_ref[s_real:s_real + BLOCK, :]
        for c0 in parts:
            spre_ref[:, c0:c0 + th] = _dot_nt(k_pre, q_ref[c0:c0 + th, :])

    v_pre_t = vt_ref[:, s_real:s_real + BLOCK]
    pre_row = tk

    def key_ids(row0, n):
        return lax.broadcasted_iota(jnp.int32, (n, th), 0) + row0

    def query_ids(c0, n):
        return lax.broadcasted_iota(jnp.int32, (n, th), 1) + c0

    @pl.when(step_id > 0)
    def _():
        init()
        prefix_scores()

        def raw_scores(dst_ref, j, which=parts):
            k_blk = kcat_ref[pl.ds(pl.multiple_of(j * tk, tk), tk), :]
            for c0 in which:
                dst_ref[:, c0:c0 + th] = _dot_nt(k_blk, q_ref[c0:c0 + th, :])

        def full_update(src_ref, tile, which=parts):
            v_t = vt_ref[:, pl.ds(pl.multiple_of(tile * tk, tk), tk)]
            for c0 in which:
                update(c0, [(src_ref, 0, tk, 0, None)], [(0, tk, v_t)])

        def pair(a, src, dst):
            raw_scores(dst[0], a + 2)
            raw_scores(dst[1], a + 3)
            full_update(src[0], a)
            full_update(src[1], a + 1)

        def diagonal(c0, src_ref, tile):
            q0 = c0 % tk
            n_k = q0 + th
            causal = lambda row0, n: key_ids(row0, n) <= query_ids(q0, n)
            no_pads = lambda row0, n: key_ids(row0, n) >= PAD_LEN
            start = pl.multiple_of(tile * tk, tk)
            update(c0, [(src_ref, 0, n_k, 0, causal), (spre_ref, 0, BLOCK, pre_row, no_pads)],
                   [(0, n_k, vt_ref[:, pl.ds(start, n_k)]), (pre_row, BLOCK, v_pre_t)])

        def own_tiles(src):
            lower = tuple(c0 for c0 in parts if c0 < tk)
            upper = tuple(c0 for c0 in parts if c0 >= tk)
            for c0 in lower:
                diagonal(c0, src[0], 2 * qi)
            full_update(src[0], 2 * qi, upper)
            for c0 in upper:
                diagonal(c0, src[1], 2 * qi + 1)
            finalize()

        buf_a = (sa0_ref, sa1_ref)
        buf_b = (sb0_ref, sb1_ref)
        raw_scores(buf_a[0], 0)
        raw_scores(buf_a[1], 1)

        def body(i, carry):
            pair(4 * i, buf_a, buf_b)
            pair(4 * i + 2, buf_b, buf_a)
            return carry

        lax.fori_loop(0, qi // 2, body, 0)

        @pl.when(qi % 2 == 1)
        def _():
            pair(2 * qi - 2, buf_a, buf_b)
            own_tiles(buf_b)

        @pl.when(qi % 2 == 0)
        def _():
            own_tiles(buf_a)

    @pl.when(step_id == 0)
    def _():
        init()
        prefix_scores()
        for c0 in parts:
            def visible(row0, n, c0=c0):
                key, query = key_ids(row0, n), query_ids(c0, n)
                return (key <= query) & ((key >= PAD_LEN) | (key == query))

            update(c0, [(spre_ref, 0, BLOCK, pre_row, visible)], [(pre_row, BLOCK, v_pre_t)])
        finalize()


def _attention(q_cat, k_slabs, v_t, kr, s_real):
    rows = q_cat.shape[0]
    tq = 1024
    tk = tq // 2
    n_real_tiles = s_real // tq
    q_tile = lambda h, i: ((i + n_real_tiles) % (n_real_tiles + 1), h)
    return pl.pallas_call(
        functools.partial(_attn_kernel, tq=tq, n_real_tiles=n_real_tiles), name="mla_attention",
        grid=(MLA_HEADS, n_real_tiles + 1),
        in_specs=[pl.BlockSpec((tq, Q_CAT), q_tile),
                  pl.BlockSpec((None, rows, LANES), lambda h, i: (h, 0, 0)),
                  pl.BlockSpec((None, V_EXT, rows), lambda h, i: (h, 0, 0)),
                  pl.BlockSpec((rows, LANES), lambda h, i: (0, 0))],
        out_specs=pl.BlockSpec((tq, V_HEAD_DIM), q_tile),
        out_shape=jax.ShapeDtypeStruct((rows, MLA_HEADS * V_HEAD_DIM), BF16),
        scratch_shapes=[pltpu.VMEM((rows, Q_CAT), BF16), pltpu.VMEM((1, tq), F32),
                        pltpu.VMEM((V_EXT, tq), F32)] + [pltpu.VMEM((tk, tq), F32)] * 4
                       + [pltpu.VMEM((BLOCK, tq), F32), pltpu.VMEM((tk + BLOCK, tq), BF16)],
        compiler_params=_cparams(2),
    )(q_cat, k_slabs, v_t, kr)


def _split3(x):
    x1 = x.astype(BF16)
    r1 = x - x1.astype(F32)
    x2 = r1.astype(BF16)
    x3 = (r1 - x2.astype(F32)).astype(BF16)
    return x1, x2, x3


def _block_row(x, size, j):
    c, n = x.shape
    g = x.reshape(c // size, size, n)[:, j:j + 1, :]
    return jnp.broadcast_to(g, (c // size, size, n)).reshape(c, n)


HGRN_LEVELS = 7
HGRN_GROUP = 16


def _hgrn_head(hq, hf, hi, hg, lb_raw, gain, st, valid, row, lev):
    n = BLOCK
    top = jnp.max(lb_raw, axis=0, keepdims=True)
    e = jnp.exp(lb_raw - top)
    lb = e[0:1, :] / jnp.sum(e, axis=0, keepdims=True)

    f = lb + (1.0 - lb) * _sigmoid(hf)
    f_eff = jnp.where(valid, f, 1.0)
    g = jnp.where(valid, jnp.log2(f), 0.0)
    k = jnp.where(valid, 1.0 - f, 0.0)
    q = _silu(hq.astype(F32))

    col = lax.broadcasted_iota(jnp.int32, (n, n), 1)
    tri = (col <= row).astype(BF16)
    g1, g2, g3 = _split3(g)
    b = _dot(tri, g1) + _dot(tri, g2) + _dot(tri, g3)

    q_bf = q.astype(BF16)
    k_bf = k.astype(BF16)
    a = jnp.where(lev == 0, jnp.sum(q * k, axis=1, keepdims=True), 0.0)
    for level in range(1, HGRN_LEVELS + 1):
        size = 1 << level
        if level == 1:
            w = jnp.where((row & 1) == 1, f_eff, 1.0)
        elif level == 2:
            r4 = row & 3
            up1 = pltpu.roll(g, n - 1, 0)
            dn1 = pltpu.roll(g, 1, 0)
            w = jnp.exp2(jnp.where(r4 == 0, up1, jnp.where(r4 == 1, 0.0, jnp.where(r4 == 2, g, g + dn1))))
        else:
            w = jnp.exp2(-jnp.abs(b - _block_row(b, size, size // 2 - 1)))
        w_bf = w.astype(BF16)
        a_l = _dot_nt(q_bf * w_bf, k_bf * w_bf)
        a = jnp.where(lev == level, a_l, a)

    o = _dot(a.astype(BF16), hi) + _dot_nt((q * jnp.exp2(b)).astype(BF16), st.astype(BF16))
    b_last = b[n - 1:n, :]
    st_new = st * jnp.exp2(b_last) + _dot_tn(hi, (k * jnp.exp2(b_last - b)).astype(BF16))
    out = _rms(o, gain) * _silu(hg.astype(F32))
    return out, st_new


def _hgrn_kernel(hq_ref, hf_ref, hi_ref, hg_ref, lb_ref, g_ref, o_ref, st_ref):
    c = pl.program_id(0)
    n = BLOCK

    @pl.when(c == 0)
    def _():
        st_ref[...] = jnp.zeros(st_ref.shape, F32)

    row = lax.broadcasted_iota(jnp.int32, (n, n), 0)
    col = lax.broadcasted_iota(jnp.int32, (n, n), 1)
    valid = (c > 0) | (row >= PAD_LEN)
    x = row ^ col
    lev = functools.reduce(jnp.add, [(x >= (1 << i)).astype(jnp.int32) for i in range(HGRN_LEVELS)])
    lev = jnp.where(col > row, -1, lev)
    gain = g_ref[...]

    def head_group(i, carry):
        for hh in range(HGRN_GROUP):
            head = HGRN_GROUP * i + hh
            cols = pl.ds(pl.multiple_of(head * BLOCK, BLOCK), BLOCK)
            out, st_new = _hgrn_head(hq_ref[:, cols], hf_ref[:, cols], hi_ref[:, cols], hg_ref[:, cols],
                                     lb_ref[:, cols], gain, st_ref[head], valid, row, lev)
            st_ref[head] = st_new
            o_ref[:, cols] = out.astype(o_ref.dtype)
        return carry

    lax.fori_loop(0, HGRN_HEADS // HGRN_GROUP, head_group, 0)


def _hgrn(hq, hf, hrest, lb_raw, g_hgrn):
    rows, width = hq.shape
    n_chunks = rows // BLOCK
    blk = lambda off: pl.BlockSpec((BLOCK, width), lambda c: ((c + n_chunks - 1) % n_chunks, off))
    return pl.pallas_call(
        _hgrn_kernel, name="hgrn2",
        grid=(n_chunks,),
        in_specs=[blk(0), blk(0), blk(0), blk(1),
                  pl.BlockSpec((lb_raw.shape[0], width), lambda c: (0, 0)),
                  pl.BlockSpec((1, HGRN_V_DIM), lambda c: (0, 0))],
        out_specs=blk(0),
        out_shape=jax.ShapeDtypeStruct((rows, width), BF16),
        scratch_shapes=[pltpu.VMEM((HGRN_HEADS, HGRN_V_DIM, HGRN_EXPAND), F32)],
        compiler_params=_cparams(1),
    )(hq, hf, hrest, hrest, lb_raw, g_hgrn)


def _merge_kernel(om_ref, oh_ref, wa_ref, wb_ref, ga_ref, gb_ref, o_ref, wa_bf, wb_bf):
    @pl.when(pl.program_id(1) == 0)
    def _():
        wa_bf[...] = wa_ref[...].astype(BF16)
        wb_bf[...] = wb_ref[...].astype(BF16)

    tm = o_ref.shape[0]
    small = tm // 5
    r0 = 0
    for tp in (tm - small, small):
        rows = slice(r0, r0 + tp)
        a = _dot(om_ref[rows, :], wa_bf[...])
        b = _dot(oh_ref[rows, :], wb_bf[...])
        o = _sigmoid(ga_ref[rows, :].astype(F32)) * a + _sigmoid(gb_ref[rows, :].astype(F32)) * b
        o_ref[rows, :] = o.astype(o_ref.dtype)
        r0 += tp


def _merge(o_mla, o_hgrn, w_a, w_b, hrest, gate_off):
    rows, k = o_mla.shape
    d = w_a.shape[2]
    tm = _row_tile(rows, 640)
    tn = 512
    ga_off = gate_off // tn
    gb_off = (gate_off + d) // tn
    w_spec = pl.BlockSpec((None, k, tn), lambda n, m: (0, 0, n))
    return pl.pallas_call(
        _merge_kernel, name="branch_merge",
        grid=(d // tn, rows // tm),
        in_specs=[pl.BlockSpec((tm, k), lambda n, m: (m, 0)), pl.BlockSpec((tm, k), lambda n, m: (m, 0)),
                  w_spec, w_spec,
                  pl.BlockSpec((tm, tn), lambda n, m: (m, n + ga_off)),
                  pl.BlockSpec((tm, tn), lambda n, m: (m, n + gb_off))],
        out_specs=pl.BlockSpec((tm, tn), lambda n, m: (m, n)),
        out_shape=jax.ShapeDtypeStruct((rows, d), BF16),
        scratch_shapes=[pltpu.VMEM((k, tn), BF16), pltpu.VMEM((k, tn), BF16)],
        compiler_params=_cparams(2),
    )(o_mla, o_hgrn, w_a, w_b, hrest, hrest)


def _out_proj_kernel(x_ref, pre_ref, mg_ref, w_ref, g_ref, h2_ref, u2_ref, w_bf, *, n_token_tiles):
    @pl.when(pl.program_id(0) == 0)
    def _():
        w_bf[...] = w_ref[...].astype(BF16)

    def body(h, rows):
        h2 = h + _dot(mg_ref[rows, :], w_bf[...])
        h2_ref[rows, :] = h2
        u2_ref[rows, :] = _rms(h2, g_ref[...]).astype(u2_ref.dtype)

    _two_source_rows(n_token_tiles, body, x_ref, pre_ref)


def _out_proj(merged, w_out, x2d, prefix, g_ffn):
    s_real, d = x2d.shape
    rows = s_real + BLOCK
    tm = 256
    n_tok = s_real // tm
    row_spec = pl.BlockSpec((tm, d), lambda i: (i, 0))
    return pl.pallas_call(
        functools.partial(_out_proj_kernel, n_token_tiles=n_tok), name="mix_out_proj",
        grid=(n_tok + 1,),
        in_specs=[pl.BlockSpec((tm, d), lambda i: (jnp.minimum(i, n_tok - 1), 0)),
                  pl.BlockSpec((BLOCK, d), lambda i: (0, 0)), row_spec,
                  pl.BlockSpec((None, d, d), lambda i: (0, 0, 0), pipeline_mode=pl.Buffered(1)),
                  pl.BlockSpec((1, d), lambda i: (0, 0))],
        out_specs=[row_spec, row_spec],
        out_shape=[jax.ShapeDtypeStruct((rows, d), F32), jax.ShapeDtypeStruct((rows, d), BF16)],
        scratch_shapes=[pltpu.VMEM((d, d), BF16)],
        compiler_params=_cparams(1),
    )(x2d, prefix, merged, w_out, g_ffn)


HALO = 16


def _ffn_in_kernel(u_ref, halo_ref, wg_ref, wu_ref, cw_ref, cb_ref, o_ref, wg_bf, wu_bf, tail_ref, *, part_rows):
    @pl.when(pl.program_id(1) == 0)
    def _():
        wg_bf[...] = wg_ref[...].astype(BF16)
        wu_bf[...] = wu_ref[...].astype(BF16)
        tail_ref[...] = _dot(halo_ref[...], wg_bf[...])[HALO - 8:HALO, :]

    wg = wg_bf[...]
    wu = wu_bf[...]
    cw = cw_ref[...]
    cb = cb_ref[...]
    tail = tail_ref[6:8, :]
    r0 = 0
    for tp in part_rows:
        row = lax.broadcasted_iota(jnp.int32, (tp, o_ref.shape[1]), 0)
        u = u_ref[r0:r0 + tp, :]
        gate = _dot(u, wg)
        up = _dot(u, wu)
        back1 = jnp.where(row == 0, tail[1:2, :], pltpu.roll(gate, 1, 0))
        back2 = jnp.where(row == 0, tail[0:1, :], jnp.where(row == 1, tail[1:2, :], pltpu.roll(gate, 2, 0)))
        conv = cw[0:1, :] * back2 + cw[1:2, :] * back1 + cw[2:3, :] * gate + cb
        o_ref[r0:r0 + tp, :] = (_silu(conv) * up).astype(o_ref.dtype)
        tail = gate[tp - 2:tp, :]
        r0 += tp
    tail_ref[...] = gate[tp - 8:tp, :]


def _ffn_in(u2, w_fi, conv_w, conv_b, s_real):
    rows, d = u2.shape
    tm = 1024
    tn = 512
    up_off = D_FF // tn
    halo_map = lambda n, m: (rows // HALO - 1, 0)
    return pl.pallas_call(
        functools.partial(_ffn_in_kernel, part_rows=(tm // 2, tm // 2)), name="ffn_in_conv_gate",
        grid=(D_FF // tn, s_real // tm),
        in_specs=[pl.BlockSpec((tm, d), lambda n, m: (m, 0)), pl.BlockSpec((HALO, d), halo_map),
                  pl.BlockSpec((None, d, tn), lambda n, m: (0, 0, n)),
                  pl.BlockSpec((None, d, tn), lambda n, m: (0, 0, n + up_off)),
                  pl.BlockSpec((None, conv_w.shape[1], tn), lambda n, m: (0, 0, n)),
                  pl.BlockSpec((1, tn), lambda n, m: (0, n))],
        out_specs=pl.BlockSpec((tm, tn), lambda n, m: (m, n)),
        out_shape=jax.ShapeDtypeStruct((s_real, D_FF), BF16),
        scratch_shapes=[pltpu.VMEM((d, tn), BF16), pltpu.VMEM((d, tn), BF16), pltpu.VMEM((8, tn), F32)],
        compiler_params=_cparams(2),
    )(u2, u2, w_fi, w_fi, conv_w, conv_b)


def _ffn_out_kernel(a_ref, w_ref, h2_ref, g_ref, o_ref):
    kk = pl.program_id(1)

    @pl.when(kk == 0)
    def _():
        o_ref[...] = h2_ref[...]

    o_ref[...] += _dot(a_ref[...], w_ref[...])

    @pl.when(kk == pl.num_programs(1) - 1)
    def _():
        o_ref[...] = _rms(o_ref[...], g_ref[...])


def _ffn_out(act, w_fo, h2, g_final):
    s_real, k = act.shape
    d = w_fo.shape[1]
    tm = 512
    tk = k // 4
    assert tk % LANES == 0
    return pl.pallas_call(
        _ffn_out_kernel, name="ffn_out_final_norm",
        grid=(s_real // tm, k // tk),
        in_specs=[pl.BlockSpec((tm, tk), lambda m, kk: (m, kk)), pl.BlockSpec((tk, d), lambda m, kk: (kk, 0)),
                  pl.BlockSpec((tm, d), lambda m, kk: (m, 0)), pl.BlockSpec((1, d), lambda m, kk: (0, 0))],
        out_specs=pl.BlockSpec((tm, d), lambda m, kk: (m, 0)),
        out_shape=jax.ShapeDtypeStruct((s_real, d), F32),
        compiler_params=_cparams(2),
    )(act, w_fo, h2, g_final)


def kernel(x, positions, meta_tokens, w_in, w_q_up, w_kv_up, w_branch_mla, w_branch_hgrn, w_out, w_ffn_in,
           w_ffn_out, conv_w, conv_b, g_mix_norm, g_q_norm, g_kv_norm, g_hgrn_norm, g_ffn_norm, g_final_norm,
           lb_raw):
    b, s_real, d = x.shape
    assert b == 1 and w_in.shape[0] == 1 and s_real % 1024 == 0
    dt = x.dtype

    prefix = jnp.concatenate([jnp.zeros((PAD_LEN, d), dt), meta_tokens.astype(dt)], axis=0)
    pos = jnp.concatenate([positions[0].astype(jnp.int32) + N_META, jnp.zeros((PAD_LEN,), jnp.int32),
                           jnp.arange(N_META, dtype=jnp.int32)])
    inv = 1.0 / (ROPE_THETA ** (jnp.arange(0, QK_ROPE_DIM, 2, dtype=F32) / QK_ROPE_DIM))
    inv = jnp.concatenate([inv, inv, jnp.zeros((LANES - QK_ROPE_DIM,), F32)])[None, :]

    w_in_t = jnp.swapaxes(w_in, 1, 2)[0]
    w_lat = _w_lat(w_in_t)
    w_kv3 = w_kv_up[0].reshape(KV_LORA_RANK, MLA_HEADS, QK_NOPE_DIM + V_HEAD_DIM)
    w_k = w_kv3[:, :, :QK_NOPE_DIM].reshape(KV_LORA_RANK, -1).astype(BF16)
    w_vt = w_kv3[:, :, QK_NOPE_DIM:].reshape(KV_LORA_RANK, -1).T.astype(BF16)
    w_fo = w_ffn_out[0].astype(BF16)

    cos_t, sin_t = _rope_tables(pos.astype(F32)[:, None], inv)

    u, qn, kvn, kr = _proj_lat(x[0], prefix, g_mix_norm, w_lat, g_q_norm, g_kv_norm, cos_t, sin_t)
    hq = _matmul_cols_t(u, w_in_t, LAT_END, d, BF16, 1024, "proj_hgrn_q")
    hf = _matmul_cols_t(u, w_in_t, LAT_END + d, d, F32, 1024, "proj_hgrn_forget")
    hrest = _matmul_cols_t(u, w_in_t, LAT_END + 2 * d, 4 * d, BF16, 1024, "proj_hgrn_rest")
    q_cat = _q_up(qn, w_q_up, cos_t, sin_t)
    k_slabs = _kv_up(kvn, w_k)
    v_t = _v_up_t(kvn, w_vt)
    o_mla = _attention(q_cat, k_slabs, v_t, kr, s_real)
    o_hgrn = _hgrn(hq, hf, hrest, lb_raw, g_hgrn_norm)
    merged = _merge(o_mla, o_hgrn, w_branch_mla, w_branch_hgrn, hrest, 2 * d)
    h2, u2 = _out_proj(merged, w_out, x[0], prefix, g_ffn_norm)

    act = _ffn_in(u2, w_ffn_in, conv_w, conv_b, s_real)
    out = _ffn_out(act, w_fo, h2, g_final_norm[None, :])
    return out[None]
```

```python
import functools

import jax
import jax.numpy as jnp
from jax import lax
from jax.experimental import pallas as pl
from jax.experimental.pallas import tpu as pltpu

F32 = jnp.float32
BF16 = jnp.bfloat16

N_META = 16
BLOCK = 128
PAD_LEN = BLOCK - N_META
MLA_HEADS = 16
Q_LORA_RANK = 1536
KV_LORA_RANK = 512
QK_NOPE_DIM = 128
QK_ROPE_DIM = 64
QK_HEAD_DIM = QK_NOPE_DIM + QK_ROPE_DIM
V_HEAD_DIM = 128
ROPE_THETA = 10000.0
HGRN_HEADS = 16
HGRN_EXPAND = 128
HGRN_V_DIM = 128
D_FF = 5632
NORM_EPS = 1e-6

LANES = 128
Q_CAT = 2 * LANES
LAT_COLS = Q_LORA_RANK + KV_LORA_RANK + LANES
LOG2E = 1.4426950408889634
VMEM_LIMIT = 52 * 1024 * 1024

ALL_ROWS_TILE = 640
PROJ_COL_TILE = 1024
MERGE_COL_TILE = 512
SLABS_PER_TILE = 8
Q_HEADS_PER_TILE = 4
WEIGHT_PREP_TILE = 512
LATENT_ROW_TILE = 512
OUT_PROJ_ROW_TILE = 256
ATTN_QUERY_TILE = 1024
FFN_ROW_TILE = 1024
FFN_COL_TILE = 512
FFN_OUT_ROW_TILE = 512
FFN_OUT_K_STEPS = 4


def _cparams(n_axes):
    return pltpu.CompilerParams(dimension_semantics=("arbitrary",) * n_axes, vmem_limit_bytes=VMEM_LIMIT)


def _row_tile(rows, target):
    for t in range(target, 0, -LANES):
        if rows % t == 0:
            return t
    raise ValueError(f"no 128-multiple row tile for {rows}")


def _dot(a, b):
    return jnp.dot(a, b, preferred_element_type=F32)


def _dot_nt(a, b):
    return lax.dot_general(a, b, (((1,), (1,)), ((), ())), preferred_element_type=F32)


def _dot_tn(a, b):
    return lax.dot_general(a, b, (((0,), (0,)), ((), ())), preferred_element_type=F32)


def _rms(x, g):
    return x * lax.rsqrt(jnp.mean(x * x, axis=-1, keepdims=True) + NORM_EPS) * g


def _sigmoid(x):
    return 0.5 * jnp.tanh(0.5 * x) + 0.5


def _silu(x):
    return x * _sigmoid(x)


def _rope(x, cos_t, sin_t):
    rot = pltpu.roll(x, 32, 1) - pltpu.roll(x, 96, 1)
    return x * cos_t + rot * sin_t


def _rope_table_kernel(pos_ref, inv_ref, cos_ref, sin_ref):
    ang = pos_ref[...] * inv_ref[...]
    keep = lax.broadcasted_iota(jnp.int32, ang.shape, 1) < QK_ROPE_DIM
    cos_ref[...] = jnp.where(keep, jnp.cos(ang), 0.0)
    sin_ref[...] = jnp.where(keep, jnp.sin(ang), 0.0)


def _rope_tables(pos_f, inv):
    rows = pos_f.shape[0]
    tm = _row_tile(rows, ALL_ROWS_TILE)
    spec = pl.BlockSpec((tm, LANES), lambda i: (i, 0))
    return pl.pallas_call(
        _rope_table_kernel, name="rope_tables",
        grid=(rows // tm,),
        in_specs=[pl.BlockSpec((tm, 1), lambda i: (i, 0)), pl.BlockSpec((1, LANES), lambda i: (0, 0))],
        out_specs=[spec, spec],
        out_shape=[jax.ShapeDtypeStruct((rows, LANES), F32)] * 2,
        compiler_params=_cparams(1),
    )(pos_f, inv)


LAT_END = Q_LORA_RANK + KV_LORA_RANK + QK_ROPE_DIM


def _w_lat_kernel(w_ref, o_ref):
    row = lax.broadcasted_iota(jnp.int32, w_ref.shape, 0)
    o_ref[...] = jnp.where(row < LAT_END, w_ref[...], 0.0).astype(o_ref.dtype)


def _w_lat(w_in_t):
    d = w_in_t.shape[1]
    tk = WEIGHT_PREP_TILE
    return pl.pallas_call(
        _w_lat_kernel, name="w_latent_cast",
        grid=(d // tk,),
        in_specs=[pl.BlockSpec((LAT_COLS, tk), lambda c: (0, c))],
        out_specs=pl.BlockSpec((LAT_COLS, tk), lambda c: (0, c)),
        out_shape=jax.ShapeDtypeStruct((LAT_COLS, d), BF16),
        compiler_params=_cparams(1),
    )(w_in_t)


def _two_source_rows(n_token_tiles, body, token_ref, prefix_ref):
    i = pl.program_id(0)

    @pl.when(i < n_token_tiles)
    def _():
        body(token_ref[...], slice(0, token_ref.shape[0]))

    @pl.when(i == n_token_tiles)
    def _():
        body(prefix_ref[...], slice(0, BLOCK))


def _proj_lat_kernel(x_ref, pre_ref, gm_ref, w_ref, gq_ref, gkv_ref, cos_ref, sin_ref, u_ref, qn_ref, kvn_ref,
                     kr_ref, *, n_token_tiles):
    kv_end = Q_LORA_RANK + KV_LORA_RANK

    def body(h, rows):
        u = _rms(h, gm_ref[...]).astype(u_ref.dtype)
        u_ref[rows, :] = u
        y = _dot_nt(u, w_ref[...])
        qn_ref[rows, :] = _rms(y[:, :Q_LORA_RANK], gq_ref[...]).astype(qn_ref.dtype)
        kvn_ref[rows, :] = _rms(y[:, Q_LORA_RANK:kv_end], gkv_ref[...]).astype(kvn_ref.dtype)
        kr_ref[rows, :] = _rope(y[:, kv_end:], cos_ref[rows, :], sin_ref[rows, :]).astype(kr_ref.dtype)

    _two_source_rows(n_token_tiles, body, x_ref, pre_ref)


def _proj_lat(x2d, prefix, g_mix, w_lat, g_q, g_kv, cos_t, sin_t):
    s_real, d = x2d.shape
    rows = s_real + BLOCK
    tm = LATENT_ROW_TILE
    n_tok = s_real // tm
    row_spec = lambda n: pl.BlockSpec((tm, n), lambda i: (i, 0))
    full_spec = lambda r, n: pl.BlockSpec((r, n), lambda i: (0, 0))
    return pl.pallas_call(
        functools.partial(_proj_lat_kernel, n_token_tiles=n_tok), name="proj_latents",
        grid=(n_tok + 1,),
        in_specs=[pl.BlockSpec((tm, d), lambda i: (jnp.minimum(i, n_tok - 1), 0)), full_spec(BLOCK, d),
                  full_spec(1, d), full_spec(LAT_COLS, d), full_spec(1, Q_LORA_RANK), full_spec(1, KV_LORA_RANK),
                  row_spec(LANES), row_spec(LANES)],
        out_specs=[row_spec(d), row_spec(Q_LORA_RANK), row_spec(KV_LORA_RANK), row_spec(LANES)],
        out_shape=[jax.ShapeDtypeStruct((rows, d), BF16),
                   jax.ShapeDtypeStruct((rows, Q_LORA_RANK), BF16),
                   jax.ShapeDtypeStruct((rows, KV_LORA_RANK), BF16),
                   jax.ShapeDtypeStruct((rows, LANES), BF16)],
        compiler_params=_cparams(1),
    )(x2d, prefix, g_mix, w_lat, g_q, g_kv, cos_t, sin_t)


def _matmul_t_kernel(x_ref, w_ref, o_ref, w_bf):
    @pl.when(pl.program_id(1) == 0)
    def _():
        w_bf[...] = w_ref[...].T.astype(BF16)

    o_ref[...] = _dot(x_ref[...], w_bf[...]).astype(o_ref.dtype)


def _matmul_cols_t(x, w_t, col_start, n_cols, out_dtype, tn, name):
    rows, k = x.shape
    tm = _row_tile(rows, ALL_ROWS_TILE)
    assert col_start % 8 == 0 and n_cols % tn == 0
    return pl.pallas_call(
        _matmul_t_kernel, name=name,
        grid=(n_cols // tn, rows // tm),
        in_specs=[pl.BlockSpec((tm, k), lambda n, m: (m, 0)),
                  pl.BlockSpec((pl.Element(tn), pl.Element(k)),
                               lambda n, m: (pl.multiple_of(col_start + n * tn, 8), 0))],
        out_specs=pl.BlockSpec((tm, tn), lambda n, m: (m, n)),
        out_shape=jax.ShapeDtypeStruct((rows, n_cols), out_dtype),
        scratch_shapes=[pltpu.VMEM((k, tn), BF16)],
        compiler_params=_cparams(2),
    )(x, w_t)


def _kv_up_kernel(x_ref, w_ref, o_ref):
    y = _dot(x_ref[...], w_ref[...])
    for hh in range(o_ref.shape[0]):
        o_ref[hh] = y[:, hh * LANES:(hh + 1) * LANES].astype(o_ref.dtype)


V_EXT = V_HEAD_DIM + 16


def _v_up_t_kernel(x_ref, w_ref, o_ref):
    y_t = _dot_nt(w_ref[...], x_ref[...])
    pad_shape = (V_EXT - V_HEAD_DIM, o_ref.shape[2])
    ones_row = (lax.broadcasted_iota(jnp.int32, pad_shape, 0) == 0).astype(o_ref.dtype)
    for hh in range(o_ref.shape[0]):
        o_ref[hh, :LANES, :] = y_t[hh * LANES:(hh + 1) * LANES, :].astype(o_ref.dtype)
        o_ref[hh, LANES:, :] = ones_row


def _v_up_t(kvn, w_vt):
    rows, k = kvn.shape
    n_slabs = w_vt.shape[0] // LANES
    tm = _row_tile(rows, ALL_ROWS_TILE)
    per_tile = SLABS_PER_TILE
    return pl.pallas_call(
        _v_up_t_kernel, name="v_up_transposed",
        grid=(n_slabs // per_tile, rows // tm),
        in_specs=[pl.BlockSpec((tm, k), lambda n, m: (m, 0)),
                  pl.BlockSpec((per_tile * LANES, k), lambda n, m: (n, 0))],
        out_specs=pl.BlockSpec((per_tile, V_EXT, tm), lambda n, m: (n, 0, m)),
        out_shape=jax.ShapeDtypeStruct((n_slabs, V_EXT, rows), BF16),
        compiler_params=_cparams(2),
    )(kvn, w_vt)


def _kv_up(kvn, w_kv):
    rows, k = kvn.shape
    n_slabs = w_kv.shape[1] // LANES
    tm = _row_tile(rows, ALL_ROWS_TILE)
    per_tile = SLABS_PER_TILE
    return pl.pallas_call(
        _kv_up_kernel, name="kv_up",
        grid=(n_slabs // per_tile, rows // tm),
        in_specs=[pl.BlockSpec((tm, k), lambda n, m: (m, 0)),
                  pl.BlockSpec((k, per_tile * LANES), lambda n, m: (0, n))],
        out_specs=pl.BlockSpec((per_tile, tm, LANES), lambda n, m: (n, m, 0)),
        out_shape=jax.ShapeDtypeStruct((n_slabs, rows, LANES), BF16),
        compiler_params=_cparams(2),
    )(kvn, w_kv)


def _q_up_kernel(x_ref, w_ref, cos_ref, sin_ref, o_ref, w_bf, *, heads_per_tile):
    @pl.when(pl.program_id(1) == 0)
    def _():
        w_bf[...] = jnp.zeros(w_bf.shape, w_bf.dtype)
        for hh in range(heads_per_tile):
            w_bf[:, hh * Q_CAT:hh * Q_CAT + QK_HEAD_DIM] = (
                w_ref[:, hh * QK_HEAD_DIM:(hh + 1) * QK_HEAD_DIM].astype(w_bf.dtype))

    tm = o_ref.shape[0]
    small = tm // 5
    r0 = 0
    for tp in (tm - small, small):
        rows = slice(r0, r0 + tp)
        y = _dot(x_ref[rows, :], w_bf[...]) * (QK_HEAD_DIM ** -0.5 * LOG2E)
        cos_t = cos_ref[rows, :]
        sin_t = sin_ref[rows, :]
        for hh in range(heads_per_tile):
            lo = hh * Q_CAT
            o_ref[rows, lo:lo + LANES] = y[:, lo:lo + LANES].astype(o_ref.dtype)
            o_ref[rows, lo + LANES:lo + Q_CAT] = _rope(y[:, lo + LANES:lo + Q_CAT], cos_t, sin_t).astype(o_ref.dtype)
        r0 += tp


def _q_up(qn, w_q_up, cos_t, sin_t):
    rows, k = qn.shape
    heads_per_tile = Q_HEADS_PER_TILE
    tm = _row_tile(rows, ALL_ROWS_TILE)
    tn = heads_per_tile * Q_CAT
    n_tiles = w_q_up.shape[2] // (heads_per_tile * QK_HEAD_DIM)
    return pl.pallas_call(
        functools.partial(_q_up_kernel, heads_per_tile=heads_per_tile), name="q_up_rope",
        grid=(n_tiles, rows // tm),
        in_specs=[pl.BlockSpec((tm, k), lambda n, m: (m, 0)),
                  pl.BlockSpec((None, k, heads_per_tile * QK_HEAD_DIM), lambda n, m: (0, 0, n)),
                  pl.BlockSpec((tm, LANES), lambda n, m: (m, 0)), pl.BlockSpec((tm, LANES), lambda n, m: (m, 0))],
        out_specs=pl.BlockSpec((tm, tn), lambda n, m: (m, n)),
        out_shape=jax.ShapeDtypeStruct((rows, n_tiles * tn), BF16),
        scratch_shapes=[pltpu.VMEM((k, tn), BF16)],
        compiler_params=_cparams(2),
    )(qn, w_q_up, cos_t, sin_t)


ATTN_KEY_BLOCK = 64
ATTN_PART = 2 * LANES


def _attn_kernel(q_ref, kn_ref, vt_ref, kr_ref, o_ref, kcat_ref, m_ref, acc_ref, sa0_ref, sa1_ref, sb0_ref,
                 sb1_ref, spre_ref, p_ref, *, tq, n_real_tiles):
    step_id = pl.program_id(1)
    qi = step_id - 1
    s_real = n_real_tiles * tq
    tk = tq // 2

    @pl.when(step_id == 0)
    def _():
        kcat_ref[:, :LANES] = kn_ref[...]
        kcat_ref[:, LANES:] = kr_ref[...]

    th = ATTN_PART
    parts = tuple(range(0, tq, th))

    def init():
        m_ref[...] = jnp.full(m_ref.shape, -jnp.inf, F32)
        acc_ref[...] = jnp.zeros(acc_ref.shape, F32)

    def update(c0, blocks, pv):
        cols = slice(c0, c0 + th)

        def load(ref, row0, n, visible):
            s = ref[row0:row0 + n, cols]
            return s if visible is None else jnp.where(visible(row0, n), s, -jnp.inf)

        m_prev = m_ref[:, cols]
        m_cur = functools.reduce(
            jnp.maximum, [jnp.max(load(ref, r, n, vis), axis=0, keepdims=True) for ref, r, n, _, vis in blocks])
        m_new = jnp.maximum(m_prev, m_cur)
        alpha = jnp.exp2(m_prev - m_new)
        m_ref[:, cols] = m_new
        for ref, r, n, p_row, vis in blocks:
            for b0 in range(0, n, ATTN_KEY_BLOCK):
                p = jnp.exp2(load(ref, r + b0, ATTN_KEY_BLOCK, vis) - m_new)
                p_ref[p_row + b0:p_row + b0 + ATTN_KEY_BLOCK, cols] = p.astype(p_ref.dtype)
        new = functools.reduce(jnp.add, [_dot(v_t, p_ref[p_row:p_row + n, cols]) for p_row, n, v_t in pv])
        acc_ref[:, cols] = alpha * acc_ref[:, cols] + new

    def finalize():
        acc = acc_ref[...]
        o_ref[...] = (acc[:V_HEAD_DIM, :] / acc[V_HEAD_DIM:V_HEAD_DIM + 1, :]).T.astype(o_ref.dtype)

    def prefix_scores():
        k_pre = kcat_ref[s_real:s_real + BLOCK, :]
        for c0 in parts:
            spre_ref[:, c0:c0 + th] = _dot_nt(k_pre, q_ref[c0:c0 + th, :])

    v_pre_t = vt_ref[:, s_real:s_real + BLOCK]
    pre_row = tk

    def key_ids(row0, n):
        return lax.broadcasted_iota(jnp.int32, (n, th), 0) + row0

    def query_ids(c0, n):
        return lax.broadcasted_iota(jnp.int32, (n, th), 1) + c0

    @pl.when(step_id > 0)
    def _():
        init()
        prefix_scores()

        def raw_scores(dst_ref, j, which=parts):
            k_blk = kcat_ref[pl.ds(pl.multiple_of(j * tk, tk), tk), :]
            for c0 in which:
                dst_ref[:, c0:c0 + th] = _dot_nt(k_blk, q_ref[c0:c0 + th, :])

        def full_update(src_ref, tile, which=parts):
            v_t = vt_ref[:, pl.ds(pl.multiple_of(tile * tk, tk), tk)]
            for c0 in which:
                update(c0, [(src_ref, 0, tk, 0, None)], [(0, tk, v_t)])

        def pair(a, src, dst):
            raw_scores(dst[0], a + 2)
            raw_scores(dst[1], a + 3)
            full_update(src[0], a)
            full_update(src[1], a + 1)

        def diagonal(c0, src_ref, tile):
            q0 = c0 % tk
            n_k = q0 + th
            causal = lambda row0, n: key_ids(row0, n) <= query_ids(q0, n)
            no_pads = lambda row0, n: key_ids(row0, n) >= PAD_LEN
            start = pl.multiple_of(tile * tk, tk)
            update(c0, [(src_ref, 0, n_k, 0, causal), (spre_ref, 0, BLOCK, pre_row, no_pads)],
                   [(0, n_k, vt_ref[:, pl.ds(start, n_k)]), (pre_row, BLOCK, v_pre_t)])

        def own_tiles(src):
            lower = tuple(c0 for c0 in parts if c0 < tk)
            upper = tuple(c0 for c0 in parts if c0 >= tk)
            for c0 in lower:
                diagonal(c0, src[0], 2 * qi)
            full_update(src[0], 2 * qi, upper)
            for c0 in upper:
                diagonal(c0, src[1], 2 * qi + 1)
            finalize()

        buf_a = (sa0_ref, sa1_ref)
        buf_b = (sb0_ref, sb1_ref)
        raw_scores(buf_a[0], 0)
        raw_scores(buf_a[1], 1)

        def body(i, carry):
            pair(4 * i, buf_a, buf_b)
            pair(4 * i + 2, buf_b, buf_a)
            return carry

        lax.fori_loop(0, qi // 2, body, 0)

        @pl.when(qi % 2 == 1)
        def _():
            pair(2 * qi - 2, buf_a, buf_b)
            own_tiles(buf_b)

        @pl.when(qi % 2 == 0)
        def _():
            own_tiles(buf_a)

    @pl.when(step_id == 0)
    def _():
        init()
        prefix_scores()
        for c0 in parts:
            def visible(row0, n, c0=c0):
                key, query = key_ids(row0, n), query_ids(c0, n)
                return (key <= query) & ((key >= PAD_LEN) | (key == query))

            update(c0, [(spre_ref, 0, BLOCK, pre_row, visible)], [(pre_row, BLOCK, v_pre_t)])
        finalize()


def _attention(q_cat, k_slabs, v_t, kr, s_real):
    rows = q_cat.shape[0]
    tq = ATTN_QUERY_TILE
    tk = tq // 2
    n_real_tiles = s_real // tq
    q_tile = lambda h, i: ((i + n_real_tiles) % (n_real_tiles + 1), h)
    return pl.pallas_call(
        functools.partial(_attn_kernel, tq=tq, n_real_tiles=n_real_tiles), name="mla_attention",
        grid=(MLA_HEADS, n_real_tiles + 1),
        in_specs=[pl.BlockSpec((tq, Q_CAT), q_tile),
                  pl.BlockSpec((None, rows, LANES), lambda h, i: (h, 0, 0)),
                  pl.BlockSpec((None, V_EXT, rows), lambda h, i: (h, 0, 0)),
                  pl.BlockSpec((rows, LANES), lambda h, i: (0, 0))],
        out_specs=pl.BlockSpec((tq, V_HEAD_DIM), q_tile),
        out_shape=jax.ShapeDtypeStruct((rows, MLA_HEADS * V_HEAD_DIM), BF16),
        scratch_shapes=[pltpu.VMEM((rows, Q_CAT), BF16), pltpu.VMEM((1, tq), F32),
                        pltpu.VMEM((V_EXT, tq), F32)] + [pltpu.VMEM((tk, tq), F32)] * 4
                       + [pltpu.VMEM((BLOCK, tq), F32), pltpu.VMEM((tk + BLOCK, tq), BF16)],
        compiler_params=_cparams(2),
    )(q_cat, k_slabs, v_t, kr)


def _split3(x):
    x1 = x.astype(BF16)
    r1 = x - x1.astype(F32)
    x2 = r1.astype(BF16)
    x3 = (r1 - x2.astype(F32)).astype(BF16)
    return x1, x2, x3


def _block_row(x, size, j):
    c, n = x.shape
    g = x.reshape(c // size, size, n)[:, j:j + 1, :]
    return jnp.broadcast_to(g, (c // size, size, n)).reshape(c, n)


HGRN_LEVELS = 7
HGRN_GROUP = 16


def _hgrn_head(hq, hf, hi, hg, lb_raw, gain, st, valid, row, lev):
    n = BLOCK
    top = jnp.max(lb_raw, axis=0, keepdims=True)
    e = jnp.exp(lb_raw - top)
    lb = e[0:1, :] / jnp.sum(e, axis=0, keepdims=True)

    f = lb + (1.0 - lb) * _sigmoid(hf)
    f_eff = jnp.where(valid, f, 1.0)
    g = jnp.where(valid, jnp.log2(f), 0.0)
    k = jnp.where(valid, 1.0 - f, 0.0)
    q = _silu(hq.astype(F32))

    col = lax.broadcasted_iota(jnp.int32, (n, n), 1)
    tri = (col <= row).astype(BF16)
    g1, g2, g3 = _split3(g)
    b = _dot(tri, g1) + _dot(tri, g2) + _dot(tri, g3)

    q_bf = q.astype(BF16)
    k_bf = k.astype(BF16)
    a = jnp.where(lev == 0, jnp.sum(q * k, axis=1, keepdims=True), 0.0)
    for level in range(1, HGRN_LEVELS + 1):
        size = 1 << level
        if level == 1:
            w = jnp.where((row & 1) == 1, f_eff, 1.0)
        elif level == 2:
            r4 = row & 3
            up1 = pltpu.roll(g, n - 1, 0)
            dn1 = pltpu.roll(g, 1, 0)
            w = jnp.exp2(jnp.where(r4 == 0, up1, jnp.where(r4 == 1, 0.0, jnp.where(r4 == 2, g, g + dn1))))
        else:
            w = jnp.exp2(-jnp.abs(b - _block_row(b, size, size // 2 - 1)))
        w_bf = w.astype(BF16)
        a_l = _dot_nt(q_bf * w_bf, k_bf * w_bf)
        a = jnp.where(lev == level, a_l, a)

    o = _dot(a.astype(BF16), hi) + _dot_nt((q * jnp.exp2(b)).astype(BF16), st.astype(BF16))
    b_last = b[n - 1:n, :]
    st_new = st * jnp.exp2(b_last) + _dot_tn(hi, (k * jnp.exp2(b_last - b)).astype(BF16))
    out = _rms(o, gain) * _silu(hg.astype(F32))
    return out, st_new


def _hgrn_kernel(hq_ref, hf_ref, hi_ref, hg_ref, lb_ref, g_ref, o_ref, st_ref):
    c = pl.program_id(0)
    n = BLOCK

    @pl.when(c == 0)
    def _():
        st_ref[...] = jnp.zeros(st_ref.shape, F32)

    row = lax.broadcasted_iota(jnp.int32, (n, n), 0)
    col = lax.broadcasted_iota(jnp.int32, (n, n), 1)
    valid = (c > 0) | (row >= PAD_LEN)
    x = row ^ col
    lev = functools.reduce(jnp.add, [(x >= (1 << i)).astype(jnp.int32) for i in range(HGRN_LEVELS)])
    lev = jnp.where(col > row, -1, lev)
    gain = g_ref[...]

    def head_group(i, carry):
        for hh in range(HGRN_GROUP):
            head = HGRN_GROUP * i + hh
            cols = pl.ds(pl.multiple_of(head * BLOCK, BLOCK), BLOCK)
            out, st_new = _hgrn_head(hq_ref[:, cols], hf_ref[:, cols], hi_ref[:, cols], hg_ref[:, cols],
                                     lb_ref[:, cols], gain, st_ref[head], valid, row, lev)
            st_ref[head] = st_new
            o_ref[:, cols] = out.astype(o_ref.dtype)
        return carry

    lax.fori_loop(0, HGRN_HEADS // HGRN_GROUP, head_group, 0)


def _hgrn(hq, hf, hrest, lb_raw, g_hgrn):
    rows, width = hq.shape
    n_chunks = rows // BLOCK
    blk = lambda off: pl.BlockSpec((BLOCK, width), lambda c: ((c + n_chunks - 1) % n_chunks, off))
    return pl.pallas_call(
        _hgrn_kernel, name="hgrn2",
        grid=(n_chunks,),
        in_specs=[blk(0), blk(0), blk(0), blk(1),
                  pl.BlockSpec((lb_raw.shape[0], width), lambda c: (0, 0)),
                  pl.BlockSpec((1, HGRN_V_DIM), lambda c: (0, 0))],
        out_specs=blk(0),
        out_shape=jax.ShapeDtypeStruct((rows, width), BF16),
        scratch_shapes=[pltpu.VMEM((HGRN_HEADS, HGRN_V_DIM, HGRN_EXPAND), F32)],
        compiler_params=_cparams(1),
    )(hq, hf, hrest, hrest, lb_raw, g_hgrn)


def _merge_kernel(om_ref, oh_ref, wa_ref, wb_ref, ga_ref, gb_ref, o_ref, wa_bf, wb_bf):
    @pl.when(pl.program_id(1) == 0)
    def _():
        wa_bf[...] = wa_ref[...].astype(BF16)
        wb_bf[...] = wb_ref[...].astype(BF16)

    tm = o_ref.shape[0]
    small = tm // 5
    r0 = 0
    for tp in (tm - small, small):
        rows = slice(r0, r0 + tp)
        a = _dot(om_ref[rows, :], wa_bf[...])
        b = _dot(oh_ref[rows, :], wb_bf[...])
        o = _sigmoid(ga_ref[rows, :].astype(F32)) * a + _sigmoid(gb_ref[rows, :].astype(F32)) * b
        o_ref[rows, :] = o.astype(o_ref.dtype)
        r0 += tp


def _merge(o_mla, o_hgrn, w_a, w_b, hrest, gate_off):
    rows, k = o_mla.shape
    d = w_a.shape[2]
    tm = _row_tile(rows, ALL_ROWS_TILE)
    tn = MERGE_COL_TILE
    ga_off = gate_off // tn
    gb_off = (gate_off + d) // tn
    w_spec = pl.BlockSpec((None, k, tn), lambda n, m: (0, 0, n))
    return pl.pallas_call(
        _merge_kernel, name="branch_merge",
        grid=(d // tn, rows // tm),
        in_specs=[pl.BlockSpec((tm, k), lambda n, m: (m, 0)), pl.BlockSpec((tm, k), lambda n, m: (m, 0)),
                  w_spec, w_spec,
                  pl.BlockSpec((tm, tn), lambda n, m: (m, n + ga_off)),
                  pl.BlockSpec((tm, tn), lambda n, m: (m, n + gb_off))],
        out_specs=pl.BlockSpec((tm, tn), lambda n, m: (m, n)),
        out_shape=jax.ShapeDtypeStruct((rows, d), BF16),
        scratch_shapes=[pltpu.VMEM((k, tn), BF16), pltpu.VMEM((k, tn), BF16)],
        compiler_params=_cparams(2),
    )(o_mla, o_hgrn, w_a, w_b, hrest, hrest)


def _out_proj_kernel(x_ref, pre_ref, mg_ref, w_ref, g_ref, h2_ref, u2_ref, w_bf, *, n_token_tiles):
    @pl.when(pl.program_id(0) == 0)
    def _():
        w_bf[...] = w_ref[...].astype(BF16)

    def body(h, rows):
        h2 = h + _dot(mg_ref[rows, :], w_bf[...])
        h2_ref[rows, :] = h2
        u2_ref[rows, :] = _rms(h2, g_ref[...]).astype(u2_ref.dtype)

    _two_source_rows(n_token_tiles, body, x_ref, pre_ref)


def _out_proj(merged, w_out, x2d, prefix, g_ffn):
    s_real, d = x2d.shape
    rows = s_real + BLOCK
    tm = OUT_PROJ_ROW_TILE
    n_tok = s_real // tm
    row_spec = pl.BlockSpec((tm, d), lambda i: (i, 0))
    return pl.pallas_call(
        functools.partial(_out_proj_kernel, n_token_tiles=n_tok), name="mix_out_proj",
        grid=(n_tok + 1,),
        in_specs=[pl.BlockSpec((tm, d), lambda i: (jnp.minimum(i, n_tok - 1), 0)),
                  pl.BlockSpec((BLOCK, d), lambda i: (0, 0)), row_spec,
                  pl.BlockSpec((None, d, d), lambda i: (0, 0, 0), pipeline_mode=pl.Buffered(1)),
                  pl.BlockSpec((1, d), lambda i: (0, 0))],
        out_specs=[row_spec, row_spec],
        out_shape=[jax.ShapeDtypeStruct((rows, d), F32), jax.ShapeDtypeStruct((rows, d), BF16)],
        scratch_shapes=[pltpu.VMEM((d, d), BF16)],
        compiler_params=_cparams(1),
    )(x2d, prefix, merged, w_out, g_ffn)


HALO = 16


def _ffn_in_kernel(u_ref, halo_ref, wg_ref, wu_ref, cw_ref, cb_ref, o_ref, wg_bf, wu_bf, tail_ref, *, part_rows):
    @pl.when(pl.program_id(1) == 0)
    def _():
        wg_bf[...] = wg_ref[...].astype(BF16)
        wu_bf[...] = wu_ref[...].astype(BF16)
        tail_ref[...] = _dot(halo_ref[...], wg_bf[...])[HALO - 8:HALO, :]

    wg = wg_bf[...]
    wu = wu_bf[...]
    cw = cw_ref[...]
    cb = cb_ref[...]
    tail = tail_ref[6:8, :]
    r0 = 0
    for tp in part_rows:
        row = lax.broadcasted_iota(jnp.int32, (tp, o_ref.shape[1]), 0)
        u = u_ref[r0:r0 + tp, :]
        gate = _dot(u, wg)
        up = _dot(u, wu)
        back1 = jnp.where(row == 0, tail[1:2, :], pltpu.roll(gate, 1, 0))
        back2 = jnp.where(row == 0, tail[0:1, :], jnp.where(row == 1, tail[1:2, :], pltpu.roll(gate, 2, 0)))
        conv = cw[0:1, :] * back2 + cw[1:2, :] * back1 + cw[2:3, :] * gate + cb
        o_ref[r0:r0 + tp, :] = (_silu(conv) * up).astype(o_ref.dtype)
        tail = gate[tp - 2:tp, :]
        r0 += tp
    tail_ref[...] = gate[tp - 8:tp, :]


def _ffn_in(u2, w_fi, conv_w, conv_b, s_real):
    rows, d = u2.shape
    tm = FFN_ROW_TILE
    tn = FFN_COL_TILE
    up_off = D_FF // tn
    halo_map = lambda n, m: (rows // HALO - 1, 0)
    return pl.pallas_call(
        functools.partial(_ffn_in_kernel, part_rows=(tm // 2, tm // 2)), name="ffn_in_conv_gate",
        grid=(D_FF // tn, s_real // tm),
        in_specs=[pl.BlockSpec((tm, d), lambda n, m: (m, 0)), pl.BlockSpec((HALO, d), halo_map),
                  pl.BlockSpec((None, d, tn), lambda n, m: (0, 0, n)),
                  pl.BlockSpec((None, d, tn), lambda n, m: (0, 0, n + up_off)),
                  pl.BlockSpec((None, conv_w.shape[1], tn), lambda n, m: (0, 0, n)),
                  pl.BlockSpec((1, tn), lambda n, m: (0, n))],
        out_specs=pl.BlockSpec((tm, tn), lambda n, m: (m, n)),
        out_shape=jax.ShapeDtypeStruct((s_real, D_FF), BF16),
        scratch_shapes=[pltpu.VMEM((d, tn), BF16), pltpu.VMEM((d, tn), BF16), pltpu.VMEM((8, tn), F32)],
        compiler_params=_cparams(2),
    )(u2, u2, w_fi, w_fi, conv_w, conv_b)


def _ffn_out_kernel(a_ref, w_ref, h2_ref, g_ref, o_ref):
    kk = pl.program_id(1)

    @pl.when(kk == 0)
    def _():
        o_ref[...] = h2_ref[...]

    o_ref[...] += _dot(a_ref[...], w_ref[...])

    @pl.when(kk == pl.num_programs(1) - 1)
    def _():
        o_ref[...] = _rms(o_ref[...], g_ref[...])


def _ffn_out(act, w_fo, h2, g_final):
    s_real, k = act.shape
    d = w_fo.shape[1]
    tm = FFN_OUT_ROW_TILE
    tk = k // FFN_OUT_K_STEPS
    assert tk % LANES == 0
    return pl.pallas_call(
        _ffn_out_kernel, name="ffn_out_final_norm",
        grid=(s_real // tm, k // tk),
        in_specs=[pl.BlockSpec((tm, tk), lambda m, kk: (m, kk)), pl.BlockSpec((tk, d), lambda m, kk: (kk, 0)),
                  pl.BlockSpec((tm, d), lambda m, kk: (m, 0)), pl.BlockSpec((1, d), lambda m, kk: (0, 0))],
        out_specs=pl.BlockSpec((tm, d), lambda m, kk: (m, 0)),
        out_shape=jax.ShapeDtypeStruct((s_real, d), F32),
        compiler_params=_cparams(2),
    )(act, w_fo, h2, g_final)


def kernel(x, positions, meta_tokens, w_in, w_q_up, w_kv_up, w_branch_mla, w_branch_hgrn, w_out, w_ffn_in,
           w_ffn_out, conv_w, conv_b, g_mix_norm, g_q_norm, g_kv_norm, g_hgrn_norm, g_ffn_norm, g_final_norm,
           lb_raw):
    b, s_real, d = x.shape
    assert b == 1 and w_in.shape[0] == 1 and s_real % max(ATTN_QUERY_TILE, FFN_ROW_TILE) == 0
    dt = x.dtype

    prefix = jnp.concatenate([jnp.zeros((PAD_LEN, d), dt), meta_tokens.astype(dt)], axis=0)
    pos = jnp.concatenate([positions[0].astype(jnp.int32) + N_META, jnp.zeros((PAD_LEN,), jnp.int32),
                           jnp.arange(N_META, dtype=jnp.int32)])
    inv = 1.0 / (ROPE_THETA ** (jnp.arange(0, QK_ROPE_DIM, 2, dtype=F32) / QK_ROPE_DIM))
    inv = jnp.concatenate([inv, inv, jnp.zeros((LANES - QK_ROPE_DIM,), F32)])[None, :]

    w_in_t = jnp.swapaxes(w_in, 1, 2)[0]
    w_lat = _w_lat(w_in_t)
    w_kv3 = w_kv_up[0].reshape(KV_LORA_RANK, MLA_HEADS, QK_NOPE_DIM + V_HEAD_DIM)
    w_k = w_kv3[:, :, :QK_NOPE_DIM].reshape(KV_LORA_RANK, -1).astype(BF16)
    w_vt = w_kv3[:, :, QK_NOPE_DIM:].reshape(KV_LORA_RANK, -1).T.astype(BF16)
    w_fo = w_ffn_out[0].astype(BF16)

    cos_t, sin_t = _rope_tables(pos.astype(F32)[:, None], inv)

    u, qn, kvn, kr = _proj_lat(x[0], prefix, g_mix_norm, w_lat, g_q_norm, g_kv_norm, cos_t, sin_t)
    hq = _matmul_cols_t(u, w_in_t, LAT_END, d, BF16, PROJ_COL_TILE, "proj_hgrn_q")
    hf = _matmul_cols_t(u, w_in_t, LAT_END + d, d, F32, PROJ_COL_TILE, "proj_hgrn_forget")
    hrest = _matmul_cols_t(u, w_in_t, LAT_END + 2 * d, 4 * d, BF16, PROJ_COL_TILE, "proj_hgrn_rest")
    q_cat = _q_up(qn, w_q_up, cos_t, sin_t)
    k_slabs = _kv_up(kvn, w_k)
    v_t = _v_up_t(kvn, w_vt)
    o_mla = _attention(q_cat, k_slabs, v_t, kr, s_real)
    o_hgrn = _hgrn(hq, hf, hrest, lb_raw, g_hgrn_norm)
    merged = _merge(o_mla, o_hgrn, w_branch_mla, w_branch_hgrn, hrest, 2 * d)
    h2, u2 = _out_proj(merged, w_out, x[0], prefix, g_ffn_norm)

    act = _ffn_in(u2, w_ffn_in, conv_w, conv_b, s_real)
    out = _ffn_out(act, w_fo, h2, g_final_norm[None, :])
    return out[None]
```

```python
import functools

import jax
import jax.numpy as jnp
from jax import lax
from jax.experimental import pallas as pl
from jax.experimental.pallas import tpu as pltpu

F32 = jnp.float32
BF16 = jnp.bfloat16

N_META = 16
BLOCK = 128
PAD_LEN = BLOCK - N_META
MLA_HEADS = 16
Q_LORA_RANK = 1536
KV_LORA_RANK = 512
QK_NOPE_DIM = 128
QK_ROPE_DIM = 64
QK_HEAD_DIM = QK_NOPE_DIM + QK_ROPE_DIM
V_HEAD_DIM = 128
ROPE_THETA = 10000.0
HGRN_HEADS = 16
HGRN_EXPAND = 128
HGRN_V_DIM = 128
D_FF = 5632
NORM_EPS = 1e-6

LANES = 128
Q_CAT = 2 * LANES
LAT_COLS = Q_LORA_RANK + KV_LORA_RANK + LANES
LOG2E = 1.4426950408889634
VMEM_LIMIT = 52 * 1024 * 1024

ALL_ROWS_TILE = 640
PROJ_COL_TILE = 1024
MERGE_COL_TILE = 512
SLABS_PER_TILE = 8
Q_HEADS_PER_TILE = 4
WEIGHT_PREP_TILE = 512
LATENT_ROW_TILE = 512
OUT_PROJ_ROW_TILE = 256
ATTN_QUERY_TILE = 1024
FFN_ROW_TILE = 1024
FFN_COL_TILE = 512
FFN_OUT_ROW_TILE = 512
FFN_OUT_K_STEPS = 4


def _cparams(n_axes):
    return pltpu.CompilerParams(dimension_semantics=("arbitrary",) * n_axes, vmem_limit_bytes=VMEM_LIMIT)


def _row_tile(rows, target):
    for t in range(target, 0, -LANES):
        if rows % t == 0:
            return t
    raise ValueError(f"no 128-multiple row tile for {rows}")


def _dot(a, b):
    return jnp.dot(a, b, preferred_element_type=F32)


def _dot_nt(a, b):
    return lax.dot_general(a, b, (((1,), (1,)), ((), ())), preferred_element_type=F32)


def _dot_tn(a, b):
    return lax.dot_general(a, b, (((0,), (0,)), ((), ())), preferred_element_type=F32)


def _rms(x, g):
    return x * lax.rsqrt(jnp.mean(x * x, axis=-1, keepdims=True) + NORM_EPS) * g


def _sigmoid(x):
    return 0.5 * jnp.tanh(0.5 * x) + 0.5


def _silu(x):
    return x * _sigmoid(x)


def _rope(x, cos_t, sin_t):
    rot = pltpu.roll(x, 32, 1) - pltpu.roll(x, 96, 1)
    return x * cos_t + rot * sin_t


def _rope_table_kernel(pos_ref, inv_ref, cos_ref, sin_ref):
    ang = pos_ref[...] * inv_ref[...]
    keep = lax.broadcasted_iota(jnp.int32, ang.shape, 1) < QK_ROPE_DIM
    cos_ref[...] = jnp.where(keep, jnp.cos(ang), 0.0)
    sin_ref[...] = jnp.where(keep, jnp.sin(ang), 0.0)


def _rope_tables(pos_f, inv):
    rows = pos_f.shape[0]
    tm = _row_tile(rows, ALL_ROWS_TILE)
    spec = pl.BlockSpec((tm, LANES), lambda i: (i, 0))
    return pl.pallas_call(
        _rope_table_kernel, name="rope_tables",
        grid=(rows // tm,),
        in_specs=[pl.BlockSpec((tm, 1), lambda i: (i, 0)), pl.BlockSpec((1, LANES), lambda i: (0, 0))],
        out_specs=[spec, spec],
        out_shape=[jax.ShapeDtypeStruct((rows, LANES), F32)] * 2,
        compiler_params=_cparams(1),
    )(pos_f, inv)


LAT_END = Q_LORA_RANK + KV_LORA_RANK + QK_ROPE_DIM


def _w_lat_kernel(w_ref, o_ref):
    row = lax.broadcasted_iota(jnp.int32, w_ref.shape, 0)
    o_ref[...] = jnp.where(row < LAT_END, w_ref[...], 0.0).astype(o_ref.dtype)


def _w_lat(w_in_t):
    d = w_in_t.shape[1]
    tk = WEIGHT_PREP_TILE
    return pl.pallas_call(
        _w_lat_kernel, name="w_latent_cast",
        grid=(d // tk,),
        in_specs=[pl.BlockSpec((LAT_COLS, tk), lambda c: (0, c))],
        out_specs=pl.BlockSpec((LAT_COLS, tk), lambda c: (0, c)),
        out_shape=jax.ShapeDtypeStruct((LAT_COLS, d), BF16),
        compiler_params=_cparams(1),
    )(w_in_t)


def _two_source_rows(n_token_tiles, body, token_ref, prefix_ref):
    i = pl.program_id(0)

    @pl.when(i < n_token_tiles)
    def _():
        body(token_ref[...], slice(0, token_ref.shape[0]))

    @pl.when(i == n_token_tiles)
    def _():
        body(prefix_ref[...], slice(0, BLOCK))


def _proj_lat_kernel(x_ref, pre_ref, gm_ref, w_ref, gq_ref, gkv_ref, cos_ref, sin_ref, u_ref, qn_ref, kvn_ref,
                     kr_ref, *, n_token_tiles):
    kv_end = Q_LORA_RANK + KV_LORA_RANK

    def body(h, rows):
        u = _rms(h, gm_ref[...]).astype(u_ref.dtype)
        u_ref[rows, :] = u
        y = _dot_nt(u, w_ref[...])
        qn_ref[rows, :] = _rms(y[:, :Q_LORA_RANK], gq_ref[...]).astype(qn_ref.dtype)
        kvn_ref[rows, :] = _rms(y[:, Q_LORA_RANK:kv_end], gkv_ref[...]).astype(kvn_ref.dtype)
        kr_ref[rows, :] = _rope(y[:, kv_end:], cos_ref[rows, :], sin_ref[rows, :]).astype(kr_ref.dtype)

    _two_source_rows(n_token_tiles, body, x_ref, pre_ref)


def _proj_lat(x2d, prefix, g_mix, w_lat, g_q, g_kv, cos_t, sin_t):
    s_real, d = x2d.shape
    rows = s_real + BLOCK
    tm = LATENT_ROW_TILE
    n_tok = s_real // tm
    row_spec = lambda n: pl.BlockSpec((tm, n), lambda i: (i, 0))
    full_spec = lambda r, n: pl.BlockSpec((r, n), lambda i: (0, 0))
    return pl.pallas_call(
        functools.partial(_proj_lat_kernel, n_token_tiles=n_tok), name="proj_latents",
        grid=(n_tok + 1,),
        in_specs=[pl.BlockSpec((tm, d), lambda i: (jnp.minimum(i, n_tok - 1), 0)), full_spec(BLOCK, d),
                  full_spec(1, d), full_spec(LAT_COLS, d), full_spec(1, Q_LORA_RANK), full_spec(1, KV_LORA_RANK),
                  row_spec(LANES), row_spec(LANES)],
        out_specs=[row_spec(d), row_spec(Q_LORA_RANK), row_spec(KV_LORA_RANK), row_spec(LANES)],
        out_shape=[jax.ShapeDtypeStruct((rows, d), BF16),
                   jax.ShapeDtypeStruct((rows, Q_LORA_RANK), BF16),
                   jax.ShapeDtypeStruct((rows, KV_LORA_RANK), BF16),
                   jax.ShapeDtypeStruct((rows, LANES), BF16)],
        compiler_params=_cparams(1),
    )(x2d, prefix, g_mix, w_lat, g_q, g_kv, cos_t, sin_t)


def _matmul_t_kernel(x_ref, w_ref, o_ref, w_bf):
    @pl.when(pl.program_id(1) == 0)
    def _():
        w_bf[...] = w_ref[...].T.astype(BF16)

    o_ref[...] = _dot(x_ref[...], w_bf[...]).astype(o_ref.dtype)


def _matmul_cols_t(x, w_t, col_start, n_cols, out_dtype, tn, name):
    rows, k = x.shape
    tm = _row_tile(rows, ALL_ROWS_TILE)
    assert col_start % 8 == 0 and n_cols % tn == 0
    return pl.pallas_call(
        _matmul_t_kernel, name=name,
        grid=(n_cols // tn, rows // tm),
        in_specs=[pl.BlockSpec((tm, k), lambda n, m: (m, 0)),
                  pl.BlockSpec((pl.Element(tn), pl.Element(k)),
                               lambda n, m: (pl.multiple_of(col_start + n * tn, 8), 0))],
        out_specs=pl.BlockSpec((tm, tn), lambda n, m: (m, n)),
        out_shape=jax.ShapeDtypeStruct((rows, n_cols), out_dtype),
        scratch_shapes=[pltpu.VMEM((k, tn), BF16)],
        compiler_params=_cparams(2),
    )(x, w_t)


def _kv_up_kernel(x_ref, w_ref, o_ref):
    y = _dot(x_ref[...], w_ref[...])
    for hh in range(o_ref.shape[0]):
        o_ref[hh] = y[:, hh * LANES:(hh + 1) * LANES].astype(o_ref.dtype)


V_EXT = V_HEAD_DIM + 16


def _v_up_t_kernel(x_ref, w_ref, o_ref):
    y_t = _dot_nt(w_ref[...], x_ref[...])
    pad_shape = (V_EXT - V_HEAD_DIM, o_ref.shape[2])
    ones_row = (lax.broadcasted_iota(jnp.int32, pad_shape, 0) == 0).astype(o_ref.dtype)
    for hh in range(o_ref.shape[0]):
        o_ref[hh, :LANES, :] = y_t[hh * LANES:(hh + 1) * LANES, :].astype(o_ref.dtype)
        o_ref[hh, LANES:, :] = ones_row


def _v_up_t(kvn, w_vt):
    rows, k = kvn.shape
    n_slabs = w_vt.shape[0] // LANES
    tm = _row_tile(rows, ALL_ROWS_TILE)
    per_tile = SLABS_PER_TILE
    return pl.pallas_call(
        _v_up_t_kernel, name="v_up_transposed",
        grid=(n_slabs // per_tile, rows // tm),
        in_specs=[pl.BlockSpec((tm, k), lambda n, m: (m, 0)),
                  pl.BlockSpec((per_tile * LANES, k), lambda n, m: (n, 0))],
        out_specs=pl.BlockSpec((per_tile, V_EXT, tm), lambda n, m: (n, 0, m)),
        out_shape=jax.ShapeDtypeStruct((n_slabs, V_EXT, rows), BF16),
        compiler_params=_cparams(2),
    )(kvn, w_vt)


def _kv_up(kvn, w_kv):
    rows, k = kvn.shape
    n_slabs = w_kv.shape[1] // LANES
    tm = _row_tile(rows, ALL_ROWS_TILE)
    per_tile = SLABS_PER_TILE
    return pl.pallas_call(
        _kv_up_kernel, name="kv_up",
        grid=(n_slabs // per_tile, rows // tm),
        in_specs=[pl.BlockSpec((tm, k), lambda n, m: (m, 0)),
                  pl.BlockSpec((k, per_tile * LANES), lambda n, m: (0, n))],
        out_specs=pl.BlockSpec((per_tile, tm, LANES), lambda n, m: (n, m, 0)),
        out_shape=jax.ShapeDtypeStruct((n_slabs, rows, LANES), BF16),
        compiler_params=_cparams(2),
    )(kvn, w_kv)


def _q_up_kernel(x_ref, w_ref, cos_ref, sin_ref, o_ref, w_bf, *, heads_per_tile):
    @pl.when(pl.program_id(1) == 0)
    def _():
        w_bf[...] = jnp.zeros(w_bf.shape, w_bf.dtype)
        for hh in range(heads_per_tile):
            w_bf[:, hh * Q_CAT:hh * Q_CAT + QK_HEAD_DIM] = (
                w_ref[:, hh * QK_HEAD_DIM:(hh + 1) * QK_HEAD_DIM].astype(w_bf.dtype))

    tm = o_ref.shape[0]
    small = tm // 5
    r0 = 0
    for tp in (tm - small, small):
        rows = slice(r0, r0 + tp)
        y = _dot(x_ref[rows, :], w_bf[...]) * (QK_HEAD_DIM ** -0.5 * LOG2E)
        cos_t = cos_ref[rows, :]
        sin_t = sin_ref[rows, :]
        for hh in range(heads_per_tile):
            lo = hh * Q_CAT
            o_ref[rows, lo:lo + LANES] = y[:, lo:lo + LANES].astype(o_ref.dtype)
            o_ref[rows, lo + LANES:lo + Q_CAT] = _rope(y[:, lo + LANES:lo + Q_CAT], cos_t, sin_t).astype(o_ref.dtype)
        r0 += tp


def _q_up(qn, w_q_up, cos_t, sin_t):
    rows, k = qn.shape
    heads_per_tile = Q_HEADS_PER_TILE
    tm = _row_tile(rows, ALL_ROWS_TILE)
    tn = heads_per_tile * Q_CAT
    n_tiles = w_q_up.shape[2] // (heads_per_tile * QK_HEAD_DIM)
    return pl.pallas_call(
        functools.partial(_q_up_kernel, heads_per_tile=heads_per_tile), name="q_up_rope",
        grid=(n_tiles, rows // tm),
        in_specs=[pl.BlockSpec((tm, k), lambda n, m: (m, 0)),
                  pl.BlockSpec((None, k, heads_per_tile * QK_HEAD_DIM), lambda n, m: (0, 0, n)),
                  pl.BlockSpec((tm, LANES), lambda n, m: (m, 0)), pl.BlockSpec((tm, LANES), lambda n, m: (m, 0))],
        out_specs=pl.BlockSpec((tm, tn), lambda n, m: (m, n)),
        out_shape=jax.ShapeDtypeStruct((rows, n_tiles * tn), BF16),
        scratch_shapes=[pltpu.VMEM((k, tn), BF16)],
        compiler_params=_cparams(2),
    )(qn, w_q_up, cos_t, sin_t)


ATTN_KEY_BLOCK = 64
ATTN_PART = 2 * LANES


def _attn_kernel(q_ref, kn_ref, vt_ref, kr_ref, o_ref, kcat_ref, m_ref, acc_ref, sa0_ref, sa1_ref, sb0_ref,
                 sb1_ref, sc0_ref, sc1_ref, spre_ref, p_ref, *, tq, n_real_tiles):
    step_id = pl.program_id(1)
    qi = step_id - 1
    s_real = n_real_tiles * tq
    tk = tq // 2

    @pl.when(step_id == 0)
    def _():
        kcat_ref[:, :LANES] = kn_ref[...]
        kcat_ref[:, LANES:] = kr_ref[...]

    th = ATTN_PART
    parts = tuple(range(0, tq, th))

    def init():
        m_ref[...] = jnp.full(m_ref.shape, -jnp.inf, F32)
        acc_ref[...] = jnp.zeros(acc_ref.shape, F32)

    def update(c0, blocks, pv):
        cols = slice(c0, c0 + th)

        def load(ref, row0, n, visible):
            s = ref[row0:row0 + n, cols]
            return s if visible is None else jnp.where(visible(row0, n), s, -jnp.inf)

        m_prev = m_ref[:, cols]
        m_cur = functools.reduce(
            jnp.maximum, [jnp.max(load(ref, r, n, vis), axis=0, keepdims=True) for ref, r, n, _, vis in blocks])
        m_new = jnp.maximum(m_prev, m_cur)
        alpha = jnp.exp2(m_prev - m_new)
        m_ref[:, cols] = m_new
        for ref, r, n, p_row, vis in blocks:
            for b0 in range(0, n, ATTN_KEY_BLOCK):
                p = jnp.exp2(load(ref, r + b0, ATTN_KEY_BLOCK, vis) - m_new)
                p_ref[p_row + b0:p_row + b0 + ATTN_KEY_BLOCK, cols] = p.astype(p_ref.dtype)
        new = functools.reduce(jnp.add, [_dot(v_t, p_ref[p_row:p_row + n, cols]) for p_row, n, v_t in pv])
        acc_ref[:, cols] = alpha * acc_ref[:, cols] + new

    def finalize():
        acc = acc_ref[...]
        o_ref[...] = (acc[:V_HEAD_DIM, :] / acc[V_HEAD_DIM:V_HEAD_DIM + 1, :]).T.astype(o_ref.dtype)

    def prefix_scores():
        k_pre = kcat_ref[s_real:s_real + BLOCK, :]
        for c0 in parts:
            spre_ref[:, c0:c0 + th] = _dot_nt(k_pre, q_ref[c0:c0 + th, :])

    v_pre_t = vt_ref[:, s_real:s_real + BLOCK]
    pre_row = tk

    def key_ids(row0, n):
        return lax.broadcasted_iota(jnp.int32, (n, th), 0) + row0

    def query_ids(c0, n):
        return lax.broadcasted_iota(jnp.int32, (n, th), 1) + c0

    @pl.when(step_id > 0)
    def _():
        init()
        prefix_scores()

        def raw_scores(dst_ref, j, which=parts):
            k_blk = kcat_ref[pl.ds(pl.multiple_of(j * tk, tk), tk), :]
            for c0 in which:
                dst_ref[:, c0:c0 + th] = _dot_nt(k_blk, q_ref[c0:c0 + th, :])

        def full_update(src_ref, tile, which=parts):
            v_t = vt_ref[:, pl.ds(pl.multiple_of(tile * tk, tk), tk)]
            for c0 in which:
                update(c0, [(src_ref, 0, tk, 0, None)], [(0, tk, v_t)])

        def pair(a, src, dst):
            raw_scores(dst[0], a + 2)
            raw_scores(dst[1], a + 3)
            full_update(src[0], a)
            full_update(src[1], a + 1)

        def diagonal(c0, src_ref, tile):
            q0 = c0 % tk
            n_k = q0 + th
            causal = lambda row0, n: key_ids(row0, n) <= query_ids(q0, n)
            no_pads = lambda row0, n: key_ids(row0, n) >= PAD_LEN
            start = pl.multiple_of(tile * tk, tk)
            update(c0, [(src_ref, 0, n_k, 0, causal), (spre_ref, 0, BLOCK, pre_row, no_pads)],
                   [(0, n_k, vt_ref[:, pl.ds(start, n_k)]), (pre_row, BLOCK, v_pre_t)])

        def own_tiles(src):
            lower = tuple(c0 for c0 in parts if c0 < tk)
            upper = tuple(c0 for c0 in parts if c0 >= tk)
            for c0 in lower:
                diagonal(c0, src[0], 2 * qi)
            full_update(src[0], 2 * qi, upper)
            for c0 in upper:
                diagonal(c0, src[1], 2 * qi + 1)
            finalize()

        buf_a = (sa0_ref, sa1_ref)
        buf_b = (sb0_ref, sb1_ref)
        buf_c = (sc0_ref, sc1_ref)
        raw_scores(buf_a[0], 0)
        raw_scores(buf_a[1], 1)

        def body(i, carry):
            pair(6 * i, buf_a, buf_b)
            pair(6 * i + 2, buf_b, buf_c)
            pair(6 * i + 4, buf_c, buf_a)
            return carry

        lax.fori_loop(0, qi // 3, body, 0)

        @pl.when(qi % 3 == 0)
        def _():
            own_tiles(buf_a)

        @pl.when(qi % 3 == 1)
        def _():
            pair(2 * qi - 2, buf_a, buf_b)
            own_tiles(buf_b)

        @pl.when(qi % 3 == 2)
        def _():
            pair(2 * qi - 4, buf_a, buf_b)
            pair(2 * qi - 2, buf_b, buf_c)
            own_tiles(buf_c)

    @pl.when(step_id == 0)
    def _():
        init()
        prefix_scores()
        for c0 in parts:
            def visible(row0, n, c0=c0):
                key, query = key_ids(row0, n), query_ids(c0, n)
                return (key <= query) & ((key >= PAD_LEN) | (key == query))

            update(c0, [(spre_ref, 0, BLOCK, pre_row, visible)], [(pre_row, BLOCK, v_pre_t)])
        finalize()


def _attention(q_cat, k_slabs, v_t, kr, s_real):
    rows = q_cat.shape[0]
    tq = ATTN_QUERY_TILE
    tk = tq // 2
    n_real_tiles = s_real // tq
    q_tile = lambda h, i: ((i + n_real_tiles) % (n_real_tiles + 1), h)
    return pl.pallas_call(
        functools.partial(_attn_kernel, tq=tq, n_real_tiles=n_real_tiles), name="mla_attention",
        grid=(MLA_HEADS, n_real_tiles + 1),
        in_specs=[pl.BlockSpec((tq, Q_CAT), q_tile),
                  pl.BlockSpec((None, rows, LANES), lambda h, i: (h, 0, 0)),
                  pl.BlockSpec((None, V_EXT, rows), lambda h, i: (h, 0, 0)),
                  pl.BlockSpec((rows, LANES), lambda h, i: (0, 0))],
        out_specs=pl.BlockSpec((tq, V_HEAD_DIM), q_tile),
        out_shape=jax.ShapeDtypeStruct((rows, MLA_HEADS * V_HEAD_DIM), BF16),
        scratch_shapes=[pltpu.VMEM((rows, Q_CAT), BF16), pltpu.VMEM((1, tq), F32),
                        pltpu.VMEM((V_EXT, tq), F32)] + [pltpu.VMEM((tk, tq), F32)] * 6
                       + [pltpu.VMEM((BLOCK, tq), F32), pltpu.VMEM((tk + BLOCK, tq), BF16)],
        compiler_params=_cparams(2),
    )(q_cat, k_slabs, v_t, kr)


def _split3(x):
    x1 = x.astype(BF16)
    r1 = x - x1.astype(F32)
    x2 = r1.astype(BF16)
    x3 = (r1 - x2.astype(F32)).astype(BF16)
    return x1, x2, x3


def _block_row(x, size, j):
    c, n = x.shape
    g = x.reshape(c // size, size, n)[:, j:j + 1, :]
    return jnp.broadcast_to(g, (c // size, size, n)).reshape(c, n)


HGRN_LEVELS = 7
HGRN_GROUP = 16


def _hgrn_head(hq, hf, hi, hg, lb_raw, gain, st, valid, row, lev):
    n = BLOCK
    top = jnp.max(lb_raw, axis=0, keepdims=True)
    e = jnp.exp(lb_raw - top)
    lb = e[0:1, :] / jnp.sum(e, axis=0, keepdims=True)

    f = lb + (1.0 - lb) * _sigmoid(hf)
    f_eff = jnp.where(valid, f, 1.0)
    g = jnp.where(valid, jnp.log2(f), 0.0)
    k = jnp.where(valid, 1.0 - f, 0.0)
    q = _silu(hq.astype(F32))

    col = lax.broadcasted_iota(jnp.int32, (n, n), 1)
    tri = (col <= row).astype(BF16)
    g1, g2, g3 = _split3(g)
    b = _dot(tri, g1) + _dot(tri, g2) + _dot(tri, g3)

    q_bf = q.astype(BF16)
    k_bf = k.astype(BF16)
    a = jnp.where(lev == 0, jnp.sum(q * k, axis=1, keepdims=True), 0.0)
    for level in range(1, HGRN_LEVELS + 1):
        size = 1 << level
        if level == 1:
            w = jnp.where((row & 1) == 1, f_eff, 1.0)
        elif level == 2:
            r4 = row & 3
            up1 = pltpu.roll(g, n - 1, 0)
            dn1 = pltpu.roll(g, 1, 0)
            w = jnp.exp2(jnp.where(r4 == 0, up1, jnp.where(r4 == 1, 0.0, jnp.where(r4 == 2, g, g + dn1))))
        else:
            w = jnp.exp2(-jnp.abs(b - _block_row(b, size, size // 2 - 1)))
        w_bf = w.astype(BF16)
        a_l = _dot_nt(q_bf * w_bf, k_bf * w_bf)
        a = jnp.where(lev == level, a_l, a)

    o = _dot(a.astype(BF16), hi) + _dot_nt((q * jnp.exp2(b)).astype(BF16), st.astype(BF16))
    b_last = b[n - 1:n, :]
    st_new = st * jnp.exp2(b_last) + _dot_tn(hi, (k * jnp.exp2(b_last - b)).astype(BF16))
    out = _rms(o, gain) * _silu(hg.astype(F32))
    return out, st_new


def _hgrn_kernel(hq_ref, hf_ref, hi_ref, hg_ref, lb_ref, g_ref, o_ref, st_ref):
    c = pl.program_id(0)
    n = BLOCK

    @pl.when(c == 0)
    def _():
        st_ref[...] = jnp.zeros(st_ref.shape, F32)

    row = lax.broadcasted_iota(jnp.int32, (n, n), 0)
    col = lax.broadcasted_iota(jnp.int32, (n, n), 1)
    valid = (c > 0) | (row >= PAD_LEN)
    x = row ^ col
    lev = functools.reduce(jnp.add, [(x >= (1 << i)).astype(jnp.int32) for i in range(HGRN_LEVELS)])
    lev = jnp.where(col > row, -1, lev)
    gain = g_ref[...]

    def head_group(i, carry):
        for hh in range(HGRN_GROUP):
            head = HGRN_GROUP * i + hh
            cols = pl.ds(pl.multiple_of(head * BLOCK, BLOCK), BLOCK)
            out, st_new = _hgrn_head(hq_ref[:, cols], hf_ref[:, cols], hi_ref[:, cols], hg_ref[:, cols],
                                     lb_ref[:, cols], gain, st_ref[head], valid, row, lev)
            st_ref[head] = st_new
            o_ref[:, cols] = out.astype(o_ref.dtype)
        return carry

    lax.fori_loop(0, HGRN_HEADS // HGRN_GROUP, head_group, 0)


def _hgrn(hq, hf, hrest, lb_raw, g_hgrn):
    rows, width = hq.shape
    n_chunks = rows // BLOCK
    blk = lambda off: pl.BlockSpec((BLOCK, width), lambda c: ((c + n_chunks - 1) % n_chunks, off))
    return pl.pallas_call(
        _hgrn_kernel, name="hgrn2",
        grid=(n_chunks,),
        in_specs=[blk(0), blk(0), blk(0), blk(1),
                  pl.BlockSpec((lb_raw.shape[0], width), lambda c: (0, 0)),
                  pl.BlockSpec((1, HGRN_V_DIM), lambda c: (0, 0))],
        out_specs=blk(0),
        out_shape=jax.ShapeDtypeStruct((rows, width), BF16),
        scratch_shapes=[pltpu.VMEM((HGRN_HEADS, HGRN_V_DIM, HGRN_EXPAND), F32)],
        compiler_params=_cparams(1),
    )(hq, hf, hrest, hrest, lb_raw, g_hgrn)


def _merge_kernel(om_ref, oh_ref, wa_ref, wb_ref, ga_ref, gb_ref, o_ref, wa_bf, wb_bf):
    @pl.when(pl.program_id(1) == 0)
    def _():
        wa_bf[...] = wa_ref[...].astype(BF16)
        wb_bf[...] = wb_ref[...].astype(BF16)

    tm = o_ref.shape[0]
    small = tm // 5
    r0 = 0
    for tp in (tm - small, small):
        rows = slice(r0, r0 + tp)
        a = _dot(om_ref[rows, :], wa_bf[...])
        b = _dot(oh_ref[rows, :], wb_bf[...])
        o = _sigmoid(ga_ref[rows, :].astype(F32)) * a + _sigmoid(gb_ref[rows, :].astype(F32)) * b
        o_ref[rows, :] = o.astype(o_ref.dtype)
        r0 += tp


def _merge(o_mla, o_hgrn, w_a, w_b, hrest, gate_off):
    rows, k = o_mla.shape
    d = w_a.shape[2]
    tm = _row_tile(rows, ALL_ROWS_TILE)
    tn = MERGE_COL_TILE
    ga_off = gate_off // tn
    gb_off = (gate_off + d) // tn
    w_spec = pl.BlockSpec((None, k, tn), lambda n, m: (0, 0, n))
    return pl.pallas_call(
        _merge_kernel, name="branch_merge",
        grid=(d // tn, rows // tm),
        in_specs=[pl.BlockSpec((tm, k), lambda n, m: (m, 0)), pl.BlockSpec((tm, k), lambda n, m: (m, 0)),
                  w_spec, w_spec,
                  pl.BlockSpec((tm, tn), lambda n, m: (m, n + ga_off)),
                  pl.BlockSpec((tm, tn), lambda n, m: (m, n + gb_off))],
        out_specs=pl.BlockSpec((tm, tn), lambda n, m: (m, n)),
        out_shape=jax.ShapeDtypeStruct((rows, d), BF16),
        scratch_shapes=[pltpu.VMEM((k, tn), BF16), pltpu.VMEM((k, tn), BF16)],
        compiler_params=_cparams(2),
    )(o_mla, o_hgrn, w_a, w_b, hrest, hrest)


def _out_proj_kernel(x_ref, pre_ref, mg_ref, w_ref, g_ref, h2_ref, u2_ref, w_bf, *, n_token_tiles):
    @pl.when(pl.program_id(0) == 0)
    def _():
        w_bf[...] = w_ref[...].astype(BF16)

    def body(h, rows):
        h2 = h + _dot(mg_ref[rows, :], w_bf[...])
        h2_ref[rows, :] = h2
        u2_ref[rows, :] = _rms(h2, g_ref[...]).astype(u2_ref.dtype)

    _two_source_rows(n_token_tiles, body, x_ref, pre_ref)


def _out_proj(merged, w_out, x2d, prefix, g_ffn):
    s_real, d = x2d.shape
    rows = s_real + BLOCK
    tm = OUT_PROJ_ROW_TILE
    n_tok = s_real // tm
    row_spec = pl.BlockSpec((tm, d), lambda i: (i, 0))
    return pl.pallas_call(
        functools.partial(_out_proj_kernel, n_token_tiles=n_tok), name="mix_out_proj",
        grid=(n_tok + 1,),
        in_specs=[pl.BlockSpec((tm, d), lambda i: (jnp.minimum(i, n_tok - 1), 0)),
                  pl.BlockSpec((BLOCK, d), lambda i: (0, 0)), row_spec,
                  pl.BlockSpec((None, d, d), lambda i: (0, 0, 0), pipeline_mode=pl.Buffered(1)),
                  pl.BlockSpec((1, d), lambda i: (0, 0))],
        out_specs=[row_spec, row_spec],
        out_shape=[jax.ShapeDtypeStruct((rows, d), F32), jax.ShapeDtypeStruct((rows, d), BF16)],
        scratch_shapes=[pltpu.VMEM((d, d), BF16)],
        compiler_params=_cparams(1),
    )(x2d, prefix, merged, w_out, g_ffn)


HALO = 16


def _ffn_in_kernel(u_ref, halo_ref, wg_ref, wu_ref, cw_ref, cb_ref, o_ref, wg_bf, wu_bf, tail_ref, *, part_rows):
    @pl.when(pl.program_id(1) == 0)
    def _():
        wg_bf[...] = wg_ref[...].astype(BF16)
        wu_bf[...] = wu_ref[...].astype(BF16)
        tail_ref[...] = _dot(halo_ref[...], wg_bf[...])[HALO - 8:HALO, :]

    wg = wg_bf[...]
    wu = wu_bf[...]
    cw = cw_ref[...]
    cb = cb_ref[...]
    tail = tail_ref[6:8, :]
    r0 = 0
    for tp in part_rows:
        row = lax.broadcasted_iota(jnp.int32, (tp, o_ref.shape[1]), 0)
        u = u_ref[r0:r0 + tp, :]
        gate = _dot(u, wg)
        up = _dot(u, wu)
        back1 = jnp.where(row == 0, tail[1:2, :], pltpu.roll(gate, 1, 0))
        back2 = jnp.where(row == 0, tail[0:1, :], jnp.where(row == 1, tail[1:2, :], pltpu.roll(gate, 2, 0)))
        conv = cw[0:1, :] * back2 + cw[1:2, :] * back1 + cw[2:3, :] * gate + cb
        o_ref[r0:r0 + tp, :] = (_silu(conv) * up).astype(o_ref.dtype)
        tail = gate[tp - 2:tp, :]
        r0 += tp
    tail_ref[...] = gate[tp - 8:tp, :]


def _ffn_in(u2, w_fi, conv_w, conv_b, s_real):
    rows, d = u2.shape
    tm = FFN_ROW_TILE
    tn = FFN_COL_TILE
    up_off = D_FF // tn
    halo_map = lambda n, m: (rows // HALO - 1, 0)
    return pl.pallas_call(
        functools.partial(_ffn_in_kernel, part_rows=(tm // 2, tm // 2)), name="ffn_in_conv_gate",
        grid=(D_FF // tn, s_real // tm),
        in_specs=[pl.BlockSpec((tm, d), lambda n, m: (m, 0)), pl.BlockSpec((HALO, d), halo_map),
                  pl.BlockSpec((None, d, tn), lambda n, m: (0, 0, n)),
                  pl.BlockSpec((None, d, tn), lambda n, m: (0, 0, n + up_off)),
                  pl.BlockSpec((None, conv_w.shape[1], tn), lambda n, m: (0, 0, n)),
                  pl.BlockSpec((1, tn), lambda n, m: (0, n))],
        out_specs=pl.BlockSpec((tm, tn), lambda n, m: (m, n)),
        out_shape=jax.ShapeDtypeStruct((s_real, D_FF), BF16),
        scratch_shapes=[pltpu.VMEM((d, tn), BF16), pltpu.VMEM((d, tn), BF16), pltpu.VMEM((8, tn), F32)],
        compiler_params=_cparams(2),
    )(u2, u2, w_fi, w_fi, conv_w, conv_b)


def _ffn_out_kernel(a_ref, w_ref, h2_ref, g_ref, o_ref):
    kk = pl.program_id(1)

    @pl.when(kk == 0)
    def _():
        o_ref[...] = h2_ref[...]

    o_ref[...] += _dot(a_ref[...], w_ref[...])

    @pl.when(kk == pl.num_programs(1) - 1)
    def _():
        o_ref[...] = _rms(o_ref[...], g_ref[...])


def _ffn_out(act, w_fo, h2, g_final):
    s_real, k = act.shape
    d = w_fo.shape[1]
    tm = FFN_OUT_ROW_TILE
    tk = k // FFN_OUT_K_STEPS
    assert tk % LANES == 0
    return pl.pallas_call(
        _ffn_out_kernel, name="ffn_out_final_norm",
        grid=(s_real // tm, k // tk),
        in_specs=[pl.BlockSpec((tm, tk), lambda m, kk: (m, kk)), pl.BlockSpec((tk, d), lambda m, kk: (kk, 0)),
                  pl.BlockSpec((tm, d), lambda m, kk: (m, 0)), pl.BlockSpec((1, d), lambda m, kk: (0, 0))],
        out_specs=pl.BlockSpec((tm, d), lambda m, kk: (m, 0)),
        out_shape=jax.ShapeDtypeStruct((s_real, d), F32),
        compiler_params=_cparams(2),
    )(act, w_fo, h2, g_final)


def kernel(x, positions, meta_tokens, w_in, w_q_up, w_kv_up, w_branch_mla, w_branch_hgrn, w_out, w_ffn_in,
           w_ffn_out, conv_w, conv_b, g_mix_norm, g_q_norm, g_kv_norm, g_hgrn_norm, g_ffn_norm, g_final_norm,
           lb_raw):
    b, s_real, d = x.shape
    assert b == 1 and w_in.shape[0] == 1 and s_real % max(ATTN_QUERY_TILE, FFN_ROW_TILE) == 0
    dt = x.dtype

    prefix = jnp.concatenate([jnp.zeros((PAD_LEN, d), dt), meta_tokens.astype(dt)], axis=0)
    pos = jnp.concatenate([positions[0].astype(jnp.int32) + N_META, jnp.zeros((PAD_LEN,), jnp.int32),
                           jnp.arange(N_META, dtype=jnp.int32)])
    inv = 1.0 / (ROPE_THETA ** (jnp.arange(0, QK_ROPE_DIM, 2, dtype=F32) / QK_ROPE_DIM))
    inv = jnp.concatenate([inv, inv, jnp.zeros((LANES - QK_ROPE_DIM,), F32)])[None, :]

    w_in_t = jnp.swapaxes(w_in, 1, 2)[0]
    w_lat = _w_lat(w_in_t)
    w_kv3 = w_kv_up[0].reshape(KV_LORA_RANK, MLA_HEADS, QK_NOPE_DIM + V_HEAD_DIM)
    w_k = w_kv3[:, :, :QK_NOPE_DIM].reshape(KV_LORA_RANK, -1).astype(BF16)
    w_vt = w_kv3[:, :, QK_NOPE_DIM:].reshape(KV_LORA_RANK, -1).T.astype(BF16)
    w_fo = w_ffn_out[0].astype(BF16)

    cos_t, sin_t = _rope_tables(pos.astype(F32)[:, None], inv)

    u, qn, kvn, kr = _proj_lat(x[0], prefix, g_mix_norm, w_lat, g_q_norm, g_kv_norm, cos_t, sin_t)
    hq = _matmul_cols_t(u, w_in_t, LAT_END, d, BF16, PROJ_COL_TILE, "proj_hgrn_q")
    hf = _matmul_cols_t(u, w_in_t, LAT_END + d, d, F32, PROJ_COL_TILE, "proj_hgrn_forget")
    hrest = _matmul_cols_t(u, w_in_t, LAT_END + 2 * d, 4 * d, BF16, PROJ_COL_TILE, "proj_hgrn_rest")
    q_cat = _q_up(qn, w_q_up, cos_t, sin_t)
    k_slabs = _kv_up(kvn, w_k)
    v_t = _v_up_t(kvn, w_vt)
    o_mla = _attention(q_cat, k_slabs, v_t, kr, s_real)
    o_hgrn = _hgrn(hq, hf, hrest, lb_raw, g_hgrn_norm)
    merged = _merge(o_mla, o_hgrn, w_branch_mla, w_branch_hgrn, hrest, 2 * d)
    h2, u2 = _out_proj(merged, w_out, x[0], prefix, g_ffn_norm)

    act = _ffn_in(u2, w_ffn_in, conv_w, conv_b, s_real)
    out = _ffn_out(act, w_fo, h2, g_final_norm[None, :])
    return out[None]
```

```python
import functools

import jax
import jax.numpy as jnp
from jax import lax
from jax.experimental import pallas as pl
from jax.experimental.pallas import tpu as pltpu

F32 = jnp.float32
BF16 = jnp.bfloat16

N_META = 16
BLOCK = 128
PAD_LEN = BLOCK - N_META
MLA_HEADS = 16
Q_LORA_RANK = 1536
KV_LORA_RANK = 512
QK_NOPE_DIM = 128
QK_ROPE_DIM = 64
QK_HEAD_DIM = QK_NOPE_DIM + QK_ROPE_DIM
V_HEAD_DIM = 128
ROPE_THETA = 10000.0
HGRN_HEADS = 16
HGRN_EXPAND = 128
HGRN_V_DIM = 128
D_FF = 5632
NORM_EPS = 1e-6

LANES = 128
Q_CAT = 2 * LANES
LAT_COLS = Q_LORA_RANK + KV_LORA_RANK + LANES
LOG2E = 1.4426950408889634
VMEM_LIMIT = 52 * 1024 * 1024

ALL_ROWS_TILE = 640
PROJ_COL_TILE = 1024
MERGE_COL_TILE = 512
SLABS_PER_TILE = 8
Q_HEADS_PER_TILE = 4
WEIGHT_PREP_TILE = 512
LATENT_ROW_TILE = 512
OUT_PROJ_ROW_TILE = 256
ATTN_QUERY_TILE = 1024
FFN_ROW_TILE = 1024
FFN_COL_TILE = 512
FFN_OUT_ROW_TILE = 512
FFN_OUT_K_STEPS = 2


def _cparams(n_axes):
    return pltpu.CompilerParams(dimension_semantics=("arbitrary",) * n_axes, vmem_limit_bytes=VMEM_LIMIT)


def _row_tile(rows, target):
    for t in range(target, 0, -LANES):
        if rows % t == 0:
            return t
    raise ValueError(f"no 128-multiple row tile for {rows}")


def _dot(a, b):
    return jnp.dot(a, b, preferred_element_type=F32)


def _dot_nt(a, b):
    return lax.dot_general(a, b, (((1,), (1,)), ((), ())), preferred_element_type=F32)


def _dot_tn(a, b):
    return lax.dot_general(a, b, (((0,), (0,)), ((), ())), preferred_element_type=F32)


def _rms(x, g):
    return x * lax.rsqrt(jnp.mean(x * x, axis=-1, keepdims=True) + NORM_EPS) * g


def _sigmoid(x):
    return 0.5 * jnp.tanh(0.5 * x) + 0.5


def _silu(x):
    return x * _sigmoid(x)


def _rope(x, cos_t, sin_t):
    rot = pltpu.roll(x, 32, 1) - pltpu.roll(x, 96, 1)
    return x * cos_t + rot * sin_t


def _rope_table_kernel(pos_ref, inv_ref, cos_ref, sin_ref):
    ang = pos_ref[...] * inv_ref[...]
    keep = lax.broadcasted_iota(jnp.int32, ang.shape, 1) < QK_ROPE_DIM
    cos_ref[...] = jnp.where(keep, jnp.cos(ang), 0.0)
    sin_ref[...] = jnp.where(keep, jnp.sin(ang), 0.0)


def _rope_tables(pos_f, inv):
    rows = pos_f.shape[0]
    tm = _row_tile(rows, ALL_ROWS_TILE)
    spec = pl.BlockSpec((tm, LANES), lambda i: (i, 0))
    return pl.pallas_call(
        _rope_table_kernel, name="rope_tables",
        grid=(rows // tm,),
        in_specs=[pl.BlockSpec((tm, 1), lambda i: (i, 0)), pl.BlockSpec((1, LANES), lambda i: (0, 0))],
        out_specs=[spec, spec],
        out_shape=[jax.ShapeDtypeStruct((rows, LANES), F32)] * 2,
        compiler_params=_cparams(1),
    )(pos_f, inv)


LAT_END = Q_LORA_RANK + KV_LORA_RANK + QK_ROPE_DIM


def _w_lat_kernel(w_ref, o_ref):
    row = lax.broadcasted_iota(jnp.int32, w_ref.shape, 0)
    o_ref[...] = jnp.where(row < LAT_END, w_ref[...], 0.0).astype(o_ref.dtype)


def _w_lat(w_in_t):
    d = w_in_t.shape[1]
    tk = WEIGHT_PREP_TILE
    return pl.pallas_call(
        _w_lat_kernel, name="w_latent_cast",
        grid=(d // tk,),
        in_specs=[pl.BlockSpec((LAT_COLS, tk), lambda c: (0, c))],
        out_specs=pl.BlockSpec((LAT_COLS, tk), lambda c: (0, c)),
        out_shape=jax.ShapeDtypeStruct((LAT_COLS, d), BF16),
        compiler_params=_cparams(1),
    )(w_in_t)


def _two_source_rows(n_token_tiles, body, token_ref, prefix_ref):
    i = pl.program_id(0)

    @pl.when(i < n_token_tiles)
    def _():
        body(token_ref[...], slice(0, token_ref.shape[0]))

    @pl.when(i == n_token_tiles)
    def _():
        body(prefix_ref[...], slice(0, BLOCK))


def _proj_lat_kernel(x_ref, pre_ref, gm_ref, w_ref, gq_ref, gkv_ref, cos_ref, sin_ref, u_ref, qn_ref, kvn_ref,
                     kr_ref, *, n_token_tiles):
    kv_end = Q_LORA_RANK + KV_LORA_RANK

    def body(h, rows):
        u = _rms(h, gm_ref[...]).astype(u_ref.dtype)
        u_ref[rows, :] = u
        y = _dot_nt(u, w_ref[...])
        qn_ref[rows, :] = _rms(y[:, :Q_LORA_RANK], gq_ref[...]).astype(qn_ref.dtype)
        kvn_ref[rows, :] = _rms(y[:, Q_LORA_RANK:kv_end], gkv_ref[...]).astype(kvn_ref.dtype)
        kr_ref[rows, :] = _rope(y[:, kv_end:], cos_ref[rows, :], sin_ref[rows, :]).astype(kr_ref.dtype)

    _two_source_rows(n_token_tiles, body, x_ref, pre_ref)


def _proj_lat(x2d, prefix, g_mix, w_lat, g_q, g_kv, cos_t, sin_t):
    s_real, d = x2d.shape
    rows = s_real + BLOCK
    tm = LATENT_ROW_TILE
    n_tok = s_real // tm
    row_spec = lambda n: pl.BlockSpec((tm, n), lambda i: (i, 0))
    full_spec = lambda r, n: pl.BlockSpec((r, n), lambda i: (0, 0))
    return pl.pallas_call(
        functools.partial(_proj_lat_kernel, n_token_tiles=n_tok), name="proj_latents",
        grid=(n_tok + 1,),
        in_specs=[pl.BlockSpec((tm, d), lambda i: (jnp.minimum(i, n_tok - 1), 0)), full_spec(BLOCK, d),
                  full_spec(1, d), full_spec(LAT_COLS, d), full_spec(1, Q_LORA_RANK), full_spec(1, KV_LORA_RANK),
                  row_spec(LANES), row_spec(LANES)],
        out_specs=[row_spec(d), row_spec(Q_LORA_RANK), row_spec(KV_LORA_RANK), row_spec(LANES)],
        out_shape=[jax.ShapeDtypeStruct((rows, d), BF16),
                   jax.ShapeDtypeStruct((rows, Q_LORA_RANK), BF16),
                   jax.ShapeDtypeStruct((rows, KV_LORA_RANK), BF16),
                   jax.ShapeDtypeStruct((rows, LANES), BF16)],
        compiler_params=_cparams(1),
    )(x2d, prefix, g_mix, w_lat, g_q, g_kv, cos_t, sin_t)


def _matmul_t_kernel(x_ref, w_ref, o_ref, w_bf):
    @pl.when(pl.program_id(1) == 0)
    def _():
        w_bf[...] = w_ref[...].T.astype(BF16)

    o_ref[...] = _dot(x_ref[...], w_bf[...]).astype(o_ref.dtype)


def _matmul_cols_t(x, w_t, col_start, n_cols, out_dtype, tn, name):
    rows, k = x.shape
    tm = _row_tile(rows, ALL_ROWS_TILE)
    assert col_start % 8 == 0 and n_cols % tn == 0
    return pl.pallas_call(
        _matmul_t_kernel, name=name,
        grid=(n_cols // tn, rows // tm),
        in_specs=[pl.BlockSpec((tm, k), lambda n, m: (m, 0)),
                  pl.BlockSpec((pl.Element(tn), pl.Element(k)),
                               lambda n, m: (pl.multiple_of(col_start + n * tn, 8), 0))],
        out_specs=pl.BlockSpec((tm, tn), lambda n, m: (m, n)),
        out_shape=jax.ShapeDtypeStruct((rows, n_cols), out_dtype),
        scratch_shapes=[pltpu.VMEM((k, tn), BF16)],
        compiler_params=_cparams(2),
    )(x, w_t)


def _kv_up_kernel(x_ref, w_ref, o_ref):
    y = _dot(x_ref[...], w_ref[...])
    for hh in range(o_ref.shape[0]):
        o_ref[hh] = y[:, hh * LANES:(hh + 1) * LANES].astype(o_ref.dtype)


V_EXT = V_HEAD_DIM + 16


def _v_up_t_kernel(x_ref, w_ref, o_ref):
    y_t = _dot_nt(w_ref[...], x_ref[...])
    pad_shape = (V_EXT - V_HEAD_DIM, o_ref.shape[2])
    ones_row = (lax.broadcasted_iota(jnp.int32, pad_shape, 0) == 0).astype(o_ref.dtype)
    for hh in range(o_ref.shape[0]):
        o_ref[hh, :LANES, :] = y_t[hh * LANES:(hh + 1) * LANES, :].astype(o_ref.dtype)
        o_ref[hh, LANES:, :] = ones_row


def _v_up_t(kvn, w_vt):
    rows, k = kvn.shape
    n_slabs = w_vt.shape[0] // LANES
    tm = _row_tile(rows, ALL_ROWS_TILE)
    per_tile = SLABS_PER_TILE
    return pl.pallas_call(
        _v_up_t_kernel, name="v_up_transposed",
        grid=(n_slabs // per_tile, rows // tm),
        in_specs=[pl.BlockSpec((tm, k), lambda n, m: (m, 0)),
                  pl.BlockSpec((per_tile * LANES, k), lambda n, m: (n, 0))],
        out_specs=pl.BlockSpec((per_tile, V_EXT, tm), lambda n, m: (n, 0, m)),
        out_shape=jax.ShapeDtypeStruct((n_slabs, V_EXT, rows), BF16),
        compiler_params=_cparams(2),
    )(kvn, w_vt)


def _kv_up(kvn, w_kv):
    rows, k = kvn.shape
    n_slabs = w_kv.shape[1] // LANES
    tm = _row_tile(rows, ALL_ROWS_TILE)
    per_tile = SLABS_PER_TILE
    return pl.pallas_call(
        _kv_up_kernel, name="kv_up",
        grid=(n_slabs // per_tile, rows // tm),
        in_specs=[pl.BlockSpec((tm, k), lambda n, m: (m, 0)),
                  pl.BlockSpec((k, per_tile * LANES), lambda n, m: (0, n))],
        out_specs=pl.BlockSpec((per_tile, tm, LANES), lambda n, m: (n, m, 0)),
        out_shape=jax.ShapeDtypeStruct((n_slabs, rows, LANES), BF16),
        compiler_params=_cparams(2),
    )(kvn, w_kv)


def _q_up_kernel(x_ref, w_ref, cos_ref, sin_ref, o_ref, w_bf, *, heads_per_tile):
    @pl.when(pl.program_id(1) == 0)
    def _():
        w_bf[...] = jnp.zeros(w_bf.shape, w_bf.dtype)
        for hh in range(heads_per_tile):
            w_bf[:, hh * Q_CAT:hh * Q_CAT + QK_HEAD_DIM] = (
                w_ref[:, hh * QK_HEAD_DIM:(hh + 1) * QK_HEAD_DIM].astype(w_bf.dtype))

    tm = o_ref.shape[0]
    small = tm // 5
    r0 = 0
    for tp in (tm - small, small):
        rows = slice(r0, r0 + tp)
        y = _dot(x_ref[rows, :], w_bf[...]) * (QK_HEAD_DIM ** -0.5 * LOG2E)
        cos_t = cos_ref[rows, :]
        sin_t = sin_ref[rows, :]
        for hh in range(heads_per_tile):
            lo = hh * Q_CAT
            o_ref[rows, lo:lo + LANES] = y[:, lo:lo + LANES].astype(o_ref.dtype)
            o_ref[rows, lo + LANES:lo + Q_CAT] = _rope(y[:, lo + LANES:lo + Q_CAT], cos_t, sin_t).astype(o_ref.dtype)
        r0 += tp


def _q_up(qn, w_q_up, cos_t, sin_t):
    rows, k = qn.shape
    heads_per_tile = Q_HEADS_PER_TILE
    tm = _row_tile(rows, ALL_ROWS_TILE)
    tn = heads_per_tile * Q_CAT
    n_tiles = w_q_up.shape[2] // (heads_per_tile * QK_HEAD_DIM)
    return pl.pallas_call(
        functools.partial(_q_up_kernel, heads_per_tile=heads_per_tile), name="q_up_rope",
        grid=(n_tiles, rows // tm),
        in_specs=[pl.BlockSpec((tm, k), lambda n, m: (m, 0)),
                  pl.BlockSpec((None, k, heads_per_tile * QK_HEAD_DIM), lambda n, m: (0, 0, n)),
                  pl.BlockSpec((tm, LANES), lambda n, m: (m, 0)), pl.BlockSpec((tm, LANES), lambda n, m: (m, 0))],
        out_specs=pl.BlockSpec((tm, tn), lambda n, m: (m, n)),
        out_shape=jax.ShapeDtypeStruct((rows, n_tiles * tn), BF16),
        scratch_shapes=[pltpu.VMEM((k, tn), BF16)],
        compiler_params=_cparams(2),
    )(qn, w_q_up, cos_t, sin_t)


ATTN_KEY_BLOCK = 64
ATTN_PART = 2 * LANES


def _attn_kernel(q_ref, kn_ref, vt_ref, kr_ref, o_ref, kcat_ref, m_ref, acc_ref, sa0_ref, sa1_ref, sb0_ref,
                 sb1_ref, sc0_ref, sc1_ref, spre_ref, p_ref, *, tq, n_real_tiles):
    step_id = pl.program_id(1)
    qi = step_id - 1
    s_real = n_real_tiles * tq
    tk = tq // 2

    @pl.when(step_id == 0)
    def _():
        kcat_ref[:, :LANES] = kn_ref[...]
        kcat_ref[:, LANES:] = kr_ref[...]

    th = ATTN_PART
    parts = tuple(range(0, tq, th))

    def init():
        m_ref[...] = jnp.full(m_ref.shape, -jnp.inf, F32)
        acc_ref[...] = jnp.zeros(acc_ref.shape, F32)

    def update(c0, blocks, pv):
        cols = slice(c0, c0 + th)

        def load(ref, row0, n, visible):
            s = ref[row0:row0 + n, cols]
            return s if visible is None else jnp.where(visible(row0, n), s, -jnp.inf)

        m_prev = m_ref[:, cols]
        m_cur = functools.reduce(
            jnp.maximum, [jnp.max(load(ref, r, n, vis), axis=0, keepdims=True) for ref, r, n, _, vis in blocks])
        m_new = jnp.maximum(m_prev, m_cur)
        alpha = jnp.exp2(m_prev - m_new)
        m_ref[:, cols] = m_new
        for ref, r, n, p_row, vis in blocks:
            for b0 in range(0, n, ATTN_KEY_BLOCK):
                p = jnp.exp2(load(ref, r + b0, ATTN_KEY_BLOCK, vis) - m_new)
                p_ref[p_row + b0:p_row + b0 + ATTN_KEY_BLOCK, cols] = p.astype(p_ref.dtype)
        new = functools.reduce(jnp.add, [_dot(v_t, p_ref[p_row:p_row + n, cols]) for p_row, n, v_t in pv])
        acc_ref[:, cols] = alpha * acc_ref[:, cols] + new

    def finalize():
        acc = acc_ref[...]
        o_ref[...] = (acc[:V_HEAD_DIM, :] / acc[V_HEAD_DIM:V_HEAD_DIM + 1, :]).T.astype(o_ref.dtype)

    def prefix_scores():
        k_pre = kcat_ref[s_real:s_real + BLOCK, :]
        for c0 in parts:
            spre_ref[:, c0:c0 + th] = _dot_nt(k_pre, q_ref[c0:c0 + th, :])

    v_pre_t = vt_ref[:, s_real:s_real + BLOCK]
    pre_row = tk

    def key_ids(row0, n):
        return lax.broadcasted_iota(jnp.int32, (n, th), 0) + row0

    def query_ids(c0, n):
        return lax.broadcasted_iota(jnp.int32, (n, th), 1) + c0

    @pl.when(step_id > 0)
    def _():
        init()
        prefix_scores()

        def raw_scores(dst_ref, j, which=parts):
            k_blk = kcat_ref[pl.ds(pl.multiple_of(j * tk, tk), tk), :]
            for c0 in which:
                dst_ref[:, c0:c0 + th] = _dot_nt(k_blk, q_ref[c0:c0 + th, :])

        def full_update(src_ref, tile, which=parts):
            v_t = vt_ref[:, pl.ds(pl.multiple_of(tile * tk, tk), tk)]
            for c0 in which:
                update(c0, [(src_ref, 0, tk, 0, None)], [(0, tk, v_t)])

        def pair(a, src, dst):
            raw_scores(dst[0], a + 2)
            raw_scores(dst[1], a + 3)
            full_update(src[0], a)
            full_update(src[1], a + 1)

        def diagonal(c0, src_ref, tile):
            q0 = c0 % tk
            n_k = q0 + th
            causal = lambda row0, n: key_ids(row0, n) <= query_ids(q0, n)
            no_pads = lambda row0, n: key_ids(row0, n) >= PAD_LEN
            start = pl.multiple_of(tile * tk, tk)
            update(c0, [(src_ref, 0, n_k, 0, causal), (spre_ref, 0, BLOCK, pre_row, no_pads)],
                   [(0, n_k, vt_ref[:, pl.ds(start, n_k)]), (pre_row, BLOCK, v_pre_t)])

        def own_tiles(src):
            lower = tuple(c0 for c0 in parts if c0 < tk)
            upper = tuple(c0 for c0 in parts if c0 >= tk)
            for c0 in lower:
                diagonal(c0, src[0], 2 * qi)
            full_update(src[0], 2 * qi, upper)
            for c0 in upper:
                diagonal(c0, src[1], 2 * qi + 1)
            finalize()

        buf_a = (sa0_ref, sa1_ref)
        buf_b = (sb0_ref, sb1_ref)
        buf_c = (sc0_ref, sc1_ref)
        raw_scores(buf_a[0], 0)
        raw_scores(buf_a[1], 1)

        def body(i, carry):
            pair(6 * i, buf_a, buf_b)
            pair(6 * i + 2, buf_b, buf_c)
            pair(6 * i + 4, buf_c, buf_a)
            return carry

        lax.fori_loop(0, qi // 3, body, 0)

        @pl.when(qi % 3 == 0)
        def _():
            own_tiles(buf_a)

        @pl.when(qi % 3 == 1)
        def _():
            pair(2 * qi - 2, buf_a, buf_b)
            own_tiles(buf_b)

        @pl.when(qi % 3 == 2)
        def _():
            pair(2 * qi - 4, buf_a, buf_b)
            pair(2 * qi - 2, buf_b, buf_c)
            own_tiles(buf_c)

    @pl.when(step_id == 0)
    def _():
        init()
        prefix_scores()
        for c0 in parts:
            def visible(row0, n, c0=c0):
                key, query = key_ids(row0, n), query_ids(c0, n)
                return (key <= query) & ((key >= PAD_LEN) | (key == query))

            update(c0, [(spre_ref, 0, BLOCK, pre_row, visible)], [(pre_row, BLOCK, v_pre_t)])
        finalize()


def _attention(q_cat, k_slabs, v_t, kr, s_real):
    rows = q_cat.shape[0]
    tq = ATTN_QUERY_TILE
    tk = tq // 2
    n_real_tiles = s_real // tq
    q_tile = lambda h, i: ((i + n_real_tiles) % (n_real_tiles + 1), h)
    return pl.pallas_call(
        functools.partial(_attn_kernel, tq=tq, n_real_tiles=n_real_tiles), name="mla_attention",
        grid=(MLA_HEADS, n_real_tiles + 1),
        in_specs=[pl.BlockSpec((tq, Q_CAT), q_tile),
                  pl.BlockSpec((None, rows, LANES), lambda h, i: (h, 0, 0)),
                  pl.BlockSpec((None, V_EXT, rows), lambda h, i: (h, 0, 0)),
                  pl.BlockSpec((rows, LANES), lambda h, i: (0, 0))],
        out_specs=pl.BlockSpec((tq, V_HEAD_DIM), q_tile),
        out_shape=jax.ShapeDtypeStruct((rows, MLA_HEADS * V_HEAD_DIM), BF16),
        scratch_shapes=[pltpu.VMEM((rows, Q_CAT), BF16), pltpu.VMEM((1, tq), F32),
                        pltpu.VMEM((V_EXT, tq), F32)] + [pltpu.VMEM((tk, tq), F32)] * 6
                       + [pltpu.VMEM((BLOCK, tq), F32), pltpu.VMEM((tk + BLOCK, tq), BF16)],
        compiler_params=_cparams(2),
    )(q_cat, k_slabs, v_t, kr)


def _split3(x):
    x1 = x.astype(BF16)
    r1 = x - x1.astype(F32)
    x2 = r1.astype(BF16)
    x3 = (r1 - x2.astype(F32)).astype(BF16)
    return x1, x2, x3


def _block_row(x, size, j):
    c, n = x.shape
    g = x.reshape(c // size, size, n)[:, j:j + 1, :]
    return jnp.broadcast_to(g, (c // size, size, n)).reshape(c, n)


HGRN_LEVELS = 7
HGRN_GROUP = 16


def _hgrn_head(hq, hf, hi, hg, lb_raw, gain, st, valid, row, lev):
    n = BLOCK
    top = jnp.max(lb_raw, axis=0, keepdims=True)
    e = jnp.exp(lb_raw - top)
    lb = e[0:1, :] / jnp.sum(e, axis=0, keepdims=True)

    f = lb + (1.0 - lb) * _sigmoid(hf)
    f_eff = jnp.where(valid, f, 1.0)
    g = jnp.where(valid, jnp.log2(f), 0.0)
    k = jnp.where(valid, 1.0 - f, 0.0)
    q = _silu(hq.astype(F32))

    col = lax.broadcasted_iota(jnp.int32, (n, n), 1)
    tri = (col <= row).astype(BF16)
    g1, g2, g3 = _split3(g)
    b = _dot(tri, g1) + _dot(tri, g2) + _dot(tri, g3)

    q_bf = q.astype(BF16)
    k_bf = k.astype(BF16)
    a = jnp.where(lev == 0, jnp.sum(q * k, axis=1, keepdims=True), 0.0)
    for level in range(1, HGRN_LEVELS + 1):
        size = 1 << level
        if level == 1:
            w = jnp.where((row & 1) == 1, f_eff, 1.0)
        elif level == 2:
            r4 = row & 3
            up1 = pltpu.roll(g, n - 1, 0)
            dn1 = pltpu.roll(g, 1, 0)
            w = jnp.exp2(jnp.where(r4 == 0, up1, jnp.where(r4 == 1, 0.0, jnp.where(r4 == 2, g, g + dn1))))
        else:
            w = jnp.exp2(-jnp.abs(b - _block_row(b, size, size // 2 - 1)))
        w_bf = w.astype(BF16)
        a_l = _dot_nt(q_bf * w_bf, k_bf * w_bf)
        a = jnp.where(lev == level, a_l, a)

    o = _dot(a.astype(BF16), hi) + _dot_nt((q * jnp.exp2(b)).astype(BF16), st.astype(BF16))
    b_last = b[n - 1:n, :]
    st_new = st * jnp.exp2(b_last) + _dot_tn(hi, (k * jnp.exp2(b_last - b)).astype(BF16))
    out = _rms(o, gain) * _silu(hg.astype(F32))
    return out, st_new


def _hgrn_kernel(hq_ref, hf_ref, hi_ref, hg_ref, lb_ref, g_ref, o_ref, st_ref):
    c = pl.program_id(0)
    n = BLOCK

    @pl.when(c == 0)
    def _():
        st_ref[...] = jnp.zeros(st_ref.shape, F32)

    row = lax.broadcasted_iota(jnp.int32, (n, n), 0)
    col = lax.broadcasted_iota(jnp.int32, (n, n), 1)
    valid = (c > 0) | (row >= PAD_LEN)
    x = row ^ col
    lev = functools.reduce(jnp.add, [(x >= (1 << i)).astype(jnp.int32) for i in range(HGRN_LEVELS)])
    lev = jnp.where(col > row, -1, lev)
    gain = g_ref[...]

    def head_group(i, carry):
        for hh in range(HGRN_GROUP):
            head = HGRN_GROUP * i + hh
            cols = pl.ds(pl.multiple_of(head * BLOCK, BLOCK), BLOCK)
            out, st_new = _hgrn_head(hq_ref[:, cols], hf_ref[:, cols], hi_ref[:, cols], hg_ref[:, cols],
                                     lb_ref[:, cols], gain, st_ref[head], valid, row, lev)
            st_ref[head] = st_new
            o_ref[:, cols] = out.astype(o_ref.dtype)
        return carry

    lax.fori_loop(0, HGRN_HEADS // HGRN_GROUP, head_group, 0)


def _hgrn(hq, hf, hrest, lb_raw, g_hgrn):
    rows, width = hq.shape
    n_chunks = rows // BLOCK
    blk = lambda off: pl.BlockSpec((BLOCK, width), lambda c: ((c + n_chunks - 1) % n_chunks, off))
    return pl.pallas_call(
        _hgrn_kernel, name="hgrn2",
        grid=(n_chunks,),
        in_specs=[blk(0), blk(0), blk(0), blk(1),
                  pl.BlockSpec((lb_raw.shape[0], width), lambda c: (0, 0)),
                  pl.BlockSpec((1, HGRN_V_DIM), lambda c: (0, 0))],
        out_specs=blk(0),
        out_shape=jax.ShapeDtypeStruct((rows, width), BF16),
        scratch_shapes=[pltpu.VMEM((HGRN_HEADS, HGRN_V_DIM, HGRN_EXPAND), F32)],
        compiler_params=_cparams(1),
    )(hq, hf, hrest, hrest, lb_raw, g_hgrn)


def _merge_kernel(om_ref, oh_ref, wa_ref, wb_ref, ga_ref, gb_ref, o_ref, wa_bf, wb_bf):
    @pl.when(pl.program_id(1) == 0)
    def _():
        wa_bf[...] = wa_ref[...].astype(BF16)
        wb_bf[...] = wb_ref[...].astype(BF16)

    tm = o_ref.shape[0]
    small = tm // 5
    r0 = 0
    for tp in (tm - small, small):
        rows = slice(r0, r0 + tp)
        a = _dot(om_ref[rows, :], wa_bf[...])
        b = _dot(oh_ref[rows, :], wb_bf[...])
        o = _sigmoid(ga_ref[rows, :].astype(F32)) * a + _sigmoid(gb_ref[rows, :].astype(F32)) * b
        o_ref[rows, :] = o.astype(o_ref.dtype)
        r0 += tp


def _merge(o_mla, o_hgrn, w_a, w_b, hrest, gate_off):
    rows, k = o_mla.shape
    d = w_a.shape[2]
    tm = _row_tile(rows, ALL_ROWS_TILE)
    tn = MERGE_COL_TILE
    ga_off = gate_off // tn
    gb_off = (gate_off + d) // tn
    w_spec = pl.BlockSpec((None, k, tn), lambda n, m: (0, 0, n))
    return pl.pallas_call(
        _merge_kernel, name="branch_merge",
        grid=(d // tn, rows // tm),
        in_specs=[pl.BlockSpec((tm, k), lambda n, m: (m, 0)), pl.BlockSpec((tm, k), lambda n, m: (m, 0)),
                  w_spec, w_spec,
                  pl.BlockSpec((tm, tn), lambda n, m: (m, n + ga_off)),
                  pl.BlockSpec((tm, tn), lambda n, m: (m, n + gb_off))],
        out_specs=pl.BlockSpec((tm, tn), lambda n, m: (m, n)),
        out_shape=jax.ShapeDtypeStruct((rows, d), BF16),
        scratch_shapes=[pltpu.VMEM((k, tn), BF16), pltpu.VMEM((k, tn), BF16)],
        compiler_params=_cparams(2),
    )(o_mla, o_hgrn, w_a, w_b, hrest, hrest)


def _out_proj_kernel(x_ref, pre_ref, mg_ref, w_ref, g_ref, h2_ref, u2_ref, w_bf, *, n_token_tiles):
    @pl.when(pl.program_id(0) == 0)
    def _():
        w_bf[...] = w_ref[...].astype(BF16)

    def body(h, rows):
        h2 = h + _dot(mg_ref[rows, :], w_bf[...])
        h2_ref[rows, :] = h2
        u2_ref[rows, :] = _rms(h2, g_ref[...]).astype(u2_ref.dtype)

    _two_source_rows(n_token_tiles, body, x_ref, pre_ref)


def _out_proj(merged, w_out, x2d, prefix, g_ffn):
    s_real, d = x2d.shape
    rows = s_real + BLOCK
    tm = OUT_PROJ_ROW_TILE
    n_tok = s_real // tm
    row_spec = pl.BlockSpec((tm, d), lambda i: (i, 0))
    return pl.pallas_call(
        functools.partial(_out_proj_kernel, n_token_tiles=n_tok), name="mix_out_proj",
        grid=(n_tok + 1,),
        in_specs=[pl.BlockSpec((tm, d), lambda i: (jnp.minimum(i, n_tok - 1), 0)),
                  pl.BlockSpec((BLOCK, d), lambda i: (0, 0)), row_spec,
                  pl.BlockSpec((None, d, d), lambda i: (0, 0, 0), pipeline_mode=pl.Buffered(1)),
                  pl.BlockSpec((1, d), lambda i: (0, 0))],
        out_specs=[row_spec, row_spec],
        out_shape=[jax.ShapeDtypeStruct((rows, d), F32), jax.ShapeDtypeStruct((rows, d), BF16)],
        scratch_shapes=[pltpu.VMEM((d, d), BF16)],
        compiler_params=_cparams(1),
    )(x2d, prefix, merged, w_out, g_ffn)


HALO = 16


def _ffn_in_kernel(u_ref, halo_ref, wg_ref, wu_ref, cw_ref, cb_ref, o_ref, wg_bf, wu_bf, tail_ref, *, part_rows):
    @pl.when(pl.program_id(1) == 0)
    def _():
        wg_bf[...] = wg_ref[...].astype(BF16)
        wu_bf[...] = wu_ref[...].astype(BF16)
        tail_ref[...] = _dot(halo_ref[...], wg_bf[...])[HALO - 8:HALO, :]

    wg = wg_bf[...]
    wu = wu_bf[...]
    cw = cw_ref[...]
    cb = cb_ref[...]
    tail = tail_ref[6:8, :]
    r0 = 0
    for tp in part_rows:
        row = lax.broadcasted_iota(jnp.int32, (tp, o_ref.shape[1]), 0)
        u = u_ref[r0:r0 + tp, :]
        gate = _dot(u, wg)
        up = _dot(u, wu)
        back1 = jnp.where(row == 0, tail[1:2, :], pltpu.roll(gate, 1, 0))
        back2 = jnp.where(row == 0, tail[0:1, :], jnp.where(row == 1, tail[1:2, :], pltpu.roll(gate, 2, 0)))
        conv = cw[0:1, :] * back2 + cw[1:2, :] * back1 + cw[2:3, :] * gate + cb
        o_ref[r0:r0 + tp, :] = (_silu(conv) * up).astype(o_ref.dtype)
        tail = gate[tp - 2:tp, :]
        r0 += tp
    tail_ref[...] = gate[tp - 8:tp, :]


def _ffn_in(u2, w_fi, conv_w, conv_b, s_real):
    rows, d = u2.shape
    tm = FFN_ROW_TILE
    tn = FFN_COL_TILE
    up_off = D_FF // tn
    halo_map = lambda n, m: (rows // HALO - 1, 0)
    return pl.pallas_call(
        functools.partial(_ffn_in_kernel, part_rows=(tm // 2, tm // 2)), name="ffn_in_conv_gate",
        grid=(D_FF // tn, s_real // tm),
        in_specs=[pl.BlockSpec((tm, d), lambda n, m: (m, 0)), pl.BlockSpec((HALO, d), halo_map),
                  pl.BlockSpec((None, d, tn), lambda n, m: (0, 0, n)),
                  pl.BlockSpec((None, d, tn), lambda n, m: (0, 0, n + up_off)),
                  pl.BlockSpec((None, conv_w.shape[1], tn), lambda n, m: (0, 0, n)),
                  pl.BlockSpec((1, tn), lambda n, m: (0, n))],
        out_specs=pl.BlockSpec((tm, tn), lambda n, m: (m, n)),
        out_shape=jax.ShapeDtypeStruct((s_real, D_FF), BF16),
        scratch_shapes=[pltpu.VMEM((d, tn), BF16), pltpu.VMEM((d, tn), BF16), pltpu.VMEM((8, tn), F32)],
        compiler_params=_cparams(2),
    )(u2, u2, w_fi, w_fi, conv_w, conv_b)


def _ffn_out_kernel(a_ref, w_ref, h2_ref, g_ref, o_ref):
    kk = pl.program_id(1)

    @pl.when(kk == 0)
    def _():
        o_ref[...] = h2_ref[...]

    o_ref[...] += _dot(a_ref[...], w_ref[...])

    @pl.when(kk == pl.num_programs(1) - 1)
    def _():
        o_ref[...] = _rms(o_ref[...], g_ref[...])


def _ffn_out(act, w_fo, h2, g_final):
    s_real, k = act.shape
    d = w_fo.shape[1]
    tm = FFN_OUT_ROW_TILE
    tk = k // FFN_OUT_K_STEPS
    assert tk % LANES == 0
    return pl.pallas_call(
        _ffn_out_kernel, name="ffn_out_final_norm",
        grid=(s_real // tm, k // tk),
        in_specs=[pl.BlockSpec((tm, tk), lambda m, kk: (m, kk)), pl.BlockSpec((tk, d), lambda m, kk: (kk, 0)),
                  pl.BlockSpec((tm, d), lambda m, kk: (m, 0)), pl.BlockSpec((1, d), lambda m, kk: (0, 0))],
        out_specs=pl.BlockSpec((tm, d), lambda m, kk: (m, 0)),
        out_shape=jax.ShapeDtypeStruct((s_real, d), F32),
        compiler_params=_cparams(2),
    )(act, w_fo, h2, g_final)


def kernel(x, positions, meta_tokens, w_in, w_q_up, w_kv_up, w_branch_mla, w_branch_hgrn, w_out, w_ffn_in,
           w_ffn_out, conv_w, conv_b, g_mix_norm, g_q_norm, g_kv_norm, g_hgrn_norm, g_ffn_norm, g_final_norm,
           lb_raw):
    b, s_real, d = x.shape
    assert b == 1 and w_in.shape[0] == 1 and s_real % max(ATTN_QUERY_TILE, FFN_ROW_TILE) == 0
    dt = x.dtype

    prefix = jnp.concatenate([jnp.zeros((PAD_LEN, d), dt), meta_tokens.astype(dt)], axis=0)
    pos = jnp.concatenate([positions[0].astype(jnp.int32) + N_META, jnp.zeros((PAD_LEN,), jnp.int32),
                           jnp.arange(N_META, dtype=jnp.int32)])
    inv = 1.0 / (ROPE_THETA ** (jnp.arange(0, QK_ROPE_DIM, 2, dtype=F32) / QK_ROPE_DIM))
    inv = jnp.concatenate([inv, inv, jnp.zeros((LANES - QK_ROPE_DIM,), F32)])[None, :]

    w_in_t = jnp.swapaxes(w_in, 1, 2)[0]
    w_lat = _w_lat(w_in_t)
    w_kv3 = w_kv_up[0].reshape(KV_LORA_RANK, MLA_HEADS, QK_NOPE_DIM + V_HEAD_DIM)
    w_k = w_kv3[:, :, :QK_NOPE_DIM].reshape(KV_LORA_RANK, -1).astype(BF16)
    w_vt = w_kv3[:, :, QK_NOPE_DIM:].reshape(KV_LORA_RANK, -1).T.astype(BF16)
    w_fo = w_ffn_out[0].astype(BF16)

    cos_t, sin_t = _rope_tables(pos.astype(F32)[:, None], inv)

    u, qn, kvn, kr = _proj_lat(x[0], prefix, g_mix_norm, w_lat, g_q_norm, g_kv_norm, cos_t, sin_t)
    hq = _matmul_cols_t(u, w_in_t, LAT_END, d, BF16, PROJ_COL_TILE, "proj_hgrn_q")
    hf = _matmul_cols_t(u, w_in_t, LAT_END + d, d, F32, PROJ_COL_TILE, "proj_hgrn_forget")
    hrest = _matmul_cols_t(u, w_in_t, LAT_END + 2 * d, 4 * d, BF16, PROJ_COL_TILE, "proj_hgrn_rest")
    q_cat = _q_up(qn, w_q_up, cos_t, sin_t)
    k_slabs = _kv_up(kvn, w_k)
    v_t = _v_up_t(kvn, w_vt)
    o_mla = _attention(q_cat, k_slabs, v_t, kr, s_real)
    o_hgrn = _hgrn(hq, hf, hrest, lb_raw, g_hgrn_norm)
    merged = _merge(o_mla, o_hgrn, w_branch_mla, w_branch_hgrn, hrest, 2 * d)
    h2, u2 = _out_proj(merged, w_out, x[0], prefix, g_ffn_norm)

    act = _ffn_in(u2, w_ffn_in, conv_w, conv_b, s_real)
    out = _ffn_out(act, w_fo, h2, g_final_norm[None, :])
    return out[None]
```

```python
import functools

import jax
import jax.numpy as jnp
from jax import lax
from jax.experimental import pallas as pl
from jax.experimental.pallas import tpu as pltpu

F32 = jnp.float32
BF16 = jnp.bfloat16

N_META = 16
BLOCK = 128
PAD_LEN = BLOCK - N_META
MLA_HEADS = 16
Q_LORA_RANK = 1536
KV_LORA_RANK = 512
QK_NOPE_DIM = 128
QK_ROPE_DIM = 64
QK_HEAD_DIM = QK_NOPE_DIM + QK_ROPE_DIM
V_HEAD_DIM = 128
ROPE_THETA = 10000.0
HGRN_HEADS = 16
HGRN_EXPAND = 128
HGRN_V_DIM = 128
D_FF = 5632
NORM_EPS = 1e-6

LANES = 128
Q_CAT = 2 * LANES
LAT_COLS = Q_LORA_RANK + KV_LORA_RANK + LANES
LOG2E = 1.4426950408889634
VMEM_LIMIT = 52 * 1024 * 1024

ALL_ROWS_TILE = 640
PROJ_COL_TILE = 1024
MERGE_COL_TILE = 512
SLABS_PER_TILE = 8
Q_HEADS_PER_TILE = 8
WEIGHT_PREP_TILE = 512
LATENT_ROW_TILE = 512
OUT_PROJ_ROW_TILE = 256
ATTN_QUERY_TILE = 1024
FFN_ROW_TILE = 2048
FFN_COL_TILE = 512
FFN_OUT_ROW_TILE = 512
FFN_OUT_K_STEPS = 2


def _cparams(n_axes):
    return pltpu.CompilerParams(dimension_semantics=("arbitrary",) * n_axes, vmem_limit_bytes=VMEM_LIMIT)


def _row_tile(rows, target):
    for t in range(target, 0, -LANES):
        if rows % t == 0:
            return t
    raise ValueError(f"no 128-multiple row tile for {rows}")


def _dot(a, b):
    return jnp.dot(a, b, preferred_element_type=F32)


def _dot_nt(a, b):
    return lax.dot_general(a, b, (((1,), (1,)), ((), ())), preferred_element_type=F32)


def _dot_tn(a, b):
    return lax.dot_general(a, b, (((0,), (0,)), ((), ())), preferred_element_type=F32)


def _rms(x, g):
    return x * lax.rsqrt(jnp.mean(x * x, axis=-1, keepdims=True) + NORM_EPS) * g


def _sigmoid(x):
    return 0.5 * jnp.tanh(0.5 * x) + 0.5


def _silu(x):
    return x * _sigmoid(x)


def _rope(x, cos_t, sin_t):
    rot = pltpu.roll(x, 32, 1) - pltpu.roll(x, 96, 1)
    return x * cos_t + rot * sin_t


def _rope_table_kernel(pos_ref, inv_ref, cos_ref, sin_ref):
    ang = pos_ref[...] * inv_ref[...]
    keep = lax.broadcasted_iota(jnp.int32, ang.shape, 1) < QK_ROPE_DIM
    cos_ref[...] = jnp.where(keep, jnp.cos(ang), 0.0)
    sin_ref[...] = jnp.where(keep, jnp.sin(ang), 0.0)


def _rope_tables(pos_f, inv):
    rows = pos_f.shape[0]
    tm = _row_tile(rows, ALL_ROWS_TILE)
    spec = pl.BlockSpec((tm, LANES), lambda i: (i, 0))
    return pl.pallas_call(
        _rope_table_kernel, name="rope_tables",
        grid=(rows // tm,),
        in_specs=[pl.BlockSpec((tm, 1), lambda i: (i, 0)), pl.BlockSpec((1, LANES), lambda i: (0, 0))],
        out_specs=[spec, spec],
        out_shape=[jax.ShapeDtypeStruct((rows, LANES), F32)] * 2,
        compiler_params=_cparams(1),
    )(pos_f, inv)


LAT_END = Q_LORA_RANK + KV_LORA_RANK + QK_ROPE_DIM


def _w_lat_kernel(w_ref, o_ref):
    row = lax.broadcasted_iota(jnp.int32, w_ref.shape, 0)
    o_ref[...] = jnp.where(row < LAT_END, w_ref[...], 0.0).astype(o_ref.dtype)


def _w_lat(w_in_t):
    d = w_in_t.shape[1]
    tk = WEIGHT_PREP_TILE
    return pl.pallas_call(
        _w_lat_kernel, name="w_latent_cast",
        grid=(d // tk,),
        in_specs=[pl.BlockSpec((LAT_COLS, tk), lambda c: (0, c))],
        out_specs=pl.BlockSpec((LAT_COLS, tk), lambda c: (0, c)),
        out_shape=jax.ShapeDtypeStruct((LAT_COLS, d), BF16),
        compiler_params=_cparams(1),
    )(w_in_t)


def _two_source_rows(n_token_tiles, body, token_ref, prefix_ref):
    i = pl.program_id(0)

    @pl.when(i < n_token_tiles)
    def _():
        body(token_ref[...], slice(0, token_ref.shape[0]))

    @pl.when(i == n_token_tiles)
    def _():
        body(prefix_ref[...], slice(0, BLOCK))


def _proj_lat_kernel(x_ref, pre_ref, gm_ref, w_ref, gq_ref, gkv_ref, cos_ref, sin_ref, u_ref, qn_ref, kvn_ref,
                     kr_ref, *, n_token_tiles):
    kv_end = Q_LORA_RANK + KV_LORA_RANK

    def body(h, rows):
        u = _rms(h, gm_ref[...]).astype(u_ref.dtype)
        u_ref[rows, :] = u
        y = _dot_nt(u, w_ref[...])
        qn_ref[rows, :] = _rms(y[:, :Q_LORA_RANK], gq_ref[...]).astype(qn_ref.dtype)
        kvn_ref[rows, :] = _rms(y[:, Q_LORA_RANK:kv_end], gkv_ref[...]).astype(kvn_ref.dtype)
        kr_ref[rows, :] = _rope(y[:, kv_end:], cos_ref[rows, :], sin_ref[rows, :]).astype(kr_ref.dtype)

    _two_source_rows(n_token_tiles, body, x_ref, pre_ref)


def _proj_lat(x2d, prefix, g_mix, w_lat, g_q, g_kv, cos_t, sin_t):
    s_real, d = x2d.shape
    rows = s_real + BLOCK
    tm = LATENT_ROW_TILE
    n_tok = s_real // tm
    row_spec = lambda n: pl.BlockSpec((tm, n), lambda i: (i, 0))
    full_spec = lambda r, n: pl.BlockSpec((r, n), lambda i: (0, 0))
    return pl.pallas_call(
        functools.partial(_proj_lat_kernel, n_token_tiles=n_tok), name="proj_latents",
        grid=(n_tok + 1,),
        in_specs=[pl.BlockSpec((tm, d), lambda i: (jnp.minimum(i, n_tok - 1), 0)), full_spec(BLOCK, d),
                  full_spec(1, d), full_spec(LAT_COLS, d), full_spec(1, Q_LORA_RANK), full_spec(1, KV_LORA_RANK),
                  row_spec(LANES), row_spec(LANES)],
        out_specs=[row_spec(d), row_spec(Q_LORA_RANK), row_spec(KV_LORA_RANK), row_spec(LANES)],
        out_shape=[jax.ShapeDtypeStruct((rows, d), BF16),
                   jax.ShapeDtypeStruct((rows, Q_LORA_RANK), BF16),
                   jax.ShapeDtypeStruct((rows, KV_LORA_RANK), BF16),
                   jax.ShapeDtypeStruct((rows, LANES), BF16)],
        compiler_params=_cparams(1),
    )(x2d, prefix, g_mix, w_lat, g_q, g_kv, cos_t, sin_t)


def _matmul_t_kernel(x_ref, w_ref, o_ref, w_bf):
    @pl.when(pl.program_id(1) == 0)
    def _():
        w_bf[...] = w_ref[...].T.astype(BF16)

    o_ref[...] = _dot(x_ref[...], w_bf[...]).astype(o_ref.dtype)


def _matmul_cols_t(x, w_t, col_start, n_cols, out_dtype, tn, name):
    rows, k = x.shape
    tm = _row_tile(rows, ALL_ROWS_TILE)
    assert col_start % 8 == 0 and n_cols % tn == 0
    return pl.pallas_call(
        _matmul_t_kernel, name=name,
        grid=(n_cols // tn, rows // tm),
        in_specs=[pl.BlockSpec((tm, k), lambda n, m: (m, 0)),
                  pl.BlockSpec((pl.Element(tn), pl.Element(k)),
                               lambda n, m: (pl.multiple_of(col_start + n * tn, 8), 0))],
        out_specs=pl.BlockSpec((tm, tn), lambda n, m: (m, n)),
        out_shape=jax.ShapeDtypeStruct((rows, n_cols), out_dtype),
        scratch_shapes=[pltpu.VMEM((k, tn), BF16)],
        compiler_params=_cparams(2),
    )(x, w_t)


def _kv_up_kernel(x_ref, w_ref, o_ref):
    y = _dot(x_ref[...], w_ref[...])
    for hh in range(o_ref.shape[0]):
        o_ref[hh] = y[:, hh * LANES:(hh + 1) * LANES].astype(o_ref.dtype)


V_EXT = V_HEAD_DIM + 16


def _v_up_t_kernel(x_ref, w_ref, o_ref):
    y_t = _dot_nt(w_ref[...], x_ref[...])
    pad_shape = (V_EXT - V_HEAD_DIM, o_ref.shape[2])
    ones_row = (lax.broadcasted_iota(jnp.int32, pad_shape, 0) == 0).astype(o_ref.dtype)
    for hh in range(o_ref.shape[0]):
        o_ref[hh, :LANES, :] = y_t[hh * LANES:(hh + 1) * LANES, :].astype(o_ref.dtype)
        o_ref[hh, LANES:, :] = ones_row


def _v_up_t(kvn, w_vt):
    rows, k = kvn.shape
    n_slabs = w_vt.shape[0] // LANES
    tm = _row_tile(rows, ALL_ROWS_TILE)
    per_tile = SLABS_PER_TILE
    return pl.pallas_call(
        _v_up_t_kernel, name="v_up_transposed",
        grid=(n_slabs // per_tile, rows // tm),
        in_specs=[pl.BlockSpec((tm, k), lambda n, m: (m, 0)),
                  pl.BlockSpec((per_tile * LANES, k), lambda n, m: (n, 0))],
        out_specs=pl.BlockSpec((per_tile, V_EXT, tm), lambda n, m: (n, 0, m)),
        out_shape=jax.ShapeDtypeStruct((n_slabs, V_EXT, rows), BF16),
        compiler_params=_cparams(2),
    )(kvn, w_vt)


def _kv_up(kvn, w_kv):
    rows, k = kvn.shape
    n_slabs = w_kv.shape[1] // LANES
    tm = _row_tile(rows, ALL_ROWS_TILE)
    per_tile = SLABS_PER_TILE
    return pl.pallas_call(
        _kv_up_kernel, name="kv_up",
        grid=(n_slabs // per_tile, rows // tm),
        in_specs=[pl.BlockSpec((tm, k), lambda n, m: (m, 0)),
                  pl.BlockSpec((k, per_tile * LANES), lambda n, m: (0, n))],
        out_specs=pl.BlockSpec((per_tile, tm, LANES), lambda n, m: (n, m, 0)),
        out_shape=jax.ShapeDtypeStruct((n_slabs, rows, LANES), BF16),
        compiler_params=_cparams(2),
    )(kvn, w_kv)


def _q_up_kernel(x_ref, w_ref, cos_ref, sin_ref, o_ref, w_bf, *, heads_per_tile):
    @pl.when(pl.program_id(1) == 0)
    def _():
        w_bf[...] = jnp.zeros(w_bf.shape, w_bf.dtype)
        for hh in range(heads_per_tile):
            w_bf[:, hh * Q_CAT:hh * Q_CAT + QK_HEAD_DIM] = (
                w_ref[:, hh * QK_HEAD_DIM:(hh + 1) * QK_HEAD_DIM].astype(w_bf.dtype))

    tm = o_ref.shape[0]
    small = tm // 5
    r0 = 0
    for tp in (tm - small, small):
        rows = slice(r0, r0 + tp)
        y = _dot(x_ref[rows, :], w_bf[...]) * (QK_HEAD_DIM ** -0.5 * LOG2E)
        cos_t = cos_ref[rows, :]
        sin_t = sin_ref[rows, :]
        for hh in range(heads_per_tile):
            lo = hh * Q_CAT
            o_ref[rows, lo:lo + LANES] = y[:, lo:lo + LANES].astype(o_ref.dtype)
            o_ref[rows, lo + LANES:lo + Q_CAT] = _rope(y[:, lo + LANES:lo + Q_CAT], cos_t, sin_t).astype(o_ref.dtype)
        r0 += tp


def _q_up(qn, w_q_up, cos_t, sin_t):
    rows, k = qn.shape
    heads_per_tile = Q_HEADS_PER_TILE
    tm = _row_tile(rows, ALL_ROWS_TILE)
    tn = heads_per_tile * Q_CAT
    n_tiles = w_q_up.shape[2] // (heads_per_tile * QK_HEAD_DIM)
    return pl.pallas_call(
        functools.partial(_q_up_kernel, heads_per_tile=heads_per_tile), name="q_up_rope",
        grid=(n_tiles, rows // tm),
        in_specs=[pl.BlockSpec((tm, k), lambda n, m: (m, 0)),
                  pl.BlockSpec((None, k, heads_per_tile * QK_HEAD_DIM), lambda n, m: (0, 0, n)),
                  pl.BlockSpec((tm, LANES), lambda n, m: (m, 0)), pl.BlockSpec((tm, LANES), lambda n, m: (m, 0))],
        out_specs=pl.BlockSpec((tm, tn), lambda n, m: (m, n)),
        out_shape=jax.ShapeDtypeStruct((rows, n_tiles * tn), BF16),
        scratch_shapes=[pltpu.VMEM((k, tn), BF16)],
        compiler_params=_cparams(2),
    )(qn, w_q_up, cos_t, sin_t)


ATTN_KEY_BLOCK = 64
ATTN_PART = 2 * LANES


def _attn_kernel(q_ref, kn_ref, vt_ref, kr_ref, o_ref, kcat_ref, m_ref, acc_ref, sa0_ref, sa1_ref, sb0_ref,
                 sb1_ref, sc0_ref, sc1_ref, spre_ref, p_ref, *, tq, n_real_tiles):
    step_id = pl.program_id(1)
    qi = step_id - 1
    s_real = n_real_tiles * tq
    tk = tq // 2

    @pl.when(step_id == 0)
    def _():
        kcat_ref[:, :LANES] = kn_ref[...]
        kcat_ref[:, LANES:] = kr_ref[...]

    th = ATTN_PART
    parts = tuple(range(0, tq, th))

    def init():
        m_ref[...] = jnp.full(m_ref.shape, -jnp.inf, F32)
        acc_ref[...] = jnp.zeros(acc_ref.shape, F32)

    def update(c0, blocks, pv):
        cols = slice(c0, c0 + th)

        def load(ref, row0, n, visible):
            s = ref[row0:row0 + n, cols]
            return s if visible is None else jnp.where(visible(row0, n), s, -jnp.inf)

        m_prev = m_ref[:, cols]
        m_cur = functools.reduce(
            jnp.maximum, [jnp.max(load(ref, r, n, vis), axis=0, keepdims=True) for ref, r, n, _, vis in blocks])
        m_new = jnp.maximum(m_prev, m_cur)
        alpha = jnp.exp2(m_prev - m_new)
        m_ref[:, cols] = m_new
        for ref, r, n, p_row, vis in blocks:
            for b0 in range(0, n, ATTN_KEY_BLOCK):
                p = jnp.exp2(load(ref, r + b0, ATTN_KEY_BLOCK, vis) - m_new)
                p_ref[p_row + b0:p_row + b0 + ATTN_KEY_BLOCK, cols] = p.astype(p_ref.dtype)
        new = functools.reduce(jnp.add, [_dot(v_t, p_ref[p_row:p_row + n, cols]) for p_row, n, v_t in pv])
        acc_ref[:, cols] = alpha * acc_ref[:, cols] + new

    def finalize():
        acc = acc_ref[...]
        o_ref[...] = (acc[:V_HEAD_DIM, :] / acc[V_HEAD_DIM:V_HEAD_DIM + 1, :]).T.astype(o_ref.dtype)

    def prefix_scores():
        k_pre = kcat_ref[s_real:s_real + BLOCK, :]
        for c0 in parts:
            spre_ref[:, c0:c0 + th] = _dot_nt(k_pre, q_ref[c0:c0 + th, :])

    v_pre_t = vt_ref[:, s_real:s_real + BLOCK]
    pre_row = tk

    def key_ids(row0, n):
        return lax.broadcasted_iota(jnp.int32, (n, th), 0) + row0

    def query_ids(c0, n):
        return lax.broadcasted_iota(jnp.int32, (n, th), 1) + c0

    @pl.when(step_id > 0)
    def _():
        init()
        prefix_scores()

        def raw_scores(dst_ref, j, which=parts):
            k_blk = kcat_ref[pl.ds(pl.multiple_of(j * tk, tk), tk), :]
            for c0 in which:
                dst_ref[:, c0:c0 + th] = _dot_nt(k_blk, q_ref[c0:c0 + th, :])

        def full_update(src_ref, tile, which=parts):
            v_t = vt_ref[:, pl.ds(pl.multiple_of(tile * tk, tk), tk)]
            for c0 in which:
                update(c0, [(src_ref, 0, tk, 0, None)], [(0, tk, v_t)])

        def pair(a, src, dst):
            raw_scores(dst[0], a + 2)
            raw_scores(dst[1], a + 3)
            full_update(src[0], a)
            full_update(src[1], a + 1)

        def diagonal(c0, src_ref, tile):
            q0 = c0 % tk
            n_k = q0 + th
            causal = lambda row0, n: key_ids(row0, n) <= query_ids(q0, n)
            no_pads = lambda row0, n: key_ids(row0, n) >= PAD_LEN
            start = pl.multiple_of(tile * tk, tk)
            update(c0, [(src_ref, 0, n_k, 0, causal), (spre_ref, 0, BLOCK, pre_row, no_pads)],
                   [(0, n_k, vt_ref[:, pl.ds(start, n_k)]), (pre_row, BLOCK, v_pre_t)])

        def own_tiles(src):
            lower = tuple(c0 for c0 in parts if c0 < tk)
            upper = tuple(c0 for c0 in parts if c0 >= tk)
            for c0 in lower:
                diagonal(c0, src[0], 2 * qi)
            full_update(src[0], 2 * qi, upper)
            for c0 in upper:
                diagonal(c0, src[1], 2 * qi + 1)
            finalize()

        buf_a = (sa0_ref, sa1_ref)
        buf_b = (sb0_ref, sb1_ref)
        buf_c = (sc0_ref, sc1_ref)
        raw_scores(buf_a[0], 0)
        raw_scores(buf_a[1], 1)

        def body(i, carry):
            pair(6 * i, buf_a, buf_b)
            pair(6 * i + 2, buf_b, buf_c)
            pair(6 * i + 4, buf_c, buf_a)
            return carry

        lax.fori_loop(0, qi // 3, body, 0)

        @pl.when(qi % 3 == 0)
        def _():
            own_tiles(buf_a)

        @pl.when(qi % 3 == 1)
        def _():
            pair(2 * qi - 2, buf_a, buf_b)
            own_tiles(buf_b)

        @pl.when(qi % 3 == 2)
        def _():
            pair(2 * qi - 4, buf_a, buf_b)
            pair(2 * qi - 2, buf_b, buf_c)
            own_tiles(buf_c)

    @pl.when(step_id == 0)
    def _():
        init()
        prefix_scores()
        for c0 in parts:
            def visible(row0, n, c0=c0):
                key, query = key_ids(row0, n), query_ids(c0, n)
                return (key <= query) & ((key >= PAD_LEN) | (key == query))

            update(c0, [(spre_ref, 0, BLOCK, pre_row, visible)], [(pre_row, BLOCK, v_pre_t)])
        finalize()


def _attention(q_cat, k_slabs, v_t, kr, s_real):
    rows = q_cat.shape[0]
    tq = ATTN_QUERY_TILE
    tk = tq // 2
    n_real_tiles = s_real // tq
    q_tile = lambda h, i: ((i + n_real_tiles) % (n_real_tiles + 1), h)
    return pl.pallas_call(
        functools.partial(_attn_kernel, tq=tq, n_real_tiles=n_real_tiles), name="mla_attention",
        grid=(MLA_HEADS, n_real_tiles + 1),
        in_specs=[pl.BlockSpec((tq, Q_CAT), q_tile),
                  pl.BlockSpec((None, rows, LANES), lambda h, i: (h, 0, 0)),
                  pl.BlockSpec((None, V_EXT, rows), lambda h, i: (h, 0, 0)),
                  pl.BlockSpec((rows, LANES), lambda h, i: (0, 0))],
        out_specs=pl.BlockSpec((tq, V_HEAD_DIM), q_tile),
        out_shape=jax.ShapeDtypeStruct((rows, MLA_HEADS * V_HEAD_DIM), BF16),
        scratch_shapes=[pltpu.VMEM((rows, Q_CAT), BF16), pltpu.VMEM((1, tq), F32),
                        pltpu.VMEM((V_EXT, tq), F32)] + [pltpu.VMEM((tk, tq), F32)] * 6
                       + [pltpu.VMEM((BLOCK, tq), F32), pltpu.VMEM((tk + BLOCK, tq), BF16)],
        compiler_params=_cparams(2),
    )(q_cat, k_slabs, v_t, kr)


def _split3(x):
    x1 = x.astype(BF16)
    r1 = x - x1.astype(F32)
    x2 = r1.astype(BF16)
    x3 = (r1 - x2.astype(F32)).astype(BF16)
    return x1, x2, x3


def _block_row(x, size, j):
    c, n = x.shape
    g = x.reshape(c // size, size, n)[:, j:j + 1, :]
    return jnp.broadcast_to(g, (c // size, size, n)).reshape(c, n)


HGRN_LEVELS = 7
HGRN_GROUP = 16


def _hgrn_head(hq, hf, hi, hg, lb_raw, gain, st, valid, row, lev):
    n = BLOCK
    top = jnp.max(lb_raw, axis=0, keepdims=True)
    e = jnp.exp(lb_raw - top)
    lb = e[0:1, :] / jnp.sum(e, axis=0, keepdims=True)

    f = lb + (1.0 - lb) * _sigmoid(hf)
    f_eff = jnp.where(valid, f, 1.0)
    g = jnp.where(valid, jnp.log2(f), 0.0)
    k = jnp.where(valid, 1.0 - f, 0.0)
    q = _silu(hq.astype(F32))

    col = lax.broadcasted_iota(jnp.int32, (n, n), 1)
    tri = (col <= row).astype(BF16)
    g1, g2, g3 = _split3(g)
    b = _dot(tri, g1) + _dot(tri, g2) + _dot(tri, g3)

    q_bf = q.astype(BF16)
    k_bf = k.astype(BF16)
    a = jnp.where(lev == 0, jnp.sum(q * k, axis=1, keepdims=True), 0.0)
    for level in range(1, HGRN_LEVELS + 1):
        size = 1 << level
        if level == 1:
            w = jnp.where((row & 1) == 1, f_eff, 1.0)
        elif level == 2:
            r4 = row & 3
            up1 = pltpu.roll(g, n - 1, 0)
            dn1 = pltpu.roll(g, 1, 0)
            w = jnp.exp2(jnp.where(r4 == 0, up1, jnp.where(r4 == 1, 0.0, jnp.where(r4 == 2, g, g + dn1))))
        else:
            w = jnp.exp2(-jnp.abs(b - _block_row(b, size, size // 2 - 1)))
        w_bf = w.astype(BF16)
        a_l = _dot_nt(q_bf * w_bf, k_bf * w_bf)
        a = jnp.where(lev == level, a_l, a)

    o = _dot(a.astype(BF16), hi) + _dot_nt((q * jnp.exp2(b)).astype(BF16), st.astype(BF16))
    b_last = b[n - 1:n, :]
    st_new = st * jnp.exp2(b_last) + _dot_tn(hi, (k * jnp.exp2(b_last - b)).astype(BF16))
    out = _rms(o, gain) * _silu(hg.astype(F32))
    return out, st_new


def _hgrn_kernel(hq_ref, hf_ref, hi_ref, hg_ref, lb_ref, g_ref, o_ref, st_ref):
    c = pl.program_id(0)
    n = BLOCK

    @pl.when(c == 0)
    def _():
        st_ref[...] = jnp.zeros(st_ref.shape, F32)

    row = lax.broadcasted_iota(jnp.int32, (n, n), 0)
    col = lax.broadcasted_iota(jnp.int32, (n, n), 1)
    valid = (c > 0) | (row >= PAD_LEN)
    x = row ^ col
    lev = functools.reduce(jnp.add, [(x >= (1 << i)).astype(jnp.int32) for i in range(HGRN_LEVELS)])
    lev = jnp.where(col > row, -1, lev)
    gain = g_ref[...]

    def head_group(i, carry):
        for hh in range(HGRN_GROUP):
            head = HGRN_GROUP * i + hh
            cols = pl.ds(pl.multiple_of(head * BLOCK, BLOCK), BLOCK)
            out, st_new = _hgrn_head(hq_ref[:, cols], hf_ref[:, cols], hi_ref[:, cols], hg_ref[:, cols],
                                     lb_ref[:, cols], gain, st_ref[head], valid, row, lev)
            st_ref[head] = st_new
            o_ref[:, cols] = out.astype(o_ref.dtype)
        return carry

    lax.fori_loop(0, HGRN_HEADS // HGRN_GROUP, head_group, 0)


def _hgrn(hq, hf, hrest, lb_raw, g_hgrn):
    rows, width = hq.shape
    n_chunks = rows // BLOCK
    blk = lambda off: pl.BlockSpec((BLOCK, width), lambda c: ((c + n_chunks - 1) % n_chunks, off))
    return pl.pallas_call(
        _hgrn_kernel, name="hgrn2",
        grid=(n_chunks,),
        in_specs=[blk(0), blk(0), blk(0), blk(1),
                  pl.BlockSpec((lb_raw.shape[0], width), lambda c: (0, 0)),
                  pl.BlockSpec((1, HGRN_V_DIM), lambda c: (0, 0))],
        out_specs=blk(0),
        out_shape=jax.ShapeDtypeStruct((rows, width), BF16),
        scratch_shapes=[pltpu.VMEM((HGRN_HEADS, HGRN_V_DIM, HGRN_EXPAND), F32)],
        compiler_params=_cparams(1),
    )(hq, hf, hrest, hrest, lb_raw, g_hgrn)


def _merge_kernel(om_ref, oh_ref, wa_ref, wb_ref, ga_ref, gb_ref, o_ref, wa_bf, wb_bf):
    @pl.when(pl.program_id(1) == 0)
    def _():
        wa_bf[...] = wa_ref[...].astype(BF16)
        wb_bf[...] = wb_ref[...].astype(BF16)

    tm = o_ref.shape[0]
    small = tm // 5
    r0 = 0
    for tp in (tm - small, small):
        rows = slice(r0, r0 + tp)
        a = _dot(om_ref[rows, :], wa_bf[...])
        b = _dot(oh_ref[rows, :], wb_bf[...])
        o = _sigmoid(ga_ref[rows, :].astype(F32)) * a + _sigmoid(gb_ref[rows, :].astype(F32)) * b
        o_ref[rows, :] = o.astype(o_ref.dtype)
        r0 += tp


def _merge(o_mla, o_hgrn, w_a, w_b, hrest, gate_off):
    rows, k = o_mla.shape
    d = w_a.shape[2]
    tm = _row_tile(rows, ALL_ROWS_TILE)
    tn = MERGE_COL_TILE
    ga_off = gate_off // tn
    gb_off = (gate_off + d) // tn
    w_spec = pl.BlockSpec((None, k, tn), lambda n, m: (0, 0, n))
    return pl.pallas_call(
        _merge_kernel, name="branch_merge",
        grid=(d // tn, rows // tm),
        in_specs=[pl.BlockSpec((tm, k), lambda n, m: (m, 0)), pl.BlockSpec((tm, k), lambda n, m: (m, 0)),
                  w_spec, w_spec,
                  pl.BlockSpec((tm, tn), lambda n, m: (m, n + ga_off)),
                  pl.BlockSpec((tm, tn), lambda n, m: (m, n + gb_off))],
        out_specs=pl.BlockSpec((tm, tn), lambda n, m: (m, n)),
        out_shape=jax.ShapeDtypeStruct((rows, d), BF16),
        scratch_shapes=[pltpu.VMEM((k, tn), BF16), pltpu.VMEM((k, tn), BF16)],
        compiler_params=_cparams(2),
    )(o_mla, o_hgrn, w_a, w_b, hrest, hrest)


def _out_proj_kernel(x_ref, pre_ref, mg_ref, w_ref, g_ref, h2_ref, u2_ref, w_bf, *, n_token_tiles):
    @pl.when(pl.program_id(0) == 0)
    def _():
        w_bf[...] = w_ref[...].astype(BF16)

    def body(h, rows):
        h2 = h + _dot(mg_ref[rows, :], w_bf[...])
        h2_ref[rows, :] = h2
        u2_ref[rows, :] = _rms(h2, g_ref[...]).astype(u2_ref.dtype)

    _two_source_rows(n_token_tiles, body, x_ref, pre_ref)


def _out_proj(merged, w_out, x2d, prefix, g_ffn):
    s_real, d = x2d.shape
    rows = s_real + BLOCK
    tm = OUT_PROJ_ROW_TILE
    n_tok = s_real // tm
    row_spec = pl.BlockSpec((tm, d), lambda i: (i, 0))
    return pl.pallas_call(
        functools.partial(_out_proj_kernel, n_token_tiles=n_tok), name="mix_out_proj",
        grid=(n_tok + 1,),
        in_specs=[pl.BlockSpec((tm, d), lambda i: (jnp.minimum(i, n_tok - 1), 0)),
                  pl.BlockSpec((BLOCK, d), lambda i: (0, 0)), row_spec,
                  pl.BlockSpec((None, d, d), lambda i: (0, 0, 0), pipeline_mode=pl.Buffered(1)),
                  pl.BlockSpec((1, d), lambda i: (0, 0))],
        out_specs=[row_spec, row_spec],
        out_shape=[jax.ShapeDtypeStruct((rows, d), F32), jax.ShapeDtypeStruct((rows, d), BF16)],
        scratch_shapes=[pltpu.VMEM((d, d), BF16)],
        compiler_params=_cparams(1),
    )(x2d, prefix, merged, w_out, g_ffn)


HALO = 16


def _ffn_in_kernel(u_ref, halo_ref, wg_ref, wu_ref, cw_ref, cb_ref, o_ref, wg_bf, wu_bf, tail_ref, *, part_rows):
    @pl.when(pl.program_id(1) == 0)
    def _():
        wg_bf[...] = wg_ref[...].astype(BF16)
        wu_bf[...] = wu_ref[...].astype(BF16)
        tail_ref[...] = _dot(halo_ref[...], wg_bf[...])[HALO - 8:HALO, :]

    wg = wg_bf[...]
    wu = wu_bf[...]
    cw = cw_ref[...]
    cb = cb_ref[...]
    tail = tail_ref[6:8, :]
    r0 = 0
    for tp in part_rows:
        row = lax.broadcasted_iota(jnp.int32, (tp, o_ref.shape[1]), 0)
        u = u_ref[r0:r0 + tp, :]
        gate = _dot(u, wg)
        up = _dot(u, wu)
        back1 = jnp.where(row == 0, tail[1:2, :], pltpu.roll(gate, 1, 0))
        back2 = jnp.where(row == 0, tail[0:1, :], jnp.where(row == 1, tail[1:2, :], pltpu.roll(gate, 2, 0)))
        conv = cw[0:1, :] * back2 + cw[1:2, :] * back1 + cw[2:3, :] * gate + cb
        o_ref[r0:r0 + tp, :] = (_silu(conv) * up).astype(o_ref.dtype)
        tail = gate[tp - 2:tp, :]
        r0 += tp
    tail_ref[...] = gate[tp - 8:tp, :]


def _ffn_in(u2, w_fi, conv_w, conv_b, s_real):
    rows, d = u2.shape
    tm = FFN_ROW_TILE
    tn = FFN_COL_TILE
    up_off = D_FF // tn
    halo_map = lambda n, m: (rows // HALO - 1, 0)
    return pl.pallas_call(
        functools.partial(_ffn_in_kernel, part_rows=(tm // 2, tm // 2)), name="ffn_in_conv_gate",
        grid=(D_FF // tn, s_real // tm),
        in_specs=[pl.BlockSpec((tm, d), lambda n, m: (m, 0)), pl.BlockSpec((HALO, d), halo_map),
                  pl.BlockSpec((None, d, tn), lambda n, m: (0, 0, n)),
                  pl.BlockSpec((None, d, tn), lambda n, m: (0, 0, n + up_off)),
                  pl.BlockSpec((None, conv_w.shape[1], tn), lambda n, m: (0, 0, n)),
                  pl.BlockSpec((1, tn), lambda n, m: (0, n))],
        out_specs=pl.BlockSpec((tm, tn), lambda n, m: (m, n)),
        out_shape=jax.ShapeDtypeStruct((s_real, D_FF), BF16),
        scratch_shapes=[pltpu.VMEM((d, tn), BF16), pltpu.VMEM((d, tn), BF16), pltpu.VMEM((8, tn), F32)],
        compiler_params=_cparams(2),
    )(u2, u2, w_fi, w_fi, conv_w, conv_b)


def _ffn_out_kernel(a_ref, w_ref, h2_ref, g_ref, o_ref):
    kk = pl.program_id(1)

    @pl.when(kk == 0)
    def _():
        o_ref[...] = h2_ref[...]

    o_ref[...] += _dot(a_ref[...], w_ref[...])

    @pl.when(kk == pl.num_programs(1) - 1)
    def _():
        o_ref[...] = _rms(o_ref[...], g_ref[...])


def _ffn_out(act, w_fo, h2, g_final):
    s_real, k = act.shape
    d = w_fo.shape[1]
    tm = FFN_OUT_ROW_TILE
    tk = k // FFN_OUT_K_STEPS
    assert tk % LANES == 0
    return pl.pallas_call(
        _ffn_out_kernel, name="ffn_out_final_norm",
        grid=(s_real // tm, k // tk),
        in_specs=[pl.BlockSpec((tm, tk), lambda m, kk: (m, kk)), pl.BlockSpec((tk, d), lambda m, kk: (kk, 0)),
                  pl.BlockSpec((tm, d), lambda m, kk: (m, 0)), pl.BlockSpec((1, d), lambda m, kk: (0, 0))],
        out_specs=pl.BlockSpec((tm, d), lambda m, kk: (m, 0)),
        out_shape=jax.ShapeDtypeStruct((s_real, d), F32),
        compiler_params=_cparams(2),
    )(act, w_fo, h2, g_final)


def kernel(x, positions, meta_tokens, w_in, w_q_up, w_kv_up, w_branch_mla, w_branch_hgrn, w_out, w_ffn_in,
           w_ffn_out, conv_w, conv_b, g_mix_norm, g_q_norm, g_kv_norm, g_hgrn_norm, g_ffn_norm, g_final_norm,
           lb_raw):
    b, s_real, d = x.shape
    assert b == 1 and w_in.shape[0] == 1 and s_real % max(ATTN_QUERY_TILE, FFN_ROW_TILE) == 0
    dt = x.dtype

    prefix = jnp.concatenate([jnp.zeros((PAD_LEN, d), dt), meta_tokens.astype(dt)], axis=0)
    pos = jnp.concatenate([positions[0].astype(jnp.int32) + N_META, jnp.zeros((PAD_LEN,), jnp.int32),
                           jnp.arange(N_META, dtype=jnp.int32)])
    inv = 1.0 / (ROPE_THETA ** (jnp.arange(0, QK_ROPE_DIM, 2, dtype=F32) / QK_ROPE_DIM))
    inv = jnp.concatenate([inv, inv, jnp.zeros((LANES - QK_ROPE_DIM,), F32)])[None, :]

    w_in_t = jnp.swapaxes(w_in, 1, 2)[0]
    w_lat = _w_lat(w_in_t)
    w_kv3 = w_kv_up[0].reshape(KV_LORA_RANK, MLA_HEADS, QK_NOPE_DIM + V_HEAD_DIM)
    w_k = w_kv3[:, :, :QK_NOPE_DIM].reshape(KV_LORA_RANK, -1).astype(BF16)
    w_vt = w_kv3[:, :, QK_NOPE_DIM:].reshape(KV_LORA_RANK, -1).T.astype(BF16)
    w_fo = w_ffn_out[0].astype(BF16)

    cos_t, sin_t = _rope_tables(pos.astype(F32)[:, None], inv)

    u, qn, kvn, kr = _proj_lat(x[0], prefix, g_mix_norm, w_lat, g_q_norm, g_kv_norm, cos_t, sin_t)
    hq = _matmul_cols_t(u, w_in_t, LAT_END, d, BF16, PROJ_COL_TILE, "proj_hgrn_q")
    hf = _matmul_cols_t(u, w_in_t, LAT_END + d, d, F32, PROJ_COL_TILE, "proj_hgrn_forget")
    hrest = _matmul_cols_t(u, w_in_t, LAT_END + 2 * d, 4 * d, BF16, PROJ_COL_TILE, "proj_hgrn_rest")
    q_cat = _q_up(qn, w_q_up, cos_t, sin_t)
    k_slabs = _kv_up(kvn, w_k)
    v_t = _v_up_t(kvn, w_vt)
    o_mla = _attention(q_cat, k_slabs, v_t, kr, s_real)
    o_hgrn = _hgrn(hq, hf, hrest, lb_raw, g_hgrn_norm)
    merged = _merge(o_mla, o_hgrn, w_branch_mla, w_branch_hgrn, hrest, 2 * d)
    h2, u2 = _out_proj(merged, w_out, x[0], prefix, g_ffn_norm)

    act = _ffn_in(u2, w_ffn_in, conv_w, conv_b, s_real)
    out = _ffn_out(act, w_fo, h2, g_final_norm[None, :])
    return out[None]
```

```python
import functools

import jax
import jax.numpy as jnp
from jax import lax
from jax.experimental import pallas as pl
from jax.experimental.pallas import tpu as pltpu

F32 = jnp.float32
BF16 = jnp.bfloat16

N_META = 16
BLOCK = 128
PAD_LEN = BLOCK - N_META
MLA_HEADS = 16
Q_LORA_RANK = 1536
KV_LORA_RANK = 512
QK_NOPE_DIM = 128
QK_ROPE_DIM = 64
QK_HEAD_DIM = QK_NOPE_DIM + QK_ROPE_DIM
V_HEAD_DIM = 128
ROPE_THETA = 10000.0
HGRN_HEADS = 16
HGRN_EXPAND = 128
HGRN_V_DIM = 128
D_FF = 5632
NORM_EPS = 1e-6

LANES = 128
Q_CAT = 2 * LANES
LAT_COLS = Q_LORA_RANK + KV_LORA_RANK + LANES
LOG2E = 1.4426950408889634
VMEM_LIMIT = 52 * 1024 * 1024

ALL_ROWS_TILE = 640
WIDE_ROWS_TILE = 1664
PROJ_COL_TILE = 1024
MERGE_COL_TILE = 512
SLABS_PER_TILE = 8
Q_HEADS_PER_TILE = 8
WEIGHT_PREP_TILE = 512
LATENT_ROW_TILE = 512
OUT_PROJ_ROW_TILE = 256
ATTN_QUERY_TILE = 1024
FFN_ROW_TILE = 2048
FFN_COL_TILE = 512
FFN_OUT_ROW_TILE = 512
FFN_OUT_K_STEPS = 2


def _cparams(n_axes):
    return pltpu.CompilerParams(dimension_semantics=("arbitrary",) * n_axes, vmem_limit_bytes=VMEM_LIMIT)


def _row_tile(rows, target):
    for t in range(target, 0, -LANES):
        if rows % t == 0:
            return t
    raise ValueError(f"no 128-multiple row tile for {rows}")


def _dot(a, b):
    return jnp.dot(a, b, preferred_element_type=F32)


def _dot_nt(a, b):
    return lax.dot_general(a, b, (((1,), (1,)), ((), ())), preferred_element_type=F32)


def _dot_tn(a, b):
    return lax.dot_general(a, b, (((0,), (0,)), ((), ())), preferred_element_type=F32)


def _rms(x, g):
    return x * lax.rsqrt(jnp.mean(x * x, axis=-1, keepdims=True) + NORM_EPS) * g


def _sigmoid(x):
    return 0.5 * jnp.tanh(0.5 * x) + 0.5


def _silu(x):
    return x * _sigmoid(x)


def _rope(x, cos_t, sin_t):
    rot = pltpu.roll(x, 32, 1) - pltpu.roll(x, 96, 1)
    return x * cos_t + rot * sin_t


def _rope_table_kernel(pos_ref, inv_ref, cos_ref, sin_ref):
    ang = pos_ref[...] * inv_ref[...]
    keep = lax.broadcasted_iota(jnp.int32, ang.shape, 1) < QK_ROPE_DIM
    cos_ref[...] = jnp.where(keep, jnp.cos(ang), 0.0)
    sin_ref[...] = jnp.where(keep, jnp.sin(ang), 0.0)


def _rope_tables(pos_f, inv):
    rows = pos_f.shape[0]
    tm = _row_tile(rows, ALL_ROWS_TILE)
    spec = pl.BlockSpec((tm, LANES), lambda i: (i, 0))
    return pl.pallas_call(
        _rope_table_kernel, name="rope_tables",
        grid=(rows // tm,),
        in_specs=[pl.BlockSpec((tm, 1), lambda i: (i, 0)), pl.BlockSpec((1, LANES), lambda i: (0, 0))],
        out_specs=[spec, spec],
        out_shape=[jax.ShapeDtypeStruct((rows, LANES), F32)] * 2,
        compiler_params=_cparams(1),
    )(pos_f, inv)


LAT_END = Q_LORA_RANK + KV_LORA_RANK + QK_ROPE_DIM


def _w_lat_kernel(w_ref, o_ref):
    row = lax.broadcasted_iota(jnp.int32, w_ref.shape, 0)
    o_ref[...] = jnp.where(row < LAT_END, w_ref[...], 0.0).astype(o_ref.dtype)


def _w_lat(w_in_t):
    d = w_in_t.shape[1]
    tk = WEIGHT_PREP_TILE
    return pl.pallas_call(
        _w_lat_kernel, name="w_latent_cast",
        grid=(d // tk,),
        in_specs=[pl.BlockSpec((LAT_COLS, tk), lambda c: (0, c))],
        out_specs=pl.BlockSpec((LAT_COLS, tk), lambda c: (0, c)),
        out_shape=jax.ShapeDtypeStruct((LAT_COLS, d), BF16),
        compiler_params=_cparams(1),
    )(w_in_t)


def _two_source_rows(n_token_tiles, body, token_ref, prefix_ref):
    i = pl.program_id(0)

    @pl.when(i < n_token_tiles)
    def _():
        body(token_ref[...], slice(0, token_ref.shape[0]))

    @pl.when(i == n_token_tiles)
    def _():
        body(prefix_ref[...], slice(0, BLOCK))


def _proj_lat_kernel(x_ref, pre_ref, gm_ref, w_ref, gq_ref, gkv_ref, cos_ref, sin_ref, u_ref, qn_ref, kvn_ref,
                     kr_ref, *, n_token_tiles):
    kv_end = Q_LORA_RANK + KV_LORA_RANK

    def body(h, rows):
        u = _rms(h, gm_ref[...]).astype(u_ref.dtype)
        u_ref[rows, :] = u
        y = _dot_nt(u, w_ref[...])
        qn_ref[rows, :] = _rms(y[:, :Q_LORA_RANK], gq_ref[...]).astype(qn_ref.dtype)
        kvn_ref[rows, :] = _rms(y[:, Q_LORA_RANK:kv_end], gkv_ref[...]).astype(kvn_ref.dtype)
        kr_ref[rows, :] = _rope(y[:, kv_end:], cos_ref[rows, :], sin_ref[rows, :]).astype(kr_ref.dtype)

    _two_source_rows(n_token_tiles, body, x_ref, pre_ref)


def _proj_lat(x2d, prefix, g_mix, w_lat, g_q, g_kv, cos_t, sin_t):
    s_real, d = x2d.shape
    rows = s_real + BLOCK
    tm = LATENT_ROW_TILE
    n_tok = s_real // tm
    row_spec = lambda n: pl.BlockSpec((tm, n), lambda i: (i, 0))
    full_spec = lambda r, n: pl.BlockSpec((r, n), lambda i: (0, 0))
    return pl.pallas_call(
        functools.partial(_proj_lat_kernel, n_token_tiles=n_tok), name="proj_latents",
        grid=(n_tok + 1,),
        in_specs=[pl.BlockSpec((tm, d), lambda i: (jnp.minimum(i, n_tok - 1), 0)), full_spec(BLOCK, d),
                  full_spec(1, d), full_spec(LAT_COLS, d), full_spec(1, Q_LORA_RANK), full_spec(1, KV_LORA_RANK),
                  row_spec(LANES), row_spec(LANES)],
        out_specs=[row_spec(d), row_spec(Q_LORA_RANK), row_spec(KV_LORA_RANK), row_spec(LANES)],
        out_shape=[jax.ShapeDtypeStruct((rows, d), BF16),
                   jax.ShapeDtypeStruct((rows, Q_LORA_RANK), BF16),
                   jax.ShapeDtypeStruct((rows, KV_LORA_RANK), BF16),
                   jax.ShapeDtypeStruct((rows, LANES), BF16)],
        compiler_params=_cparams(1),
    )(x2d, prefix, g_mix, w_lat, g_q, g_kv, cos_t, sin_t)


def _matmul_t_kernel(x_ref, w_ref, o_ref, w_bf):
    @pl.when(pl.program_id(1) == 0)
    def _():
        w_bf[...] = w_ref[...].T.astype(BF16)

    o_ref[...] = _dot(x_ref[...], w_bf[...]).astype(o_ref.dtype)


def _matmul_cols_t(x, w_t, col_start, n_cols, out_dtype, tn, name):
    rows, k = x.shape
    tm = _row_tile(rows, ALL_ROWS_TILE)
    assert col_start % 8 == 0 and n_cols % tn == 0
    return pl.pallas_call(
        _matmul_t_kernel, name=name,
        grid=(n_cols // tn, rows // tm),
        in_specs=[pl.BlockSpec((tm, k), lambda n, m: (m, 0)),
                  pl.BlockSpec((pl.Element(tn), pl.Element(k)),
                               lambda n, m: (pl.multiple_of(col_start + n * tn, 8), 0))],
        out_specs=pl.BlockSpec((tm, tn), lambda n, m: (m, n)),
        out_shape=jax.ShapeDtypeStruct((rows, n_cols), out_dtype),
        scratch_shapes=[pltpu.VMEM((k, tn), BF16)],
        compiler_params=_cparams(2),
    )(x, w_t)


def _kv_up_kernel(x_ref, w_ref, o_ref):
    y = _dot(x_ref[...], w_ref[...])
    for hh in range(o_ref.shape[0]):
        o_ref[hh] = y[:, hh * LANES:(hh + 1) * LANES].astype(o_ref.dtype)


V_EXT = V_HEAD_DIM + 16


def _v_up_t_kernel(x_ref, w_ref, o_ref):
    y_t = _dot_nt(w_ref[...], x_ref[...])
    pad_shape = (V_EXT - V_HEAD_DIM, o_ref.shape[2])
    ones_row = (lax.broadcasted_iota(jnp.int32, pad_shape, 0) == 0).astype(o_ref.dtype)
    for hh in range(o_ref.shape[0]):
        o_ref[hh, :LANES, :] = y_t[hh * LANES:(hh + 1) * LANES, :].astype(o_ref.dtype)
        o_ref[hh, LANES:, :] = ones_row


def _v_up_t(kvn, w_vt):
    rows, k = kvn.shape
    n_slabs = w_vt.shape[0] // LANES
    tm = _row_tile(rows, WIDE_ROWS_TILE)
    per_tile = SLABS_PER_TILE
    return pl.pallas_call(
        _v_up_t_kernel, name="v_up_transposed",
        grid=(n_slabs // per_tile, rows // tm),
        in_specs=[pl.BlockSpec((tm, k), lambda n, m: (m, 0)),
                  pl.BlockSpec((per_tile * LANES, k), lambda n, m: (n, 0))],
        out_specs=pl.BlockSpec((per_tile, V_EXT, tm), lambda n, m: (n, 0, m)),
        out_shape=jax.ShapeDtypeStruct((n_slabs, V_EXT, rows), BF16),
        compiler_params=_cparams(2),
    )(kvn, w_vt)


def _kv_up(kvn, w_kv):
    rows, k = kvn.shape
    n_slabs = w_kv.shape[1] // LANES
    tm = _row_tile(rows, WIDE_ROWS_TILE)
    per_tile = SLABS_PER_TILE
    return pl.pallas_call(
        _kv_up_kernel, name="kv_up",
        grid=(n_slabs // per_tile, rows // tm),
        in_specs=[pl.BlockSpec((tm, k), lambda n, m: (m, 0)),
                  pl.BlockSpec((k, per_tile * LANES), lambda n, m: (0, n))],
        out_specs=pl.BlockSpec((per_tile, tm, LANES), lambda n, m: (n, m, 0)),
        out_shape=jax.ShapeDtypeStruct((n_slabs, rows, LANES), BF16),
        compiler_params=_cparams(2),
    )(kvn, w_kv)


def _q_up_kernel(x_ref, w_ref, cos_ref, sin_ref, o_ref, w_bf, *, heads_per_tile):
    @pl.when(pl.program_id(1) == 0)
    def _():
        w_bf[...] = jnp.zeros(w_bf.shape, w_bf.dtype)
        for hh in range(heads_per_tile):
            w_bf[:, hh * Q_CAT:hh * Q_CAT + QK_HEAD_DIM] = (
                w_ref[:, hh * QK_HEAD_DIM:(hh + 1) * QK_HEAD_DIM].astype(w_bf.dtype))

    tm = o_ref.shape[0]
    small = tm // 5
    r0 = 0
    for tp in (tm - small, small):
        rows = slice(r0, r0 + tp)
        y = _dot(x_ref[rows, :], w_bf[...]) * (QK_HEAD_DIM ** -0.5 * LOG2E)
        cos_t = cos_ref[rows, :]
        sin_t = sin_ref[rows, :]
        for hh in range(heads_per_tile):
            lo = hh * Q_CAT
            o_ref[rows, lo:lo + LANES] = y[:, lo:lo + LANES].astype(o_ref.dtype)
            o_ref[rows, lo + LANES:lo + Q_CAT] = _rope(y[:, lo + LANES:lo + Q_CAT], cos_t, sin_t).astype(o_ref.dtype)
        r0 += tp


def _q_up(qn, w_q_up, cos_t, sin_t):
    rows, k = qn.shape
    heads_per_tile = Q_HEADS_PER_TILE
    tm = _row_tile(rows, ALL_ROWS_TILE)
    tn = heads_per_tile * Q_CAT
    n_tiles = w_q_up.shape[2] // (heads_per_tile * QK_HEAD_DIM)
    return pl.pallas_call(
        functools.partial(_q_up_kernel, heads_per_tile=heads_per_tile), name="q_up_rope",
        grid=(n_tiles, rows // tm),
        in_specs=[pl.BlockSpec((tm, k), lambda n, m: (m, 0)),
                  pl.BlockSpec((None, k, heads_per_tile * QK_HEAD_DIM), lambda n, m: (0, 0, n)),
                  pl.BlockSpec((tm, LANES), lambda n, m: (m, 0)), pl.BlockSpec((tm, LANES), lambda n, m: (m, 0))],
        out_specs=pl.BlockSpec((tm, tn), lambda n, m: (m, n)),
        out_shape=jax.ShapeDtypeStruct((rows, n_tiles * tn), BF16),
        scratch_shapes=[pltpu.VMEM((k, tn), BF16)],
        compiler_params=_cparams(2),
    )(qn, w_q_up, cos_t, sin_t)


ATTN_KEY_BLOCK = 64
ATTN_PART = 2 * LANES


def _attn_kernel(q_ref, kn_ref, vt_ref, kr_ref, o_ref, kcat_ref, m_ref, acc_ref, sa0_ref, sa1_ref, sb0_ref,
                 sb1_ref, sc0_ref, sc1_ref, spre_ref, p_ref, *, tq, n_real_tiles):
    step_id = pl.program_id(1)
    qi = step_id - 1
    s_real = n_real_tiles * tq
    tk = tq // 2

    @pl.when(step_id == 0)
    def _():
        kcat_ref[:, :LANES] = kn_ref[...]
        kcat_ref[:, LANES:] = kr_ref[...]

    th = ATTN_PART
    parts = tuple(range(0, tq, th))

    def init():
        m_ref[...] = jnp.full(m_ref.shape, -jnp.inf, F32)
        acc_ref[...] = jnp.zeros(acc_ref.shape, F32)

    def update(c0, blocks, pv):
        cols = slice(c0, c0 + th)

        def load(ref, row0, n, visible):
            s = ref[row0:row0 + n, cols]
            return s if visible is None else jnp.where(visible(row0, n), s, -jnp.inf)

        m_prev = m_ref[:, cols]
        m_cur = functools.reduce(
            jnp.maximum, [jnp.max(load(ref, r, n, vis), axis=0, keepdims=True) for ref, r, n, _, vis in blocks])
        m_new = jnp.maximum(m_prev, m_cur)
        alpha = jnp.exp2(m_prev - m_new)
        m_ref[:, cols] = m_new
        for ref, r, n, p_row, vis in blocks:
            for b0 in range(0, n, ATTN_KEY_BLOCK):
                p = jnp.exp2(load(ref, r + b0, ATTN_KEY_BLOCK, vis) - m_new)
                p_ref[p_row + b0:p_row + b0 + ATTN_KEY_BLOCK, cols] = p.astype(p_ref.dtype)
        new = functools.reduce(jnp.add, [_dot(v_t, p_ref[p_row:p_row + n, cols]) for p_row, n, v_t in pv])
        acc_ref[:, cols] = alpha * acc_ref[:, cols] + new

    def finalize():
        acc = acc_ref[...]
        o_ref[...] = (acc[:V_HEAD_DIM, :] / acc[V_HEAD_DIM:V_HEAD_DIM + 1, :]).T.astype(o_ref.dtype)

    def prefix_scores():
        k_pre = kcat_ref[s_real:s_real + BLOCK, :]
        for c0 in parts:
            spre_ref[:, c0:c0 + th] = _dot_nt(k_pre, q_ref[c0:c0 + th, :])

    v_pre_t = vt_ref[:, s_real:s_real + BLOCK]
    pre_row = tk

    def key_ids(row0, n):
        return lax.broadcasted_iota(jnp.int32, (n, th), 0) + row0

    def query_ids(c0, n):
        return lax.broadcasted_iota(jnp.int32, (n, th), 1) + c0

    @pl.when(step_id > 0)
    def _():
        init()
        prefix_scores()

        def raw_scores(dst_ref, j, which=parts):
            k_blk = kcat_ref[pl.ds(pl.multiple_of(j * tk, tk), tk), :]
            for c0 in which:
                dst_ref[:, c0:c0 + th] = _dot_nt(k_blk, q_ref[c0:c0 + th, :])

        def full_update(src_ref, tile, which=parts):
            v_t = vt_ref[:, pl.ds(pl.multiple_of(tile * tk, tk), tk)]
            for c0 in which:
                update(c0, [(src_ref, 0, tk, 0, None)], [(0, tk, v_t)])

        def pair(a, src, dst):
            raw_scores(dst[0], a + 2)
            raw_scores(dst[1], a + 3)
            full_update(src[0], a)
            full_update(src[1], a + 1)

        def diagonal(c0, src_ref, tile):
            q0 = c0 % tk
            n_k = q0 + th
            causal = lambda row0, n: key_ids(row0, n) <= query_ids(q0, n)
            no_pads = lambda row0, n: key_ids(row0, n) >= PAD_LEN
            start = pl.multiple_of(tile * tk, tk)
            update(c0, [(src_ref, 0, n_k, 0, causal), (spre_ref, 0, BLOCK, pre_row, no_pads)],
                   [(0, n_k, vt_ref[:, pl.ds(start, n_k)]), (pre_row, BLOCK, v_pre_t)])

        def own_tiles(src):
            lower = tuple(c0 for c0 in parts if c0 < tk)
            upper = tuple(c0 for c0 in parts if c0 >= tk)
            for c0 in lower:
                diagonal(c0, src[0], 2 * qi)
            full_update(src[0], 2 * qi, upper)
            for c0 in upper:
                diagonal(c0, src[1], 2 * qi + 1)
            finalize()

        buf_a = (sa0_ref, sa1_ref)
        buf_b = (sb0_ref, sb1_ref)
        buf_c = (sc0_ref, sc1_ref)
        raw_scores(buf_a[0], 0)
        raw_scores(buf_a[1], 1)

        def body(i, carry):
            pair(6 * i, buf_a, buf_b)
            pair(6 * i + 2, buf_b, buf_c)
            pair(6 * i + 4, buf_c, buf_a)
            return carry

        lax.fori_loop(0, qi // 3, body, 0)

        @pl.when(qi % 3 == 0)
        def _():
            own_tiles(buf_a)

        @pl.when(qi % 3 == 1)
        def _():
            pair(2 * qi - 2, buf_a, buf_b)
            own_tiles(buf_b)

        @pl.when(qi % 3 == 2)
        def _():
            pair(2 * qi - 4, buf_a, buf_b)
            pair(2 * qi - 2, buf_b, buf_c)
            own_tiles(buf_c)

    @pl.when(step_id == 0)
    def _():
        init()
        prefix_scores()
        for c0 in parts:
            def visible(row0, n, c0=c0):
                key, query = key_ids(row0, n), query_ids(c0, n)
                return (key <= query) & ((key >= PAD_LEN) | (key == query))

            update(c0, [(spre_ref, 0, BLOCK, pre_row, visible)], [(pre_row, BLOCK, v_pre_t)])
        finalize()


def _attention(q_cat, k_slabs, v_t, kr, s_real):
    rows = q_cat.shape[0]
    tq = ATTN_QUERY_TILE
    tk = tq // 2
    n_real_tiles = s_real // tq
    q_tile = lambda h, i: ((i + n_real_tiles) % (n_real_tiles + 1), h)
    return pl.pallas_call(
        functools.partial(_attn_kernel, tq=tq, n_real_tiles=n_real_tiles), name="mla_attention",
        grid=(MLA_HEADS, n_real_tiles + 1),
        in_specs=[pl.BlockSpec((tq, Q_CAT), q_tile),
                  pl.BlockSpec((None, rows, LANES), lambda h, i: (h, 0, 0)),
                  pl.BlockSpec((None, V_EXT, rows), lambda h, i: (h, 0, 0)),
                  pl.BlockSpec((rows, LANES), lambda h, i: (0, 0))],
        out_specs=pl.BlockSpec((tq, V_HEAD_DIM), q_tile),
        out_shape=jax.ShapeDtypeStruct((rows, MLA_HEADS * V_HEAD_DIM), BF16),
        scratch_shapes=[pltpu.VMEM((rows, Q_CAT), BF16), pltpu.VMEM((1, tq), F32),
                        pltpu.VMEM((V_EXT, tq), F32)] + [pltpu.VMEM((tk, tq), F32)] * 6
                       + [pltpu.VMEM((BLOCK, tq), F32), pltpu.VMEM((tk + BLOCK, tq), BF16)],
        compiler_params=_cparams(2),
    )(q_cat, k_slabs, v_t, kr)


def _split3(x):
    x1 = x.astype(BF16)
    r1 = x - x1.astype(F32)
    x2 = r1.astype(BF16)
    x3 = (r1 - x2.astype(F32)).astype(BF16)
    return x1, x2, x3


def _block_row(x, size, j):
    c, n = x.shape
    g = x.reshape(c // size, size, n)[:, j:j + 1, :]
    return jnp.broadcast_to(g, (c // size, size, n)).reshape(c, n)


HGRN_LEVELS = 7
HGRN_GROUP = 16


def _hgrn_head(hq, hf, hi, hg, lb_raw, gain, st, valid, row, lev):
    n = BLOCK
    top = jnp.max(lb_raw, axis=0, keepdims=True)
    e = jnp.exp(lb_raw - top)
    lb = e[0:1, :] / jnp.sum(e, axis=0, keepdims=True)

    f = lb + (1.0 - lb) * _sigmoid(hf)
    f_eff = jnp.where(valid, f, 1.0)
    g = jnp.where(valid, jnp.log2(f), 0.0)
    k = jnp.where(valid, 1.0 - f, 0.0)
    q = _silu(hq.astype(F32))

    col = lax.broadcasted_iota(jnp.int32, (n, n), 1)
    tri = (col <= row).astype(BF16)
    g1, g2, g3 = _split3(g)
    b = _dot(tri, g1) + _dot(tri, g2) + _dot(tri, g3)

    q_bf = q.astype(BF16)
    k_bf = k.astype(BF16)
    a = jnp.where(lev == 0, jnp.sum(q * k, axis=1, keepdims=True), 0.0)
    for level in range(1, HGRN_LEVELS + 1):
        size = 1 << level
        if level == 1:
            w = jnp.where((row & 1) == 1, f_eff, 1.0)
        elif level == 2:
            r4 = row & 3
            up1 = pltpu.roll(g, n - 1, 0)
            dn1 = pltpu.roll(g, 1, 0)
            w = jnp.exp2(jnp.where(r4 == 0, up1, jnp.where(r4 == 1, 0.0, jnp.where(r4 == 2, g, g + dn1))))
        else:
            w = jnp.exp2(-jnp.abs(b - _block_row(b, size, size // 2 - 1)))
        w_bf = w.astype(BF16)
        a_l = _dot_nt(q_bf * w_bf, k_bf * w_bf)
        a = jnp.where(lev == level, a_l, a)

    o = _dot(a.astype(BF16), hi) + _dot_nt((q * jnp.exp2(b)).astype(BF16), st.astype(BF16))
    b_last = b[n - 1:n, :]
    st_new = st * jnp.exp2(b_last) + _dot_tn(hi, (k * jnp.exp2(b_last - b)).astype(BF16))
    out = _rms(o, gain) * _silu(hg.astype(F32))
    return out, st_new


def _hgrn_kernel(hq_ref, hf_ref, hi_ref, hg_ref, lb_ref, g_ref, o_ref, st_ref):
    c = pl.program_id(0)
    n = BLOCK

    @pl.when(c == 0)
    def _():
        st_ref[...] = jnp.zeros(st_ref.shape, F32)

    row = lax.broadcasted_iota(jnp.int32, (n, n), 0)
    col = lax.broadcasted_iota(jnp.int32, (n, n), 1)
    valid = (c > 0) | (row >= PAD_LEN)
    x = row ^ col
    lev = functools.reduce(jnp.add, [(x >= (1 << i)).astype(jnp.int32) for i in range(HGRN_LEVELS)])
    lev = jnp.where(col > row, -1, lev)
    gain = g_ref[...]

    def head_group(i, carry):
        for hh in range(HGRN_GROUP):
            head = HGRN_GROUP * i + hh
            cols = pl.ds(pl.multiple_of(head * BLOCK, BLOCK), BLOCK)
            out, st_new = _hgrn_head(hq_ref[:, cols], hf_ref[:, cols], hi_ref[:, cols], hg_ref[:, cols],
                                     lb_ref[:, cols], gain, st_ref[head], valid, row, lev)
            st_ref[head] = st_new
            o_ref[:, cols] = out.astype(o_ref.dtype)
        return carry

    lax.fori_loop(0, HGRN_HEADS // HGRN_GROUP, head_group, 0)


def _hgrn(hq, hf, hrest, lb_raw, g_hgrn):
    rows, width = hq.shape
    n_chunks = rows // BLOCK
    blk = lambda off: pl.BlockSpec((BLOCK, width), lambda c: ((c + n_chunks - 1) % n_chunks, off))
    return pl.pallas_call(
        _hgrn_kernel, name="hgrn2",
        grid=(n_chunks,),
        in_specs=[blk(0), blk(0), blk(0), blk(1),
                  pl.BlockSpec((lb_raw.shape[0], width), lambda c: (0, 0)),
                  pl.BlockSpec((1, HGRN_V_DIM), lambda c: (0, 0))],
        out_specs=blk(0),
        out_shape=jax.ShapeDtypeStruct((rows, width), BF16),
        scratch_shapes=[pltpu.VMEM((HGRN_HEADS, HGRN_V_DIM, HGRN_EXPAND), F32)],
        compiler_params=_cparams(1),
    )(hq, hf, hrest, hrest, lb_raw, g_hgrn)


def _merge_kernel(om_ref, oh_ref, wa_ref, wb_ref, ga_ref, gb_ref, o_ref, wa_bf, wb_bf):
    @pl.when(pl.program_id(1) == 0)
    def _():
        wa_bf[...] = wa_ref[...].astype(BF16)
        wb_bf[...] = wb_ref[...].astype(BF16)

    tm = o_ref.shape[0]
    small = tm // 5
    r0 = 0
    for tp in (tm - small, small):
        rows = slice(r0, r0 + tp)
        a = _dot(om_ref[rows, :], wa_bf[...])
        b = _dot(oh_ref[rows, :], wb_bf[...])
        o = _sigmoid(ga_ref[rows, :].astype(F32)) * a + _sigmoid(gb_ref[rows, :].astype(F32)) * b
        o_ref[rows, :] = o.astype(o_ref.dtype)
        r0 += tp


def _merge(o_mla, o_hgrn, w_a, w_b, hrest, gate_off):
    rows, k = o_mla.shape
    d = w_a.shape[2]
    tm = _row_tile(rows, ALL_ROWS_TILE)
    tn = MERGE_COL_TILE
    ga_off = gate_off // tn
    gb_off = (gate_off + d) // tn
    w_spec = pl.BlockSpec((None, k, tn), lambda n, m: (0, 0, n))
    return pl.pallas_call(
        _merge_kernel, name="branch_merge",
        grid=(d // tn, rows // tm),
        in_specs=[pl.BlockSpec((tm, k), lambda n, m: (m, 0)), pl.BlockSpec((tm, k), lambda n, m: (m, 0)),
                  w_spec, w_spec,
                  pl.BlockSpec((tm, tn), lambda n, m: (m, n + ga_off)),
                  pl.BlockSpec((tm, tn), lambda n, m: (m, n + gb_off))],
        out_specs=pl.BlockSpec((tm, tn), lambda n, m: (m, n)),
        out_shape=jax.ShapeDtypeStruct((rows, d), BF16),
        scratch_shapes=[pltpu.VMEM((k, tn), BF16), pltpu.VMEM((k, tn), BF16)],
        compiler_params=_cparams(2),
    )(o_mla, o_hgrn, w_a, w_b, hrest, hrest)


def _out_proj_kernel(x_ref, pre_ref, mg_ref, w_ref, g_ref, h2_ref, u2_ref, w_bf, *, n_token_tiles):
    @pl.when(pl.program_id(0) == 0)
    def _():
        w_bf[...] = w_ref[...].astype(BF16)

    def body(h, rows):
        h2 = h + _dot(mg_ref[rows, :], w_bf[...])
        h2_ref[rows, :] = h2
        u2_ref[rows, :] = _rms(h2, g_ref[...]).astype(u2_ref.dtype)

    _two_source_rows(n_token_tiles, body, x_ref, pre_ref)


def _out_proj(merged, w_out, x2d, prefix, g_ffn):
    s_real, d = x2d.shape
    rows = s_real + BLOCK
    tm = OUT_PROJ_ROW_TILE
    n_tok = s_real // tm
    row_spec = pl.BlockSpec((tm, d), lambda i: (i, 0))
    return pl.pallas_call(
        functools.partial(_out_proj_kernel, n_token_tiles=n_tok), name="mix_out_proj",
        grid=(n_tok + 1,),
        in_specs=[pl.BlockSpec((tm, d), lambda i: (jnp.minimum(i, n_tok - 1), 0)),
                  pl.BlockSpec((BLOCK, d), lambda i: (0, 0)), row_spec,
                  pl.BlockSpec((None, d, d), lambda i: (0, 0, 0), pipeline_mode=pl.Buffered(1)),
                  pl.BlockSpec((1, d), lambda i: (0, 0))],
        out_specs=[row_spec, row_spec],
        out_shape=[jax.ShapeDtypeStruct((rows, d), F32), jax.ShapeDtypeStruct((rows, d), BF16)],
        scratch_shapes=[pltpu.VMEM((d, d), BF16)],
        compiler_params=_cparams(1),
    )(x2d, prefix, merged, w_out, g_ffn)


HALO = 16


def _ffn_in_kernel(u_ref, halo_ref, wg_ref, wu_ref, cw_ref, cb_ref, o_ref, wg_bf, wu_bf, tail_ref, *, part_rows):
    @pl.when(pl.program_id(1) == 0)
    def _():
        wg_bf[...] = wg_ref[...].astype(BF16)
        wu_bf[...] = wu_ref[...].astype(BF16)
        tail_ref[...] = _dot(halo_ref[...], wg_bf[...])[HALO - 8:HALO, :]

    wg = wg_bf[...]
    wu = wu_bf[...]
    cw = cw_ref[...]
    cb = cb_ref[...]
    tail = tail_ref[6:8, :]
    r0 = 0
    for tp in part_rows:
        row = lax.broadcasted_iota(jnp.int32, (tp, o_ref.shape[1]), 0)
        u = u_ref[r0:r0 + tp, :]
        gate = _dot(u, wg)
        up = _dot(u, wu)
        back1 = jnp.where(row == 0, tail[1:2, :], pltpu.roll(gate, 1, 0))
        back2 = jnp.where(row == 0, tail[0:1, :], jnp.where(row == 1, tail[1:2, :], pltpu.roll(gate, 2, 0)))
        conv = cw[0:1, :] * back2 + cw[1:2, :] * back1 + cw[2:3, :] * gate + cb
        o_ref[r0:r0 + tp, :] = (_silu(conv) * up).astype(o_ref.dtype)
        tail = gate[tp - 2:tp, :]
        r0 += tp
    tail_ref[...] = gate[tp - 8:tp, :]


def _ffn_in(u2, w_fi, conv_w, conv_b, s_real):
    rows, d = u2.shape
    tm = FFN_ROW_TILE
    tn = FFN_COL_TILE
    up_off = D_FF // tn
    halo_map = lambda n, m: (rows // HALO - 1, 0)
    return pl.pallas_call(
        functools.partial(_ffn_in_kernel, part_rows=(tm // 2, tm // 2)), name="ffn_in_conv_gate",
        grid=(D_FF // tn, s_real // tm),
        in_specs=[pl.BlockSpec((tm, d), lambda n, m: (m, 0)), pl.BlockSpec((HALO, d), halo_map),
                  pl.BlockSpec((None, d, tn), lambda n, m: (0, 0, n)),
                  pl.BlockSpec((None, d, tn), lambda n, m: (0, 0, n + up_off)),
                  pl.BlockSpec((None, conv_w.shape[1], tn), lambda n, m: (0, 0, n)),
                  pl.BlockSpec((1, tn), lambda n, m: (0, n))],
        out_specs=pl.BlockSpec((tm, tn), lambda n, m: (m, n)),
        out_shape=jax.ShapeDtypeStruct((s_real, D_FF), BF16),
        scratch_shapes=[pltpu.VMEM((d, tn), BF16), pltpu.VMEM((d, tn), BF16), pltpu.VMEM((8, tn), F32)],
        compiler_params=_cparams(2),
    )(u2, u2, w_fi, w_fi, conv_w, conv_b)


def _ffn_out_kernel(a_ref, w_ref, h2_ref, g_ref, o_ref):
    kk = pl.program_id(1)

    @pl.when(kk == 0)
    def _():
        o_ref[...] = h2_ref[...]

    o_ref[...] += _dot(a_ref[...], w_ref[...])

    @pl.when(kk == pl.num_programs(1) - 1)
    def _():
        o_ref[...] = _rms(o_ref[...], g_ref[...])


def _ffn_out(act, w_fo, h2, g_final):
    s_real, k = act.shape
    d = w_fo.shape[1]
    tm = FFN_OUT_ROW_TILE
    tk = k // FFN_OUT_K_STEPS
    assert tk % LANES == 0
    return pl.pallas_call(
        _ffn_out_kernel, name="ffn_out_final_norm",
        grid=(s_real // tm, k // tk),
        in_specs=[pl.BlockSpec((tm, tk), lambda m, kk: (m, kk)), pl.BlockSpec((tk, d), lambda m, kk: (kk, 0)),
                  pl.BlockSpec((tm, d), lambda m, kk: (m, 0)), pl.BlockSpec((1, d), lambda m, kk: (0, 0))],
        out_specs=pl.BlockSpec((tm, d), lambda m, kk: (m, 0)),
        out_shape=jax.ShapeDtypeStruct((s_real, d), F32),
        compiler_params=_cparams(2),
    )(act, w_fo, h2, g_final)


def kernel(x, positions, meta_tokens, w_in, w_q_up, w_kv_up, w_branch_mla, w_branch_hgrn, w_out, w_ffn_in,
           w_ffn_out, conv_w, conv_b, g_mix_norm, g_q_norm, g_kv_norm, g_hgrn_norm, g_ffn_norm, g_final_norm,
           lb_raw):
    b, s_real, d = x.shape
    assert b == 1 and w_in.shape[0] == 1 and s_real % max(ATTN_QUERY_TILE, FFN_ROW_TILE) == 0
    dt = x.dtype

    prefix = jnp.concatenate([jnp.zeros((PAD_LEN, d), dt), meta_tokens.astype(dt)], axis=0)
    pos = jnp.concatenate([positions[0].astype(jnp.int32) + N_META, jnp.zeros((PAD_LEN,), jnp.int32),
                           jnp.arange(N_META, dtype=jnp.int32)])
    inv = 1.0 / (ROPE_THETA ** (jnp.arange(0, QK_ROPE_DIM, 2, dtype=F32) / QK_ROPE_DIM))
    inv = jnp.concatenate([inv, inv, jnp.zeros((LANES - QK_ROPE_DIM,), F32)])[None, :]

    w_in_t = jnp.swapaxes(w_in, 1, 2)[0]
    w_lat = _w_lat(w_in_t)
    w_kv3 = w_kv_up[0].reshape(KV_LORA_RANK, MLA_HEADS, QK_NOPE_DIM + V_HEAD_DIM)
    w_k = w_kv3[:, :, :QK_NOPE_DIM].reshape(KV_LORA_RANK, -1).astype(BF16)
    w_vt = w_kv3[:, :, QK_NOPE_DIM:].reshape(KV_LORA_RANK, -1).T.astype(BF16)
    w_fo = w_ffn_out[0].astype(BF16)

    cos_t, sin_t = _rope_tables(pos.astype(F32)[:, None], inv)

    u, qn, kvn, kr = _proj_lat(x[0], prefix, g_mix_norm, w_lat, g_q_norm, g_kv_norm, cos_t, sin_t)
    hq = _matmul_cols_t(u, w_in_t, LAT_END, d, BF16, PROJ_COL_TILE, "proj_hgrn_q")
    hf = _matmul_cols_t(u, w_in_t, LAT_END + d, d, F32, PROJ_COL_TILE, "proj_hgrn_forget")
    hrest = _matmul_cols_t(u, w_in_t, LAT_END + 2 * d, 4 * d, BF16, PROJ_COL_TILE, "proj_hgrn_rest")
    q_cat = _q_up(qn, w_q_up, cos_t, sin_t)
    k_slabs = _kv_up(kvn, w_k)
    v_t = _v_up_t(kvn, w_vt)
    o_mla = _attention(q_cat, k_slabs, v_t, kr, s_real)
    o_hgrn = _hgrn(hq, hf, hrest, lb_raw, g_hgrn_norm)
    merged = _merge(o_mla, o_hgrn, w_branch_mla, w_branch_hgrn, hrest, 2 * d)
    h2, u2 = _out_proj(merged, w_out, x[0], prefix, g_ffn_norm)

    act = _ffn_in(u2, w_ffn_in, conv_w, conv_b, s_real)
    out = _ffn_out(act, w_fo, h2, g_final_norm[None, :])
    return out[None]
```
